```python
import math
import jax
import jax.numpy as jnp
from jax import lax
import numpy as np

D_MODEL = 2048
BATCH = 4
SEQ = 8192
DEPTH = 1
DEC_BATCH = 16
DEC_SEQ = 16
PAST_LEN = 4096

CHUNK = 64
N_META = 16
Q_BLOCK = 128
EPS = 1e-6
MIX_WIDTH = D_MODEL
DA_HEADS = 8
DA_D = 64
DA_V = 2 * DA_D
DA_QK_COLS = DA_HEADS * 2 * DA_D
DA_V_COLS = DA_HEADS * DA_V
MLA_HEADS = 8
MLA_NOPE = 128
MLA_ROPE = 64
MLA_V = 128
Q_LORA = 512
KV_LORA = 512
ROPE_THETA = 10000.0
MLA_SCALE = (MLA_NOPE + MLA_ROPE) ** -0.5
IN_COLS = 2 * DA_QK_COLS + DA_V_COLS + Q_LORA + KV_LORA + MLA_ROPE
N_GROUPS = 4
EXPERTS_PER_GROUP = 4
N_EXPERTS = N_GROUPS * EXPERTS_PER_GROUP
TOP_K = 2
D_EXPERT = 1024

kernel_name = 'hymba_diffattn_mla_hmoe_stream_step'


def rmsnorm(x, g):
    xf = x.astype(jnp.float32)
    y = xf * lax.rsqrt(jnp.mean(xf * xf, axis=-1, keepdims=True) + EPS)
    return (y * g.astype(jnp.float32)).astype(x.dtype)


def alibi_slopes():
    return 2.0 ** (-8.0 * jnp.arange(1, DA_HEADS + 1, dtype=jnp.float32) / DA_HEADS)


def rope(x, pos):
    half = MLA_ROPE // 2
    inv_freq = ROPE_THETA ** (-jnp.arange(half, dtype=jnp.float32) / half)
    ang = pos.astype(jnp.float32)[:, None] * inv_freq[None, :]
    cos = jnp.cos(ang)[:, None, :]
    sin = jnp.sin(ang)[:, None, :]
    xf = x.astype(jnp.float32)
    x1, x2 = xf[..., :half], xf[..., half:]
    return jnp.concatenate([x1 * cos - x2 * sin, x2 * cos + x1 * sin], axis=-1).astype(x.dtype)


def split_cols(z):
    o1 = DA_QK_COLS
    o2 = o1 + DA_QK_COLS
    o3 = o2 + DA_V_COLS
    o4 = o3 + Q_LORA
    o5 = o4 + KV_LORA
    return z[..., :o1], z[..., o1:o2], z[..., o2:o3], z[..., o3:o4], z[..., o4:o5], z[..., o5:]


def qk_heads(z):
    return z.reshape(z.shape[:2] + (DA_HEADS, 2, DA_D))


def v_heads(z):
    return z.reshape(z.shape[:2] + (DA_HEADS, DA_V))


def mla_queries(zcq, pos, q_norm_g, w_uq):
    q = rmsnorm(zcq, q_norm_g) @ w_uq
    q = q.reshape(q.shape[:2] + (MLA_HEADS, MLA_NOPE + MLA_ROPE))
    return q[..., :MLA_NOPE], rope(q[..., MLA_NOPE:], pos)


def rope_key(zkr, pos):
    return rope(zkr[:, :, None, :], pos)[:, :, 0, :]


def mla_expand(ckv, w_uk, w_uv):
    kn = jnp.einsum('blc,chd->blhd', ckv, w_uk.reshape(KV_LORA, MLA_HEADS, MLA_NOPE))
    v = jnp.einsum('blc,chd->blhd', ckv, w_uv.reshape(KV_LORA, MLA_HEADS, MLA_V))
    return kn, v


def lambda_init(layer):
    return 0.8 - 0.6 * math.exp(-0.3 * layer)


def diff_lambda(lq1, lk1, lq2, lk2, lam_init):
    f = lambda a: a.astype(jnp.float32)
    return jnp.exp(jnp.sum(f(lq1) * f(lk1))) - jnp.exp(jnp.sum(f(lq2) * f(lk2))) + lam_init


def diff_attend(q, q_pos, q_chunk, q_frame, k, v, k_pos, k_chunk, k_frame, lam, slopes):
    s = jnp.einsum('bqhcd,bkhcd->bhcqk', q, k).astype(jnp.float32) * (DA_D ** -0.5)
    dist = jnp.abs(q_pos[:, None] - k_pos[None, :]).astype(jnp.float32)
    both = q_frame[:, None] & k_frame[None, :]
    bias = jnp.where(both[None], -slopes[:, None, None] * dist[None], 0.0)
    visible = k_chunk[None, :] <= q_chunk[:, None]
    s = jnp.where(visible, s + bias[None, :, None], -jnp.inf)
    p = jax.nn.softmax(s, axis=-1)
    a = (p[:, :, 0] - lam * p[:, :, 1]).astype(v.dtype)
    return jnp.einsum('bhqk,bkhe->bqhe', a, v)


def mla_attend(qn, qr, q_chunk, kn, kr, v, k_chunk):
    s = (jnp.einsum('bqhd,bkhd->bhqk', qn, kn)
         + jnp.einsum('bqhr,bkr->bhqk', qr, kr)).astype(jnp.float32) * MLA_SCALE
    visible = k_chunk[None, :] <= q_chunk[:, None]
    p = jax.nn.softmax(jnp.where(visible, s, -jnp.inf), axis=-1)
    return jnp.einsum('bhqk,bkhe->bqhe', p.astype(v.dtype), v)


def sweep_query_blocks(fn, q_args, q_pos):
    nb = q_pos.shape[0] // Q_BLOCK
    blocks = tuple(jnp.moveaxis(a.reshape((a.shape[0], nb, Q_BLOCK) + a.shape[2:]), 1, 0) for a in q_args)
    out = lax.map(lambda t: fn(t[0], t[1]), (blocks, q_pos.reshape(nb, Q_BLOCK)))
    out = jnp.moveaxis(out, 0, 1)
    return out.reshape((out.shape[0], nb * Q_BLOCK) + out.shape[3:])


def merge_heads(o_diff, o_mla, lam_init, subln_g, w_o):
    b, n = o_diff.shape[:2]
    od = rmsnorm(o_diff, subln_g) * (1.0 - lam_init)
    cat = jnp.concatenate([od.reshape(b, n, DA_HEADS * DA_V), o_mla.reshape(b, n, MLA_HEADS * MLA_V)], axis=-1)
    return cat @ w_o


def hier_moe(x, rg_w, rg_b, re_w, re_b, w_gate, w_up, w_down):
    b, n, d = x.shape
    t = x.reshape(b * n, d)
    g_logits = (t @ rg_w + rg_b).astype(jnp.float32)
    p_g = jax.nn.softmax(g_logits, axis=-1)
    _, g_idx = lax.top_k(g_logits, 1)
    p_top = jnp.take_along_axis(p_g, g_idx, axis=-1)
    e_logits = (t @ re_w + re_b).astype(jnp.float32).reshape(b * n, N_GROUPS, EXPERTS_PER_GROUP)
    e_sel = jnp.take_along_axis(e_logits, g_idx[:, :, None], axis=1)[:, 0]
    vals, idx = lax.top_k(e_sel, TOP_K)
    w = jax.nn.softmax(vals, axis=-1) * p_top
    eid = g_idx * EXPERTS_PER_GROUP + idx
    gates = jnp.sum(w[..., None] * jax.nn.one_hot(eid, N_EXPERTS, dtype=jnp.float32), axis=1)
    y = jnp.zeros((b * n, d), jnp.float32)
    for e in range(N_EXPERTS):
        h = jax.nn.silu(t @ w_gate[e]) * (t @ w_up[e])
        y = y + gates[:, e:e + 1] * (h @ w_down[e]).astype(jnp.float32)
    return y.astype(x.dtype).reshape(b, n, d)


def setup_inputs(seed: int = 0) -> dict:
    key = jax.random.key(seed)
    ks = jax.random.split(key, 32)
    f32 = jnp.float32

    def nrm(k, shape, scale):
        return jax.random.normal(k, shape, f32) * scale

    def gain(k, shape):
        return 1.0 + 0.02 * jax.random.normal(k, shape, f32)

    lc = N_META + PAST_LEN
    return {
        'x_prompt': nrm(ks[0], (BATCH, SEQ, D_MODEL), 1.0),
        'x_sample': nrm(ks[1], (DEC_BATCH, DEC_SEQ, D_MODEL), 1.0),
        'cache_diff_k': nrm(ks[2], (DEPTH, DEC_BATCH, lc, DA_HEADS, 2 * DA_D), 1.0),
        'cache_diff_v': nrm(ks[3], (DEPTH, DEC_BATCH, lc, DA_HEADS, DA_V), 1.0),
        'cache_mla_ckv': nrm(ks[4], (DEPTH, DEC_BATCH, lc, KV_LORA), 1.0),
        'cache_mla_kr': nrm(ks[5], (DEPTH, DEC_BATCH, PAST_LEN, MLA_ROPE), 1.0),
        'meta_tokens': nrm(ks[6], (N_META, D_MODEL), 1.0),
        'norm1_g': gain(ks[7], (DEPTH, D_MODEL)),
        'w_in': nrm(ks[8], (DEPTH, D_MODEL, IN_COLS), D_MODEL ** -0.5),
        'diff_lam_q1': nrm(ks[9], (DEPTH, DA_D), 0.1),
        'diff_lam_k1': nrm(ks[10], (DEPTH, DA_D), 0.1),
        'diff_lam_q2': nrm(ks[11], (DEPTH, DA_D), 0.1),
        'diff_lam_k2': nrm(ks[12], (DEPTH, DA_D), 0.1),
        'diff_subln_g': gain(ks[13], (DEPTH, DA_V)),
        'mla_q_norm_g': gain(ks[14], (DEPTH, Q_LORA)),
        'mla_w_uq': nrm(ks[15], (DEPTH, Q_LORA, MLA_HEADS * (MLA_NOPE + MLA_ROPE)), Q_LORA ** -0.5),
        'mla_kv_norm_g': gain(ks[16], (DEPTH, KV_LORA)),
        'mla_w_uk': nrm(ks[17], (DEPTH, KV_LORA, MLA_HEADS * MLA_NOPE), KV_LORA ** -0.5),
        'mla_w_uv': nrm(ks[18], (DEPTH, KV_LORA, MLA_HEADS * MLA_V), KV_LORA ** -0.5),
        'w_o': nrm(ks[19], (DEPTH, MIX_WIDTH, D_MODEL), MIX_WIDTH ** -0.5),
        'norm2_g': gain(ks[20], (DEPTH, D_MODEL)),
        'router_group_w': nrm(ks[21], (DEPTH, D_MODEL, N_GROUPS), D_MODEL ** -0.5),
        'router_group_b': nrm(ks[22], (DEPTH, N_GROUPS), 0.01),
        'router_expert_w': nrm(ks[23], (DEPTH, D_MODEL, N_EXPERTS), D_MODEL ** -0.5),
        'router_expert_b': nrm(ks[24], (DEPTH, N_EXPERTS), 0.01),
        'expert_w_gate': nrm(ks[25], (DEPTH, N_EXPERTS, D_MODEL, D_EXPERT), D_MODEL ** -0.5),
        'expert_w_up': nrm(ks[26], (DEPTH, N_EXPERTS, D_MODEL, D_EXPERT), D_MODEL ** -0.5),
        'expert_w_down': nrm(ks[27], (DEPTH, N_EXPERTS, D_EXPERT, D_MODEL), D_EXPERT ** -0.5),
        'final_norm_g': gain(ks[28], (D_MODEL,)),
    }


def reference(x_prompt, x_sample, cache_diff_k, cache_diff_v, cache_mla_ckv, cache_mla_kr,
              meta_tokens, norm1_g, w_in, diff_lam_q1, diff_lam_k1, diff_lam_q2, diff_lam_k2,
              diff_subln_g, mla_q_norm_g, mla_w_uq, mla_kv_norm_g, mla_w_uk, mla_w_uv, w_o,
              norm2_g, router_group_w, router_group_b, router_expert_w, router_expert_b,
              expert_w_gate, expert_w_up, expert_w_down, final_norm_g):
    slopes = alibi_slopes()
    m_pos = jnp.zeros((N_META,), jnp.int32)
    m_chunk = jnp.full((N_META,), -1, jnp.int32)
    m_frame = jnp.zeros((N_META,), bool)
    f_pos = jnp.arange(SEQ, dtype=jnp.int32)
    pk_pos = jnp.concatenate([m_pos, f_pos])
    pk_chunk = jnp.concatenate([m_chunk, f_pos // CHUNK])
    pk_frame = jnp.concatenate([m_frame, jnp.ones((SEQ,), bool)])
    c_pos = jnp.arange(PAST_LEN, dtype=jnp.int32)
    s_pos = PAST_LEN + jnp.arange(DEC_SEQ, dtype=jnp.int32)
    sk_pos = jnp.concatenate([m_pos, c_pos, s_pos])
    sk_chunk = jnp.concatenate([m_chunk, c_pos // CHUNK, s_pos // CHUNK])
    sk_frame = jnp.concatenate([m_frame, jnp.ones((PAST_LEN + DEC_SEQ,), bool)])

    xp = x_prompt
    xs = x_sample
    bp = x_prompt.shape[0]
    bs = x_sample.shape[0]
    m = meta_tokens[None].astype(x_prompt.dtype)
    p_dk, p_dv, p_ckv, p_kr = [], [], [], []
    s_dk, s_dv, s_ckv, s_kr = [], [], [], []
    for l in range(DEPTH):
        lam_init = lambda_init(l)
        lam = diff_lambda(diff_lam_q1[l], diff_lam_k1[l], diff_lam_q2[l], diff_lam_k2[l], lam_init)

        zp = split_cols(rmsnorm(xp, norm1_g[l]) @ w_in[l])
        zm = split_cols(rmsnorm(m, norm1_g[l]) @ w_in[l])
        mdk, mdv = qk_heads(zm[1]), v_heads(zm[2])
        mckv = rmsnorm(zm[4], mla_kv_norm_g[l])
        pdq, pdk, pdv = qk_heads(zp[0]), qk_heads(zp[1]), v_heads(zp[2])
        pqn, pqr = mla_queries(zp[3], f_pos, mla_q_norm_g[l], mla_w_uq[l])
        pckv = rmsnorm(zp[4], mla_kv_norm_g[l])
        pkr = rope_key(zp[5], f_pos)
        bc = lambda a: jnp.broadcast_to(a, (bp,) + a.shape[1:])
        pdk_all = jnp.concatenate([bc(mdk), pdk], axis=1)
        pdv_all = jnp.concatenate([bc(mdv), pdv], axis=1)
        pckv_all = jnp.concatenate([bc(mckv), pckv], axis=1)
        pkn, pmv = mla_expand(pckv_all, mla_w_uk[l], mla_w_uv[l])
        pkr_all = jnp.concatenate([jnp.zeros((bp, N_META, MLA_ROPE), pkr.dtype), pkr], axis=1)

        def diff_block(qs, pos):
            return diff_attend(qs[0], pos, pos // CHUNK, jnp.ones(pos.shape, bool),
                               pdk_all, pdv_all, pk_pos, pk_chunk, pk_frame, lam, slopes)

        def mla_block(qs, pos):
            return mla_attend(qs[0], qs[1], pos // CHUNK, pkn, pkr_all, pmv, pk_chunk)

        od = sweep_query_blocks(diff_block, (pdq,), f_pos)
        om = sweep_query_blocks(mla_block, (pqn, pqr), f_pos)
        hp = xp + merge_heads(od, om, lam_init, diff_subln_g[l], w_o[l])
        xp = hp + hier_moe(rmsnorm(hp, norm2_g[l]), router_group_w[l], router_group_b[l],
                           router_expert_w[l], router_expert_b[l],
                           expert_w_gate[l], expert_w_up[l], expert_w_down[l])
        p_dk.append(pdk_all.reshape(bp, N_META + SEQ, DA_HEADS, 2 * DA_D))
        p_dv.append(pdv_all)
        p_ckv.append(pckv_all)
        p_kr.append(pkr)

        if l < DEPTH - 1:
            mdq = qk_heads(zm[0])
            mqn, mqr = mla_queries(zm[3], m_pos, mla_q_norm_g[l], mla_w_uq[l])
            mkn, mmv = mla_expand(mckv, mla_w_uk[l], mla_w_uv[l])
            mod = diff_attend(mdq, m_pos, m_chunk, m_frame, mdk, mdv, m_pos, m_chunk, m_frame, lam, slopes)
            mom = mla_attend(mqn, mqr, m_chunk, mkn, jnp.zeros((1, N_META, MLA_ROPE), m.dtype), mmv, m_chunk)
            mh = m + merge_heads(mod, mom, lam_init, diff_subln_g[l], w_o[l])
            m = mh + hier_moe(rmsnorm(mh, norm2_g[l]), router_group_w[l], router_group_b[l],
                              router_expert_w[l], router_expert_b[l],
                              expert_w_gate[l], expert_w_up[l], expert_w_down[l])

        zs = split_cols(rmsnorm(xs, norm1_g[l]) @ w_in[l])
        sdq, sdk, sdv = qk_heads(zs[0]), qk_heads(zs[1]), v_heads(zs[2])
        sqn, sqr = mla_queries(zs[3], s_pos, mla_q_norm_g[l], mla_w_uq[l])
        sckv = rmsnorm(zs[4], mla_kv_norm_g[l])
        skr = rope_key(zs[5], s_pos)
        sdk_all = jnp.concatenate([cache_diff_k[l].reshape(bs, N_META + PAST_LEN, DA_HEADS, 2, DA_D), sdk], axis=1)
        sdv_all = jnp.concatenate([cache_diff_v[l], sdv], axis=1)
        sckv_all = jnp.concatenate([cache_mla_ckv[l], sckv], axis=1)
        skn, smv = mla_expand(sckv_all, mla_w_uk[l], mla_w_uv[l])
        skr_all = jnp.concatenate([jnp.zeros((bs, N_META, MLA_ROPE), skr.dtype), cache_mla_kr[l], skr], axis=1)
        sod = diff_attend(sdq, s_pos, s_pos // CHUNK, jnp.ones((DEC_SEQ,), bool),
                          sdk_all, sdv_all, sk_pos, sk_chunk, sk_frame, lam, slopes)
        som = mla_attend(sqn, sqr, s_pos // CHUNK, skn, skr_all, smv, sk_chunk)
        hs = xs + merge_heads(sod, som, lam_init, diff_subln_g[l], w_o[l])
        xs = hs + hier_moe(rmsnorm(hs, norm2_g[l]), router_group_w[l], router_group_b[l],
                           router_expert_w[l], router_expert_b[l],
                           expert_w_gate[l], expert_w_up[l], expert_w_down[l])
        s_dk.append(sdk.reshape(bs, DEC_SEQ, DA_HEADS, 2 * DA_D))
        s_dv.append(sdv)
        s_ckv.append(sckv)
        s_kr.append(skr)

    y_prompt = rmsnorm(xp, final_norm_g)
    y_sample = rmsnorm(xs, final_norm_g)
    return (y_prompt, y_sample,
            jnp.stack(p_dk), jnp.stack(p_dv), jnp.stack(p_ckv), jnp.stack(p_kr),
            jnp.stack(s_dk), jnp.stack(s_dv), jnp.stack(s_ckv), jnp.stack(s_kr))
```

```python
import functools
import math

import numpy as np
import jax
import jax.numpy as jnp
from jax import lax
from jax.experimental import pallas as pl
from jax.experimental.pallas import tpu as pltpu

CHUNK = 64
N_META = 16
EPS = 1e-6
DA_HEADS = 8
DA_D = 64
DA_V = 2 * DA_D
MLA_HEADS = 8
MLA_NOPE = 128
MLA_ROPE = 64
MLA_V = 128
ROPE_THETA = 10000.0
MLA_SCALE = (MLA_NOPE + MLA_ROPE) ** -0.5
N_GROUPS = 4
EXPERTS_PER_GROUP = 4
N_EXPERTS = N_GROUPS * EXPERTS_PER_GROUP
LANES = 128
HEAD_W = 128
MLA_W = 256
NEG_BIG = -1e30
VMEM_LIMIT = 56 * 1024 * 1024

BF16 = jnp.bfloat16
F32 = jnp.float32


def _dot(a, b):
    return jnp.dot(a, b, preferred_element_type=F32)


def _dot_nt(a, b):
    return lax.dot_general(a, b, (((1,), (1,)), ((), ())), preferred_element_type=F32)


def _rms(x, g):
    return x * lax.rsqrt(jnp.mean(x * x, axis=-1, keepdims=True) + EPS) * g


def _cparams(n_axes):
    return pltpu.CompilerParams(dimension_semantics=("arbitrary",) * n_axes,
                                vmem_limit_bytes=VMEM_LIMIT)


def _const_spec(shape):
    nd = len(shape)
    return pl.BlockSpec(shape, lambda *_: (0,) * nd, pipeline_mode=pl.Buffered(1))


def _inproj_kernel(x_ref, tab_ref, g1_ref, win_ref, qg_ref, wuq_ref, kvg_ref, wukv_ref,
                   qd_ref, kd32_ref, kdb_ref, vd32_ref, vdb_ref, qm_ref, ckv_ref, kr_ref,
                   km_ref, vm_ref, *, c_qk, c_v, c_ql, c_kvl):
    x = x_ref[...]
    xn = _rms(x, g1_ref[...]).astype(BF16)
    tab = tab_ref[...]
    o1 = c_qk
    o2 = o1 + c_qk
    o3 = o2 + c_v
    o4 = o3 + c_ql
    o5 = o4 + c_kvl

    zq = _dot(xn, win_ref[:, 0:o1])
    qd_ref[...] = (zq * (DA_D ** -0.5)).astype(BF16)
    zk = _dot(xn, win_ref[:, o1:o2])
    kd32_ref[...] = zk
    kdb_ref[...] = zk.astype(BF16)
    zv = _dot(xn, win_ref[:, o2:o3])
    vd32_ref[...] = zv
    vdb_ref[...] = zv.astype(BF16)

    cq = _rms(_dot(xn, win_ref[:, o3:o4]), qg_ref[...]).astype(BF16)
    nq = MLA_HEADS * MLA_NOPE
    qn = _dot(cq, wuq_ref[:, 0:nq]) * MLA_SCALE
    qr = _dot(cq, wuq_ref[:, nq:2 * nq]) * MLA_SCALE
    for h in range(MLA_HEADS):
        qm_ref[:, h * MLA_W:h * MLA_W + LANES] = qn[:, h * LANES:(h + 1) * LANES].astype(BF16)
        u = qr[:, h * LANES:(h + 1) * LANES] * tab
        qm_ref[:, h * MLA_W + LANES:(h + 1) * MLA_W] = (u + pltpu.roll(u, MLA_ROPE, 1)).astype(BF16)

    ckv = _rms(_dot(xn, win_ref[:, o4:o5]), kvg_ref[...])
    ckv_ref[...] = ckv
    ckvb = ckv.astype(BF16)
    nk = MLA_HEADS * MLA_NOPE
    kn = _dot(ckvb, wukv_ref[:, 0:nk])
    vm_ref[...] = _dot(ckvb, wukv_ref[:, nk:nk + MLA_HEADS * MLA_V]).astype(BF16)

    u = _dot(xn, win_ref[:, o5:o5 + LANES]) * tab
    rot = u + pltpu.roll(u, MLA_ROPE, 1)
    kr_ref[...] = rot[:, 0:MLA_ROPE]
    lane = lax.broadcasted_iota(jnp.int32, rot.shape, 1)
    krp = jnp.where(lane < MLA_ROPE, rot, 0.0).astype(BF16)
    for h in range(MLA_HEADS):
        km_ref[:, h * MLA_W:h * MLA_W + LANES] = kn[:, h * LANES:(h + 1) * LANES].astype(BF16)
        km_ref[:, h * MLA_W + LANES:(h + 1) * MLA_W] = krp


def _inproj(x, tab, g1, win, qg, wuq, kvg, wukv, *, tm, tab_blocks, dims):
    m, d = x.shape
    c_qk, c_v, c_ql, c_kvl = dims
    assert m % tm == 0
    row = lambda w: pl.BlockSpec((tm, w), lambda i: (i, 0))
    outs = [
        (c_qk, BF16), (c_qk, F32), (c_qk, BF16), (c_v, F32), (c_v, BF16),
        (MLA_HEADS * MLA_W, BF16), (c_kvl, F32), (MLA_ROPE, F32),
        (MLA_HEADS * MLA_W, BF16), (MLA_HEADS * MLA_V, BF16),
    ]
    return pl.pallas_call(
        functools.partial(_inproj_kernel, c_qk=c_qk, c_v=c_v, c_ql=c_ql, c_kvl=c_kvl),
        grid=(m // tm,),
        in_specs=[
            row(d),
            pl.BlockSpec((tm, LANES), lambda i: (i % tab_blocks, 0)),
            _const_spec(g1.shape), _const_spec(win.shape), _const_spec(qg.shape),
            _const_spec(wuq.shape), _const_spec(kvg.shape), _const_spec(wukv.shape),
        ],
        out_specs=[row(w) for w, _ in outs],
        out_shape=[jax.ShapeDtypeStruct((m, w), dt) for w, dt in outs],
        compiler_params=_cparams(1),
        name="inproj",
    )(x, tab, g1, win, qg, wuq, kvg, wukv)


def _softmax_seed(s, v, m_ref, l_ref, acc_ref, j):
    m = jnp.max(s, axis=1, keepdims=True)
    p = jnp.exp(s - m)
    m_ref[j] = m
    l_ref[j] = jnp.sum(p, axis=1, keepdims=True)
    acc_ref[j] = _dot(p.astype(BF16), v)


def _softmax_step(s, v, m_ref, l_ref, acc_ref, j):
    m_old = m_ref[j]
    m_new = jnp.maximum(m_old, jnp.max(s, axis=1, keepdims=True))
    alpha = jnp.exp(m_old - m_new)
    p = jnp.exp(s - m_new)
    l_ref[j] = alpha * l_ref[j] + jnp.sum(p, axis=1, keepdims=True)
    acc_ref[j] = alpha * acc_ref[j] + _dot(p.astype(BF16), v)
    m_ref[j] = m_new


def _diff_lambda(lamv, lam_init):
    a = jnp.sum(lamv[0:1] * lamv[1:2], axis=1, keepdims=True)
    b = jnp.sum(lamv[2:3] * lamv[3:4], axis=1, keepdims=True)
    return jnp.exp(a) - jnp.exp(b) + lam_init


def _diff_finish(m_ref, l_ref, acc_ref, h, lam, g, lam_init):
    o = acc_ref[2 * h] / l_ref[2 * h] - lam * (acc_ref[2 * h + 1] / l_ref[2 * h + 1])
    return (_rms(o, g) * (1.0 - lam_init)).astype(BF16)


def _split_maps(q):
    lane = lax.broadcasted_iota(jnp.int32, q.shape, 1)
    zero = jnp.zeros_like(q)
    return jnp.where(lane < DA_D, q, zero), jnp.where(lane >= DA_D, q, zero)


def _diff_attn_kernel(qi_ref, ki_ref, q_ref, k_ref, v_ref, mk_ref, mv_ref, dist_ref, lamv_ref,
                      g_ref, o_ref, m_s, l_s, acc_s, *, tq, lam_init):
    t = pl.program_id(1)
    qi = qi_ref[t]
    ki = ki_ref[t]

    @pl.when(ki == 0)
    def _seed():
        for h in range(DA_HEADS):
            hs = slice(h * HEAD_W, (h + 1) * HEAD_W)
            kk = mk_ref[:, hs]
            vv = mv_ref[:, hs]
            for c, qc in enumerate(_split_maps(q_ref[:, hs])):
                _softmax_seed(_dot_nt(qc, kk), vv, m_s, l_s, acc_s, 2 * h + c)

    dist = dist_ref[...] + ((qi - ki) * tq).astype(F32)
    for h in range(DA_HEADS):
        hs = slice(h * HEAD_W, (h + 1) * HEAD_W)
        bias = dist * (-(2.0 ** (-8.0 * (h + 1) / DA_HEADS)))
        kk = k_ref[:, hs]
        vv = v_ref[:, hs]
        for c, qc in enumerate(_split_maps(q_ref[:, hs])):
            _softmax_step(_dot_nt(qc, kk) + bias, vv, m_s, l_s, acc_s, 2 * h + c)

    @pl.when(ki == qi)
    def _finish():
        lam = _diff_lambda(lamv_ref[...], lam_init)
        g = g_ref[...]
        for h in range(DA_HEADS):
            o_ref[:, h * HEAD_W:(h + 1) * HEAD_W] = _diff_finish(m_s, l_s, acc_s, h, lam, g, lam_init)


def _mla_attn_kernel(qi_ref, ki_ref, q_ref, k_ref, v_ref, mk_ref, mv_ref, mask_ref, o_ref,
                     m_s, l_s, acc_s):
    t = pl.program_id(1)
    qi = qi_ref[t]
    ki = ki_ref[t]

    @pl.when(ki == 0)
    def _seed():
        for h in range(MLA_HEADS):
            s = _dot_nt(q_ref[:, h * MLA_W:(h + 1) * MLA_W], mk_ref[:, h * MLA_W:(h + 1) * MLA_W])
            _softmax_seed(s, mv_ref[:, h * MLA_V:(h + 1) * MLA_V], m_s, l_s, acc_s, h)

    mask = mask_ref[...]
    for h in range(MLA_HEADS):
        s = _dot_nt(q_ref[:, h * MLA_W:(h + 1) * MLA_W], k_ref[:, h * MLA_W:(h + 1) * MLA_W]) + mask
        _softmax_step(s, v_ref[:, h * MLA_V:(h + 1) * MLA_V], m_s, l_s, acc_s, h)

    @pl.when(ki == qi)
    def _finish():
        for h in range(MLA_HEADS):
            o_ref[:, h * MLA_V:(h + 1) * MLA_V] = (acc_s[h] / l_s[h]).astype(BF16)


def _pair_tables(nq):
    qi = np.concatenate([np.full((i + 1,), i, np.int32) for i in range(nq)])
    ki = np.concatenate([np.arange(i + 1, dtype=np.int32) for i in range(nq)])
    return jnp.asarray(qi), jnp.asarray(ki)


def _tile_geometry(tq):
    i = np.arange(tq)[:, None]
    j = np.arange(tq)[None, :]
    visible = (j // CHUNK) <= (i // CHUNK)
    return i, j, visible


def _prompt_attn_specs(tq, wq, wk, wv):
    q_spec = pl.BlockSpec((None, tq, wq), lambda b, t, qi, ki: (b, qi[t], 0))
    k_spec = pl.BlockSpec((None, tq, wk), lambda b, t, qi, ki: (b, ki[t], 0))
    v_spec = pl.BlockSpec((None, tq, wv), lambda b, t, qi, ki: (b, ki[t], 0))
    mk_spec = pl.BlockSpec((N_META, wk), lambda b, t, qi, ki: (0, 0))
    mv_spec = pl.BlockSpec((N_META, wv), lambda b, t, qi, ki: (0, 0))
    tile_spec = pl.BlockSpec((None, tq, tq), lambda b, t, qi, ki: (jnp.where(qi[t] == ki[t], 1, 0), 0, 0))
    o_spec = pl.BlockSpec((None, tq, wv), lambda b, t, qi, ki: (b, qi[t], 0))
    return q_spec, k_spec, v_spec, mk_spec, mv_spec, tile_spec, o_spec


def _diff_attn(q, k, v, mk, mv, lamv, g, *, tq, lam_init):
    b, s, w = q.shape
    nq = s // tq
    qi, ki = _pair_tables(nq)
    i, j, visible = _tile_geometry(tq)
    dist = jnp.asarray(np.stack([(i - j).astype(np.float32),
                                 np.where(visible, np.abs(i - j), 1e30).astype(np.float32)]))
    q_spec, k_spec, v_spec, mk_spec, mv_spec, tile_spec, o_spec = _prompt_attn_specs(tq, w, w, w)
    full = lambda a: pl.BlockSpec(a.shape, lambda b_, t, qi_, ki_: (0,) * a.ndim)
    return pl.pallas_call(
        functools.partial(_diff_attn_kernel, tq=tq, lam_init=lam_init),
        grid_spec=pltpu.PrefetchScalarGridSpec(
            num_scalar_prefetch=2,
            grid=(b, int(qi.shape[0])),
            in_specs=[q_spec, k_spec, v_spec, mk_spec, mv_spec, tile_spec, full(lamv), full(g)],
            out_specs=o_spec,
            scratch_shapes=[pltpu.VMEM((2 * DA_HEADS, tq, 1), F32), pltpu.VMEM((2 * DA_HEADS, tq, 1), F32),
                            pltpu.VMEM((2 * DA_HEADS, tq, DA_V), F32)],
        ),
        out_shape=jax.ShapeDtypeStruct((b, s, w), BF16),
        compiler_params=_cparams(2),
        name="diff_attn",
    )(qi, ki, q, k, v, mk, mv, dist, lamv, g)


def _mla_attn(q, k, v, mk, mv, *, tq):
    b, s, wq = q.shape
    wv = v.shape[-1]
    nq = s // tq
    qi, ki = _pair_tables(nq)
    _, _, visible = _tile_geometry(tq)
    mask = jnp.asarray(np.stack([np.zeros((tq, tq), np.float32),
                                 np.where(visible, 0.0, NEG_BIG).astype(np.float32)]))
    q_spec, k_spec, v_spec, mk_spec, mv_spec, tile_spec, o_spec = _prompt_attn_specs(tq, wq, wq, wv)
    return pl.pallas_call(
        _mla_attn_kernel,
        grid_spec=pltpu.PrefetchScalarGridSpec(
            num_scalar_prefetch=2,
            grid=(b, int(qi.shape[0])),
            in_specs=[q_spec, k_spec, v_spec, mk_spec, mv_spec, tile_spec],
            out_specs=o_spec,
            scratch_shapes=[pltpu.VMEM((MLA_HEADS, tq, 1), F32), pltpu.VMEM((MLA_HEADS, tq, 1), F32),
                            pltpu.VMEM((MLA_HEADS, tq, MLA_V), F32)],
        ),
        out_shape=jax.ShapeDtypeStruct((b, s, wv), BF16),
        compiler_params=_cparams(2),
        name="mla_attn",
    )(qi, ki, q, k, v, mk, mv, mask)


def _sample_diff_kernel(q_ref, kc_ref, vc_ref, kn_ref, vn_ref, dc_ref, dn_ref, lamv_ref, g_ref,
                        o_ref, m_s, l_s, acc_s, *, lam_init):
    kt = pl.program_id(1)
    nkt = pl.num_programs(1)

    def heads(kk_ref, vv_ref, dist, first):
        for h in range(DA_HEADS):
            hs = slice(h * HEAD_W, (h + 1) * HEAD_W)
            q1, q2 = _split_maps(q_ref[:, hs])
            qq = jnp.concatenate([q1, q2], axis=0)
            s = _dot_nt(qq, kk_ref[:, hs].astype(BF16)) + dist * (-(2.0 ** (-8.0 * (h + 1) / DA_HEADS)))
            vv = vv_ref[:, hs].astype(BF16)
            if first:
                _softmax_seed(s, vv, m_s, l_s, acc_s, h)
            else:
                _softmax_step(s, vv, m_s, l_s, acc_s, h)

    @pl.when(kt == 0)
    def _first():
        heads(kc_ref, vc_ref, dc_ref[...], True)

    @pl.when(kt > 0)
    def _rest():
        heads(kc_ref, vc_ref, dc_ref[...], False)

    @pl.when(kt == nkt - 1)
    def _finish():
        heads(kn_ref, vn_ref, dn_ref[...], False)
        lam = _diff_lambda(lamv_ref[...], lam_init)
        g = g_ref[...]
        nq = q_ref.shape[0]
        for h in range(DA_HEADS):
            a = acc_s[h] / l_s[h]
            o = a[0:nq] - lam * a[nq:2 * nq]
            o_ref[:, h * HEAD_W:(h + 1) * HEAD_W] = (_rms(o, g) * (1.0 - lam_init)).astype(BF16)


def _sample_diff_attn(q, kc, vc, kn, vn, dist_c, dist_n, lamv, g, *, tk, lam_init):
    bs, nq, w = q.shape
    lc = kc.shape[1]
    assert lc % tk == 0
    full = lambda a: pl.BlockSpec(a.shape, lambda b, t: (0,) * a.ndim)
    per_stream = lambda a: pl.BlockSpec((None,) + a.shape[1:], lambda b, t: (b,) + (0,) * (a.ndim - 1))
    return pl.pallas_call(
        functools.partial(_sample_diff_kernel, lam_init=lam_init),
        grid=(bs, lc // tk),
        in_specs=[per_stream(q),
                  pl.BlockSpec((None, tk, w), lambda b, t: (b, t, 0)),
                  pl.BlockSpec((None, tk, w), lambda b, t: (b, t, 0)),
                  per_stream(kn), per_stream(vn),
                  pl.BlockSpec((None, 2 * nq, tk), lambda b, t: (t, 0, 0)),
                  full(dist_n), full(lamv), full(g)],
        out_specs=per_stream(q),
        out_shape=jax.ShapeDtypeStruct((bs, nq, w), BF16),
        scratch_shapes=[pltpu.VMEM((DA_HEADS, 2 * nq, 1), F32), pltpu.VMEM((DA_HEADS, 2 * nq, 1), F32),
                        pltpu.VMEM((DA_HEADS, 2 * nq, DA_V), F32)],
        compiler_params=_cparams(2),
        name="sample_diff_attn",
    )(q, kc, vc, kn, vn, dist_c, dist_n, lamv, g)


def _sample_mla_kernel(q_ref, cc_ref, krc_ref, cn_ref, krn_ref, wuk_ref, wuv_ref, o_ref,
                       ql_s, qr_s, m_s, l_s, acc_s):
    kt = pl.program_id(1)
    nkt = pl.num_programs(1)
    nq = q_ref.shape[0]

    @pl.when(kt == 0)
    def _prep():
        for h in range(MLA_HEADS):
            qn = q_ref[:, h * MLA_W:h * MLA_W + MLA_NOPE]
            ql_s[h * nq:(h + 1) * nq, :] = _dot_nt(qn, wuk_ref[:, h * MLA_NOPE:(h + 1) * MLA_NOPE]).astype(BF16)
            qr_s[h * nq:(h + 1) * nq, :] = q_ref[:, h * MLA_W + MLA_NOPE:(h + 1) * MLA_W]

    def scores(c_ref, kr_ref):
        cb = c_ref[...].astype(BF16)
        krb = kr_ref[...].astype(BF16)
        s = _dot_nt(ql_s[...], cb) + _dot_nt(qr_s[:, 0:MLA_ROPE], krb)
        return s, cb

    @pl.when(kt == 0)
    def _first():
        s, cb = scores(cc_ref, krc_ref)
        _softmax_seed(s, cb, m_s, l_s, acc_s, 0)

    @pl.when(kt > 0)
    def _rest():
        s, cb = scores(cc_ref, krc_ref)
        _softmax_step(s, cb, m_s, l_s, acc_s, 0)

    @pl.when(kt == nkt - 1)
    def _finish():
        s, cb = scores(cn_ref, krn_ref)
        _softmax_step(s, cb, m_s, l_s, acc_s, 0)
        ol = (acc_s[0] / l_s[0]).astype(BF16)
        for h in range(MLA_HEADS):
            o_ref[:, h * MLA_V:(h + 1) * MLA_V] = _dot(
                ol[h * nq:(h + 1) * nq, :], wuv_ref[:, h * MLA_V:(h + 1) * MLA_V]).astype(BF16)


def _sample_mla_attn(q, cc, krc, cn, krn, wuk, wuv, *, tk):
    bs, nq, wq = q.shape
    lc, kvl = cc.shape[1], cc.shape[2]
    assert lc % tk == 0
    full = lambda a: pl.BlockSpec(a.shape, lambda b, t: (0,) * a.ndim)
    per_stream = lambda a: pl.BlockSpec((None,) + a.shape[1:], lambda b, t: (b,) + (0,) * (a.ndim - 1))
    rows = MLA_HEADS * nq
    return pl.pallas_call(
        _sample_mla_kernel,
        grid=(bs, lc // tk),
        in_specs=[per_stream(q),
                  pl.BlockSpec((None, tk, kvl), lambda b, t: (b, t, 0)),
                  pl.BlockSpec((None, tk, MLA_ROPE), lambda b, t: (b, t, 0)),
                  per_stream(cn), per_stream(krn), full(wuk), full(wuv)],
        out_specs=pl.BlockSpec((None, nq, MLA_HEADS * MLA_V), lambda b, t: (b, 0, 0)),
        out_shape=jax.ShapeDtypeStruct((bs, nq, MLA_HEADS * MLA_V), BF16),
        scratch_shapes=[pltpu.VMEM((rows, kvl), BF16), pltpu.VMEM((rows, LANES), BF16),
                        pltpu.VMEM((1, rows, 1), F32), pltpu.VMEM((1, rows, 1), F32),
                        pltpu.VMEM((1, rows, kvl), F32)],
        compiler_params=_cparams(2),
        name="sample_mla_attn",
    )(q, cc, krc, cn, krn, wuk, wuv)


ROUTER_ROWS = 8 * (1 + N_GROUPS)


def _route(lt):
    g = [lt[i:i + 1] for i in range(N_GROUPS)]
    gmax = functools.reduce(jnp.maximum, g)
    gidx = jnp.full_like(gmax, float(N_GROUPS - 1))
    for i in range(N_GROUPS - 2, -1, -1):
        gidx = jnp.where(g[i] == gmax, float(i), gidx)
    den = functools.reduce(lambda a, b: a + b, [jnp.exp(gi - gmax) for gi in g])
    p_top = 1.0 / den
    e = []
    for j in range(EXPERTS_PER_GROUP):
        ej = lt[8 * N_GROUPS + j:8 * N_GROUPS + j + 1]
        for grp in range(N_GROUPS - 2, -1, -1):
            ej = jnp.where(gidx == float(grp), lt[8 * (grp + 1) + j:8 * (grp + 1) + j + 1], ej)
        e.append(ej)

    def first_argmax(vals):
        vmax = functools.reduce(jnp.maximum, vals)
        idx = jnp.full_like(vmax, float(len(vals) - 1))
        for i in range(len(vals) - 2, -1, -1):
            idx = jnp.where(vals[i] == vmax, float(i), idx)
        return vmax, idx

    v1, i1 = first_argmax(e)
    rest = [jnp.where(i1 == float(j), -jnp.inf, e[j]) for j in range(EXPERTS_PER_GROUP)]
    v2, i2 = first_argmax(rest)
    r = jnp.exp(v2 - v1)
    w1 = p_top / (1.0 + r)
    w2 = p_top * r / (1.0 + r)
    base = gidx * float(EXPERTS_PER_GROUP)
    return w1, w2, base + i1, base + i2


def _merge_kernel(od_ref, om_ref, x_ref, wo_ref, g2_ref, wr_ref, br_ref, hp_ref, xn_ref, rt_ref):
    nd = od_ref.shape[1]
    y = _dot(od_ref[...], wo_ref[0:nd, :]) + _dot(om_ref[...], wo_ref[nd:, :])
    hp = x_ref[...] + y
    hp_ref[...] = hp
    xn = _rms(hp, g2_ref[...]).astype(BF16)
    xn_ref[...] = xn
    lt = _dot_nt(wr_ref[...], xn) + br_ref[...]
    rows = _route(lt)
    for i, r in enumerate(rows):
        rt_ref[i:i + 1, :] = r
    rt_ref[4:8, :] = jnp.zeros((4, rt_ref.shape[1]), F32)


def _merge(od, om, x, wo, g2, wr, br, *, tm):
    m, d = x.shape
    row = lambda w: pl.BlockSpec((tm, w), lambda i: (i, 0))
    return pl.pallas_call(
        _merge_kernel,
        grid=(m // tm,),
        in_specs=[row(od.shape[1]), row(om.shape[1]), row(d), _const_spec(wo.shape),
                  _const_spec(g2.shape), _const_spec(wr.shape), _const_spec(br.shape)],
        out_specs=[row(d), row(d), pl.BlockSpec((8, tm), lambda i: (0, i))],
        out_shape=[jax.ShapeDtypeStruct((m, d), F32), jax.ShapeDtypeStruct((m, d), BF16),
                   jax.ShapeDtypeStruct((8, m), F32)],
        compiler_params=_cparams(1),
        name="merge",
    )(od, om, x, wo, g2, wr, br)


def _swiglu(x, wg, wu, wd):
    g = _dot(x, wg)
    u = _dot(x, wu)
    h = (g * jax.nn.sigmoid(g) * u).astype(BF16)
    return _dot(h, wd)


def _moe_sorted_kernel(te_ref, nu_ref, x_ref, w_ref, wg_ref, wu_ref, wd_ref, y_ref):
    i = pl.program_id(0)

    @pl.when(i < nu_ref[0])
    def _():
        y_ref[...] = w_ref[...] * _swiglu(x_ref[...], wg_ref[...], wu_ref[...], wd_ref[...])

    @pl.when(i >= nu_ref[0])
    def _():
        y_ref[...] = jnp.zeros_like(y_ref)


def _moe_sorted(tile_expert, n_used, xs, ws, wg, wu, wd, *, tm):
    n, d = xs.shape
    f = wg.shape[2]
    nt = n // tm
    return pl.pallas_call(
        _moe_sorted_kernel,
        grid_spec=pltpu.PrefetchScalarGridSpec(
            num_scalar_prefetch=2,
            grid=(nt,),
            in_specs=[pl.BlockSpec((tm, d), lambda i, te, nu: (i, 0)),
                      pl.BlockSpec((tm, 1), lambda i, te, nu: (i, 0)),
                      pl.BlockSpec((None, d, f), lambda i, te, nu: (te[i], 0, 0)),
                      pl.BlockSpec((None, d, f), lambda i, te, nu: (te[i], 0, 0)),
                      pl.BlockSpec((None, f, d), lambda i, te, nu: (te[i], 0, 0))],
            out_specs=pl.BlockSpec((tm, d), lambda i, te, nu: (i, 0)),
        ),
        out_shape=jax.ShapeDtypeStruct((n, d), F32),
        compiler_params=_cparams(1),
        name="moe_sorted",
    )(tile_expert, n_used, xs, ws, wg, wu, wd)


def _moe_dense_kernel(x_ref, hp_ref, gates_ref, wg_ref, wu_ref, wd_ref, gf_ref, o_ref, acc_s):
    e = pl.program_id(0)

    @pl.when(e == 0)
    def _():
        acc_s[...] = jnp.zeros_like(acc_s)

    lane = lax.broadcasted_iota(jnp.int32, gates_ref.shape, 1)
    gate = jnp.sum(jnp.where(lane == e, gates_ref[...], 0.0), axis=1, keepdims=True)
    acc_s[...] += gate * _swiglu(x_ref[...], wg_ref[...], wu_ref[...], wd_ref[...])

    @pl.when(e == pl.num_programs(0) - 1)
    def _():
        o_ref[...] = _rms(hp_ref[...] + acc_s[...], gf_ref[...])


def _moe_dense(xn, hp, gates, wg, wu, wd, gf):
    m, d = xn.shape
    ne, _, f = wg.shape
    full = lambda a: pl.BlockSpec(a.shape, lambda e: (0,) * a.ndim)
    return pl.pallas_call(
        _moe_dense_kernel,
        grid=(ne,),
        in_specs=[full(xn), full(hp), full(gates),
                  pl.BlockSpec((None, d, f), lambda e: (e, 0, 0)),
                  pl.BlockSpec((None, d, f), lambda e: (e, 0, 0)),
                  pl.BlockSpec((None, f, d), lambda e: (e, 0, 0)),
                  full(gf)],
        out_specs=full(hp),
        out_shape=jax.ShapeDtypeStruct((m, d), F32),
        scratch_shapes=[pltpu.VMEM((m, d), F32)],
        compiler_params=_cparams(1),
        name="moe_dense",
    )(xn, hp, gates, wg, wu, wd, gf)


def _combine_kernel(hp_ref, y1_ref, y2_ref, gf_ref, o_ref):
    o_ref[...] = _rms(hp_ref[...] + (y1_ref[...] + y2_ref[...]), gf_ref[...])


def _combine(hp, y1, y2, gf, *, tm):
    m, d = hp.shape
    row = pl.BlockSpec((tm, d), lambda i: (i, 0))
    return pl.pallas_call(
        _combine_kernel,
        grid=(m // tm,),
        in_specs=[row, row, row, _const_spec(gf.shape)],
        out_specs=row,
        out_shape=jax.ShapeDtypeStruct((m, d), F32),
        compiler_params=_cparams(1),
        name="combine",
    )(hp, y1, y2, gf)


def _rope_table(pos):
    half = MLA_ROPE // 2
    inv_freq = ROPE_THETA ** (-jnp.arange(half, dtype=F32) / half)
    ang = pos.astype(F32)[:, None] * inv_freq[None, :]
    c, s = jnp.cos(ang), jnp.sin(ang)
    return jnp.concatenate([c, c, -s, s], axis=1)


def _swap_halves(w):
    half = MLA_ROPE // 2
    return jnp.concatenate([w[..., half:], w[..., :half]], axis=-1)


def _sort_by_expert(eid, w, tm):
    t = eid.shape[1]
    flat_e = eid.reshape(-1)
    onehot = (flat_e[:, None] == jnp.arange(N_EXPERTS, dtype=jnp.int32)[None, :]).astype(jnp.int32)
    rank = jnp.sum((jnp.cumsum(onehot, axis=0) - onehot) * onehot, axis=1)
    counts = jnp.sum(onehot, axis=0)
    tiles_per = (counts + tm - 1) // tm
    tiles_end = jnp.cumsum(tiles_per)
    row_start = (tiles_end - tiles_per) * tm
    pos = row_start[flat_e] + rank
    n_tiles = (2 * t) // tm + N_EXPERTS
    tok = jnp.tile(jnp.arange(t, dtype=jnp.int32), 2)
    sorted_tok = jnp.zeros((n_tiles * tm,), jnp.int32).at[pos].set(tok)
    sorted_w = jnp.zeros((n_tiles * tm,), F32).at[pos].set(w.reshape(-1))
    tile_expert = jnp.minimum(
        jnp.searchsorted(tiles_end, jnp.arange(n_tiles, dtype=jnp.int32), side="right"),
        N_EXPERTS - 1).astype(jnp.int32)
    n_used = tiles_end[-1:].astype(jnp.int32)
    return pos.reshape(2, t), sorted_tok, sorted_w, tile_expert, n_used


def kernel(x_prompt, x_sample, cache_diff_k, cache_diff_v, cache_mla_ckv, cache_mla_kr, meta_tokens, norm1_g, w_in, diff_lam_q1, diff_lam_k1, diff_lam_q2, diff_lam_k2, diff_subln_g, mla_q_norm_g, mla_w_uq, mla_kv_norm_g, mla_w_uk, mla_w_uv, w_o, norm2_g, router_group_w, router_group_b, router_expert_w, router_expert_b, expert_w_gate, expert_w_up, expert_w_down, final_norm_g):
    depth = norm1_g.shape[0]
    assert depth == 1, "single-layer step only"
    lam_init = 0.8 - 0.6 * math.exp(-0.3 * 0)
    b, s, d = x_prompt.shape
    bs, ss, _ = x_sample.shape
    past = cache_mla_kr.shape[2]
    lc = N_META + past
    c_qk = DA_HEADS * 2 * DA_D
    c_v = DA_HEADS * DA_V
    c_ql = mla_q_norm_g.shape[1]
    c_kvl = mla_kv_norm_g.shape[1]
    o5 = 2 * c_qk + c_v + c_ql + c_kvl

    win = w_in[0]
    win_ext = jnp.concatenate([win, _swap_halves(win[:, o5:])], axis=1).astype(BF16)
    wuq = mla_w_uq[0].reshape(c_ql, MLA_HEADS, MLA_NOPE + MLA_ROPE)
    wuq_n = wuq[:, :, :MLA_NOPE].reshape(c_ql, MLA_HEADS * MLA_NOPE)
    wuq_r = jnp.concatenate([wuq[:, :, MLA_NOPE:], _swap_halves(wuq[:, :, MLA_NOPE:])], axis=2)
    wuq_ext = jnp.concatenate([wuq_n, wuq_r.reshape(c_ql, MLA_HEADS * LANES)], axis=1).astype(BF16)
    wuk = mla_w_uk[0].astype(BF16)
    wuv = mla_w_uv[0].astype(BF16)
    wukv = jnp.concatenate([wuk, wuv], axis=1)
    g1 = norm1_g
    qg = mla_q_norm_g
    kvg = mla_kv_norm_g
    wo = w_o[0].astype(BF16)
    wr = jnp.zeros((ROUTER_ROWS, d), F32).at[0:N_GROUPS].set(router_group_w[0].T)
    br = jnp.zeros((ROUTER_ROWS, 1), F32).at[0:N_GROUPS, 0].set(router_group_b[0])
    rew = router_expert_w[0].T.reshape(N_GROUPS, EXPERTS_PER_GROUP, d)
    reb = router_expert_b[0].reshape(N_GROUPS, EXPERTS_PER_GROUP)
    for grp in range(N_GROUPS):
        wr = wr.at[8 * (grp + 1):8 * (grp + 1) + EXPERTS_PER_GROUP].set(rew[grp])
        br = br.at[8 * (grp + 1):8 * (grp + 1) + EXPERTS_PER_GROUP, 0].set(reb[grp])
    wr = wr.astype(BF16)
    wg = expert_w_gate[0].astype(BF16)
    wu = expert_w_up[0].astype(BF16)
    wd = expert_w_down[0].astype(BF16)
    gf = final_norm_g[None, :]
    lamv = jnp.stack([diff_lam_q1[0], diff_lam_k1[0], diff_lam_q2[0], diff_lam_k2[0]])
    subg = diff_subln_g

    dims = (c_qk, c_v, c_ql, c_kvl)
    inproj = functools.partial(_inproj, g1=g1, win=win_ext, qg=qg, wuq=wuq_ext, kvg=kvg, wukv=wukv, dims=dims)

    (_, mdk32, mdk, mdv32, mdv, _, mckv, _, mkm, mvm) = inproj(
        meta_tokens, jnp.zeros((N_META, LANES), F32), tm=N_META, tab_blocks=1)

    tm_p = min(256, s)
    tab_p = _rope_table(jnp.arange(s, dtype=jnp.int32))
    (pqd, pkd32, pkd, pvd32, pvd, pqm, pckv, pkr, pkm, pvm) = inproj(
        x_prompt.reshape(b * s, d), tab_p, tm=tm_p, tab_blocks=s // tm_p)
    tq = min(512, s)
    r3 = lambda a: a.reshape(b, s, a.shape[-1])
    od = _diff_attn(r3(pqd), r3(pkd), r3(pvd), mdk, mdv, lamv, subg, tq=tq, lam_init=lam_init)
    om = _mla_attn(r3(pqm), r3(pkm), r3(pvm), mkm, mvm, tq=tq)
    t = b * s
    hp, xn2, rt = _merge(od.reshape(t, -1), om.reshape(t, -1), x_prompt.reshape(t, d), wo, norm2_g, wr, br,
                         tm=min(512, t))
    tm_e = min(512, t)
    pos, sorted_tok, sorted_w, tile_expert, n_used = _sort_by_expert(rt[2:4].astype(jnp.int32), rt[0:2], tm_e)
    xs = jnp.take(xn2, sorted_tok, axis=0)
    ys = _moe_sorted(tile_expert, n_used, xs, sorted_w[:, None], wg, wu, wd, tm=tm_e)
    y_prompt = _combine(hp, jnp.take(ys, pos[0], axis=0), jnp.take(ys, pos[1], axis=0), gf,
                        tm=min(512, t)).reshape(b, s, d)

    bc = lambda a: jnp.broadcast_to(a[None], (b,) + a.shape)
    p_dk = jnp.concatenate([bc(mdk32), r3(pkd32)], axis=1).reshape(1, b, N_META + s, DA_HEADS, 2 * DA_D)
    p_dv = jnp.concatenate([bc(mdv32), r3(pvd32)], axis=1).reshape(1, b, N_META + s, DA_HEADS, DA_V)
    p_ckv = jnp.concatenate([bc(mckv), r3(pckv)], axis=1)[None]
    p_kr = pkr.reshape(1, b, s, MLA_ROPE)

    ts = bs * ss
    s_pos = past + jnp.arange(ss, dtype=jnp.int32)
    (sqd, skd32, skd, svd32, svd, sqm, sckv, skr, _, _) = inproj(
        x_sample.reshape(ts, d), _rope_table(s_pos), tm=ss, tab_blocks=1)
    q3 = lambda a: a.reshape(bs, ss, a.shape[-1])
    kpos_c = np.arange(lc) - N_META
    dist_c = np.where(kpos_c[None, :] >= 0, np.abs(past + np.arange(ss)[:, None] - kpos_c[None, :]), 0)
    dist_n = np.abs(np.arange(ss)[:, None] - np.arange(ss)[None, :])
    tk_s = lc // 2 if (lc // 2) % 8 == 0 and lc % 2 == 0 else lc
    dist_c = np.tile(dist_c, (2, 1)).astype(np.float32).reshape(2 * ss, lc // tk_s, tk_s)
    dist_c = jnp.asarray(np.moveaxis(dist_c, 1, 0))
    dist_n = jnp.asarray(np.tile(dist_n, (2, 1)).astype(np.float32))
    sod = _sample_diff_attn(q3(sqd), cache_diff_k[0].reshape(bs, lc, c_qk), cache_diff_v[0].reshape(bs, lc, c_v),
                            q3(skd), q3(svd), dist_c, dist_n, lamv, subg, tk=tk_s, lam_init=lam_init)
    krc = jnp.concatenate([jnp.zeros((bs, N_META, MLA_ROPE), F32), cache_mla_kr[0]], axis=1)
    som = _sample_mla_attn(q3(sqm), cache_mla_ckv[0], krc, q3(sckv), q3(skr), wuk, wuv, tk=tk_s)
    hs, xn2s, rts = _merge(sod.reshape(ts, -1), som.reshape(ts, -1), x_sample.reshape(ts, d), wo, norm2_g, wr, br,
                           tm=ts)
    eids = rts[2:4].astype(jnp.int32)
    gates = (jnp.where(eids[0][:, None] == jnp.arange(LANES)[None, :], rts[0][:, None], 0.0)
             + jnp.where(eids[1][:, None] == jnp.arange(LANES)[None, :], rts[1][:, None], 0.0))
    y_sample = _moe_dense(xn2s, hs, gates, wg, wu, wd, gf).reshape(bs, ss, d)

    s_dk = skd32.reshape(1, bs, ss, DA_HEADS, 2 * DA_D)
    s_dv = svd32.reshape(1, bs, ss, DA_HEADS, DA_V)
    s_ckv = sckv.reshape(1, bs, ss, c_kvl)
    s_kr = skr.reshape(1, bs, ss, MLA_ROPE)
    return (y_prompt, y_sample, p_dk, p_dv, p_ckv, p_kr, s_dk, s_dv, s_ckv, s_kr)
```

```python
import functools
import math

import numpy as np
import jax
import jax.numpy as jnp
from jax import lax
from jax.experimental import pallas as pl
from jax.experimental.pallas import tpu as pltpu

CHUNK = 64
N_META = 16
EPS = 1e-6
DA_HEADS = 8
DA_D = 64
DA_V = 2 * DA_D
MLA_HEADS = 8
MLA_NOPE = 128
MLA_ROPE = 64
MLA_V = 128
ROPE_THETA = 10000.0
MLA_SCALE = (MLA_NOPE + MLA_ROPE) ** -0.5
N_GROUPS = 4
EXPERTS_PER_GROUP = 4
N_EXPERTS = N_GROUPS * EXPERTS_PER_GROUP
LANES = 128
HEAD_W = 128
MLA_W = 256
NEG_BIG = -1e30
VMEM_LIMIT = 56 * 1024 * 1024

BF16 = jnp.bfloat16
F32 = jnp.float32


def _dot(a, b):
    return jnp.dot(a, b, preferred_element_type=F32)


def _dot_nt(a, b):
    return lax.dot_general(a, b, (((1,), (1,)), ((), ())), preferred_element_type=F32)


def _rms(x, g):
    return x * lax.rsqrt(jnp.mean(x * x, axis=-1, keepdims=True) + EPS) * g


def _cparams(n_axes):
    return pltpu.CompilerParams(dimension_semantics=("arbitrary",) * n_axes,
                                vmem_limit_bytes=VMEM_LIMIT)


def _const_spec(shape):
    nd = len(shape)
    return pl.BlockSpec(shape, lambda *_: (0,) * nd, pipeline_mode=pl.Buffered(1))


def _inproj_kernel(x_ref, tab_ref, g1_ref, win_ref, qg_ref, wuq_ref, kvg_ref, wukv_ref,
                   qd_ref, kd32_ref, kdb_ref, vd32_ref, vdb_ref, qm_ref, ckv_ref, kr_ref,
                   km_ref, vm_ref, *, c_qk, c_v, c_ql, c_kvl, transposed):
    out_t = (lambda a: a.T.astype(BF16)) if transposed else (lambda a: a.astype(BF16))
    x = x_ref[...]
    xn = _rms(x, g1_ref[...]).astype(BF16)
    tab = tab_ref[...]
    o1 = c_qk
    o2 = o1 + c_qk
    o3 = o2 + c_v
    o4 = o3 + c_ql
    o5 = o4 + c_kvl

    zq = _dot(xn, win_ref[:, 0:o1])
    qd_ref[...] = out_t(zq * (DA_D ** -0.5))
    zk = _dot(xn, win_ref[:, o1:o2])
    kd32_ref[...] = zk
    kdb_ref[...] = zk.astype(BF16)
    zv = _dot(xn, win_ref[:, o2:o3])
    vd32_ref[...] = zv
    vdb_ref[...] = out_t(zv)

    cq = _rms(_dot(xn, win_ref[:, o3:o4]), qg_ref[...]).astype(BF16)
    nq = MLA_HEADS * MLA_NOPE
    qn = _dot(cq, wuq_ref[:, 0:nq]) * MLA_SCALE
    qr = _dot(cq, wuq_ref[:, nq:2 * nq]) * MLA_SCALE
    for h in range(MLA_HEADS):
        u = qr[:, h * LANES:(h + 1) * LANES] * tab
        rot = u + pltpu.roll(u, MLA_ROPE, 1)
        if transposed:
            qm_ref[h * MLA_W:h * MLA_W + LANES, :] = out_t(qn[:, h * LANES:(h + 1) * LANES])
            qm_ref[h * MLA_W + LANES:(h + 1) * MLA_W, :] = out_t(rot)
        else:
            qm_ref[:, h * MLA_W:h * MLA_W + LANES] = out_t(qn[:, h * LANES:(h + 1) * LANES])
            qm_ref[:, h * MLA_W + LANES:(h + 1) * MLA_W] = out_t(rot)

    ckv = _rms(_dot(xn, win_ref[:, o4:o5]), kvg_ref[...])
    ckv_ref[...] = ckv
    ckvb = ckv.astype(BF16)
    nk = MLA_HEADS * MLA_NOPE
    kn = _dot(ckvb, wukv_ref[:, 0:nk])
    vm_ref[...] = out_t(_dot(ckvb, wukv_ref[:, nk:nk + MLA_HEADS * MLA_V]))

    u = _dot(xn, win_ref[:, o5:o5 + LANES]) * tab
    rot = u + pltpu.roll(u, MLA_ROPE, 1)
    kr_ref[...] = rot[:, 0:MLA_ROPE]
    lane = lax.broadcasted_iota(jnp.int32, rot.shape, 1)
    krp = jnp.where(lane < MLA_ROPE, rot, 0.0).astype(BF16)
    for h in range(MLA_HEADS):
        km_ref[:, h * MLA_W:h * MLA_W + LANES] = kn[:, h * LANES:(h + 1) * LANES].astype(BF16)
        km_ref[:, h * MLA_W + LANES:(h + 1) * MLA_W] = krp


def _inproj(x, tab, g1, win, qg, wuq, kvg, wukv, *, tm, tab_blocks, dims, batch=None):
    m, d = x.shape
    c_qk, c_v, c_ql, c_kvl = dims
    assert m % tm == 0
    transposed = batch is not None
    row = lambda w: pl.BlockSpec((tm, w), lambda i: (i, 0))
    outs = [
        (c_qk, BF16, True), (c_qk, F32, False), (c_qk, BF16, False), (c_v, F32, False), (c_v, BF16, True),
        (MLA_HEADS * MLA_W, BF16, True), (c_kvl, F32, False), (MLA_ROPE, F32, False),
        (MLA_HEADS * MLA_W, BF16, False), (MLA_HEADS * MLA_V, BF16, True),
    ]
    if transposed:
        assert m == batch * tab_blocks * tm
        col = lambda w: pl.BlockSpec((None, w, tm), lambda i: (i // tab_blocks, 0, i % tab_blocks))
        out_specs = [col(w) if t else row(w) for w, _, t in outs]
        out_shape = [jax.ShapeDtypeStruct((batch, w, m // batch) if t else (m, w), dt) for w, dt, t in outs]
    else:
        out_specs = [row(w) for w, _, _ in outs]
        out_shape = [jax.ShapeDtypeStruct((m, w), dt) for w, dt, _ in outs]
    return pl.pallas_call(
        functools.partial(_inproj_kernel, c_qk=c_qk, c_v=c_v, c_ql=c_ql, c_kvl=c_kvl, transposed=transposed),
        grid=(m // tm,),
        in_specs=[
            row(d),
            pl.BlockSpec((tm, LANES), lambda i: (i % tab_blocks, 0)),
            _const_spec(g1.shape), _const_spec(win.shape), _const_spec(qg.shape),
            _const_spec(wuq.shape), _const_spec(kvg.shape), _const_spec(wukv.shape),
        ],
        out_specs=out_specs,
        out_shape=out_shape,
        compiler_params=_cparams(1),
        name="inproj",
    )(x, tab, g1, win, qg, wuq, kvg, wukv)


def _seed_t(st, vt, m_ref, l_ref, acc_ref, j):
    m = jnp.max(st, axis=0, keepdims=True)
    p = jnp.exp(st - m)
    m_ref[j] = m
    l_ref[j] = jnp.sum(p, axis=0, keepdims=True)
    acc_ref[j] = _dot(vt, p.astype(BF16))


def _step_t(st, vt, m_ref, l_ref, acc_ref, j):
    m_old = m_ref[j]
    m_new = jnp.maximum(m_old, jnp.max(st, axis=0, keepdims=True))
    alpha = jnp.exp(m_old - m_new)
    p = jnp.exp(st - m_new)
    l_ref[j] = alpha * l_ref[j] + jnp.sum(p, axis=0, keepdims=True)
    acc_ref[j] = alpha * acc_ref[j] + _dot(vt, p.astype(BF16))
    m_ref[j] = m_new


def _softmax_seed(s, v, m_ref, l_ref, acc_ref, j):
    m = jnp.max(s, axis=1, keepdims=True)
    p = jnp.exp(s - m)
    m_ref[j] = m
    l_ref[j] = jnp.sum(p, axis=1, keepdims=True)
    acc_ref[j] = _dot(p.astype(BF16), v)


def _softmax_step(s, v, m_ref, l_ref, acc_ref, j):
    m_old = m_ref[j]
    m_new = jnp.maximum(m_old, jnp.max(s, axis=1, keepdims=True))
    alpha = jnp.exp(m_old - m_new)
    p = jnp.exp(s - m_new)
    l_ref[j] = alpha * l_ref[j] + jnp.sum(p, axis=1, keepdims=True)
    acc_ref[j] = alpha * acc_ref[j] + _dot(p.astype(BF16), v)
    m_ref[j] = m_new


def _diff_lambda(lamv, lam_init):
    a = jnp.sum(lamv[0:1] * lamv[1:2], axis=1, keepdims=True)
    b = jnp.sum(lamv[2:3] * lamv[3:4], axis=1, keepdims=True)
    return jnp.exp(a) - jnp.exp(b) + lam_init


def _split_maps(q):
    lane = lax.broadcasted_iota(jnp.int32, q.shape, 1)
    zero = jnp.zeros_like(q)
    return jnp.where(lane < DA_D, q, zero), jnp.where(lane >= DA_D, q, zero)


def _split_maps_t(qt):
    row = lax.broadcasted_iota(jnp.int32, qt.shape, 0)
    zero = jnp.zeros_like(qt)
    return jnp.where(row < DA_D, qt, zero), jnp.where(row >= DA_D, qt, zero)


def _diff_attn_kernel(qi_ref, ki_ref, qt_ref, k_ref, vt_ref, mk_ref, mvt_ref, dist_ref, lamv_ref,
                      g_ref, o_ref, m_s, l_s, acc_s, *, tq, lam_init):
    t = pl.program_id(1)
    qi = qi_ref[t]
    ki = ki_ref[t]

    @pl.when(ki == 0)
    def _seed():
        for h in range(DA_HEADS):
            hs = slice(h * HEAD_W, (h + 1) * HEAD_W)
            for c, qc in enumerate(_split_maps_t(qt_ref[hs, :])):
                _seed_t(_dot(mk_ref[:, hs], qc), mvt_ref[hs, :], m_s, l_s, acc_s, 2 * h + c)

    dist = dist_ref[...] + ((qi - ki) * tq).astype(F32)
    for h in range(DA_HEADS):
        hs = slice(h * HEAD_W, (h + 1) * HEAD_W)
        bias = dist * (-(2.0 ** (-8.0 * (h + 1) / DA_HEADS)))
        for c, qc in enumerate(_split_maps_t(qt_ref[hs, :])):
            _step_t(_dot(k_ref[:, hs], qc) + bias, vt_ref[hs, :], m_s, l_s, acc_s, 2 * h + c)

    @pl.when(ki == qi)
    def _finish():
        lam = _diff_lambda(lamv_ref[...], lam_init)
        g = g_ref[...]
        for h in range(DA_HEADS):
            ot = acc_s[2 * h] / l_s[2 * h] - lam * (acc_s[2 * h + 1] / l_s[2 * h + 1])
            ot = ot * lax.rsqrt(jnp.mean(ot * ot, axis=0, keepdims=True) + EPS) * g * (1.0 - lam_init)
            o_ref[:, h * HEAD_W:(h + 1) * HEAD_W] = ot.T.astype(BF16)


def _mla_attn_kernel(qi_ref, ki_ref, qt_ref, k_ref, vt_ref, mk_ref, mvt_ref, mask_ref, o_ref,
                     m_s, l_s, acc_s):
    t = pl.program_id(1)
    qi = qi_ref[t]
    ki = ki_ref[t]

    @pl.when(ki == 0)
    def _seed():
        for h in range(MLA_HEADS):
            st = _dot(mk_ref[:, h * MLA_W:(h + 1) * MLA_W], qt_ref[h * MLA_W:(h + 1) * MLA_W, :])
            _seed_t(st, mvt_ref[h * MLA_V:(h + 1) * MLA_V, :], m_s, l_s, acc_s, h)

    mask = mask_ref[...]
    for h in range(MLA_HEADS):
        st = _dot(k_ref[:, h * MLA_W:(h + 1) * MLA_W], qt_ref[h * MLA_W:(h + 1) * MLA_W, :]) + mask
        _step_t(st, vt_ref[h * MLA_V:(h + 1) * MLA_V, :], m_s, l_s, acc_s, h)

    @pl.when(ki == qi)
    def _finish():
        for h in range(MLA_HEADS):
            o_ref[:, h * MLA_V:(h + 1) * MLA_V] = (acc_s[h] / l_s[h]).T.astype(BF16)


def _pair_tables(nq):
    qi = np.concatenate([np.full((i + 1,), i, np.int32) for i in range(nq)])
    ki = np.concatenate([np.arange(i + 1, dtype=np.int32) for i in range(nq)])
    return jnp.asarray(qi), jnp.asarray(ki)


def _tile_geometry(tq):
    j = np.arange(tq)[:, None]
    i = np.arange(tq)[None, :]
    visible = (j // CHUNK) <= (i // CHUNK)
    return i, j, visible


def _prompt_attn_specs(tq, wq, wk, wv):
    qt_spec = pl.BlockSpec((None, wq, tq), lambda b, t, qi, ki: (b, 0, qi[t]))
    k_spec = pl.BlockSpec((None, tq, wk), lambda b, t, qi, ki: (b, ki[t], 0))
    vt_spec = pl.BlockSpec((None, wv, tq), lambda b, t, qi, ki: (b, 0, ki[t]))
    mk_spec = pl.BlockSpec((N_META, wk), lambda b, t, qi, ki: (0, 0))
    mvt_spec = pl.BlockSpec((wv, N_META), lambda b, t, qi, ki: (0, 0))
    tile_spec = pl.BlockSpec((None, tq, tq), lambda b, t, qi, ki: (jnp.where(qi[t] == ki[t], 1, 0), 0, 0))
    o_spec = pl.BlockSpec((None, tq, wv), lambda b, t, qi, ki: (b, qi[t], 0))
    return qt_spec, k_spec, vt_spec, mk_spec, mvt_spec, tile_spec, o_spec


def _diff_attn(qt, k, vt, mk, mvt, lamv, g, *, tq, lam_init):
    b, s, w = k.shape
    nq = s // tq
    qi, ki = _pair_tables(nq)
    i, j, visible = _tile_geometry(tq)
    dist = jnp.asarray(np.stack([(i - j).astype(np.float32),
                                 np.where(visible, np.abs(i - j), 1e30).astype(np.float32)]))
    qt_spec, k_spec, vt_spec, mk_spec, mvt_spec, tile_spec, o_spec = _prompt_attn_specs(tq, w, w, w)
    full = lambda a: pl.BlockSpec(a.shape, lambda b_, t, qi_, ki_: (0,) * a.ndim)
    return pl.pallas_call(
        functools.partial(_diff_attn_kernel, tq=tq, lam_init=lam_init),
        grid_spec=pltpu.PrefetchScalarGridSpec(
            num_scalar_prefetch=2,
            grid=(b, int(qi.shape[0])),
            in_specs=[qt_spec, k_spec, vt_spec, mk_spec, mvt_spec, tile_spec, full(lamv), full(g)],
            out_specs=o_spec,
            scratch_shapes=[pltpu.VMEM((2 * DA_HEADS, 1, tq), F32), pltpu.VMEM((2 * DA_HEADS, 1, tq), F32),
                            pltpu.VMEM((2 * DA_HEADS, DA_V, tq), F32)],
        ),
        out_shape=jax.ShapeDtypeStruct((b, s, w), BF16),
        compiler_params=_cparams(2),
        name="diff_attn",
    )(qi, ki, qt, k, vt, mk, mvt, dist, lamv, g)


def _mla_attn(qt, k, vt, mk, mvt, *, tq):
    b, s, wq = k.shape
    wv = vt.shape[1]
    nq = s // tq
    qi, ki = _pair_tables(nq)
    _, _, visible = _tile_geometry(tq)
    mask = jnp.asarray(np.stack([np.zeros((tq, tq), np.float32),
                                 np.where(visible, 0.0, NEG_BIG).astype(np.float32)]))
    qt_spec, k_spec, vt_spec, mk_spec, mvt_spec, tile_spec, o_spec = _prompt_attn_specs(tq, wq, wq, wv)
    return pl.pallas_call(
        _mla_attn_kernel,
        grid_spec=pltpu.PrefetchScalarGridSpec(
            num_scalar_prefetch=2,
            grid=(b, int(qi.shape[0])),
            in_specs=[qt_spec, k_spec, vt_spec, mk_spec, mvt_spec, tile_spec],
            out_specs=o_spec,
            scratch_shapes=[pltpu.VMEM((MLA_HEADS, 1, tq), F32), pltpu.VMEM((MLA_HEADS, 1, tq), F32),
                            pltpu.VMEM((MLA_HEADS, MLA_V, tq), F32)],
        ),
        out_shape=jax.ShapeDtypeStruct((b, s, wv), BF16),
        compiler_params=_cparams(2),
        name="mla_attn",
    )(qi, ki, qt, k, vt, mk, mvt, mask)


def _sample_diff_kernel(q_ref, kc_ref, vc_ref, kn_ref, vn_ref, dc_ref, dn_ref, lamv_ref, g_ref,
                        o_ref, m_s, l_s, acc_s, *, lam_init):
    kt = pl.program_id(1)
    nkt = pl.num_programs(1)

    def heads(kk_ref, vv_ref, dist, first):
        for h in range(DA_HEADS):
            hs = slice(h * HEAD_W, (h + 1) * HEAD_W)
            q1, q2 = _split_maps(q_ref[:, hs])
            qq = jnp.concatenate([q1, q2], axis=0)
            s = _dot_nt(qq, kk_ref[:, hs].astype(BF16)) + dist * (-(2.0 ** (-8.0 * (h + 1) / DA_HEADS)))
            vv = vv_ref[:, hs].astype(BF16)
            if first:
                _softmax_seed(s, vv, m_s, l_s, acc_s, h)
            else:
                _softmax_step(s, vv, m_s, l_s, acc_s, h)

    @pl.when(kt == 0)
    def _first():
        heads(kc_ref, vc_ref, dc_ref[...], True)

    @pl.when(kt > 0)
    def _rest():
        heads(kc_ref, vc_ref, dc_ref[...], False)

    @pl.when(kt == nkt - 1)
    def _finish():
        heads(kn_ref, vn_ref, dn_ref[...], False)
        lam = _diff_lambda(lamv_ref[...], lam_init)
        g = g_ref[...]
        nq = q_ref.shape[0]
        for h in range(DA_HEADS):
            a = acc_s[h] / l_s[h]
            o = a[0:nq] - lam * a[nq:2 * nq]
            o_ref[:, h * HEAD_W:(h + 1) * HEAD_W] = (_rms(o, g) * (1.0 - lam_init)).astype(BF16)


def _sample_diff_attn(q, kc, vc, kn, vn, dist_c, dist_n, lamv, g, *, tk, lam_init):
    bs, nq, w = q.shape
    lc = kc.shape[1]
    assert lc % tk == 0
    full = lambda a: pl.BlockSpec(a.shape, lambda b, t: (0,) * a.ndim)
    per_stream = lambda a: pl.BlockSpec((None,) + a.shape[1:], lambda b, t: (b,) + (0,) * (a.ndim - 1))
    return pl.pallas_call(
        functools.partial(_sample_diff_kernel, lam_init=lam_init),
        grid=(bs, lc // tk),
        in_specs=[per_stream(q),
                  pl.BlockSpec((None, tk, w), lambda b, t: (b, t, 0)),
                  pl.BlockSpec((None, tk, w), lambda b, t: (b, t, 0)),
                  per_stream(kn), per_stream(vn),
                  pl.BlockSpec((None, 2 * nq, tk), lambda b, t: (t, 0, 0)),
                  full(dist_n), full(lamv), full(g)],
        out_specs=per_stream(q),
        out_shape=jax.ShapeDtypeStruct((bs, nq, w), BF16),
        scratch_shapes=[pltpu.VMEM((DA_HEADS, 2 * nq, 1), F32), pltpu.VMEM((DA_HEADS, 2 * nq, 1), F32),
                        pltpu.VMEM((DA_HEADS, 2 * nq, DA_V), F32)],
        compiler_params=_cparams(2),
        name="sample_diff_attn",
    )(q, kc, vc, kn, vn, dist_c, dist_n, lamv, g)


def _sample_mla_kernel(q_ref, cc_ref, krc_ref, cn_ref, krn_ref, wuk_ref, wuv_ref, o_ref,
                       ql_s, qr_s, m_s, l_s, acc_s):
    kt = pl.program_id(1)
    nkt = pl.num_programs(1)
    nq = q_ref.shape[0]

    @pl.when(kt == 0)
    def _prep():
        for h in range(MLA_HEADS):
            qn = q_ref[:, h * MLA_W:h * MLA_W + MLA_NOPE]
            ql_s[h * nq:(h + 1) * nq, :] = _dot_nt(qn, wuk_ref[:, h * MLA_NOPE:(h + 1) * MLA_NOPE]).astype(BF16)
            qr_s[h * nq:(h + 1) * nq, :] = q_ref[:, h * MLA_W + MLA_NOPE:(h + 1) * MLA_W]

    def scores(c_ref, kr_ref):
        cb = c_ref[...].astype(BF16)
        krb = kr_ref[...].astype(BF16)
        s = _dot_nt(ql_s[...], cb) + _dot_nt(qr_s[:, 0:MLA_ROPE], krb)
        return s, cb

    @pl.when(kt == 0)
    def _first():
        s, cb = scores(cc_ref, krc_ref)
        _softmax_seed(s, cb, m_s, l_s, acc_s, 0)

    @pl.when(kt > 0)
    def _rest():
        s, cb = scores(cc_ref, krc_ref)
        _softmax_step(s, cb, m_s, l_s, acc_s, 0)

    @pl.when(kt == nkt - 1)
    def _finish():
        s, cb = scores(cn_ref, krn_ref)
        _softmax_step(s, cb, m_s, l_s, acc_s, 0)
        ol = (acc_s[0] / l_s[0]).astype(BF16)
        for h in range(MLA_HEADS):
            o_ref[:, h * MLA_V:(h + 1) * MLA_V] = _dot(
                ol[h * nq:(h + 1) * nq, :], wuv_ref[:, h * MLA_V:(h + 1) * MLA_V]).astype(BF16)


def _sample_mla_attn(q, cc, krc, cn, krn, wuk, wuv, *, tk):
    bs, nq, wq = q.shape
    lc, kvl = cc.shape[1], cc.shape[2]
    assert lc % tk == 0
    full = lambda a: pl.BlockSpec(a.shape, lambda b, t: (0,) * a.ndim)
    per_stream = lambda a: pl.BlockSpec((None,) + a.shape[1:], lambda b, t: (b,) + (0,) * (a.ndim - 1))
    rows = MLA_HEADS * nq
    return pl.pallas_call(
        _sample_mla_kernel,
        grid=(bs, lc // tk),
        in_specs=[per_stream(q),
                  pl.BlockSpec((None, tk, kvl), lambda b, t: (b, t, 0)),
                  pl.BlockSpec((None, tk, MLA_ROPE), lambda b, t: (b, t, 0)),
                  per_stream(cn), per_stream(krn), full(wuk), full(wuv)],
        out_specs=pl.BlockSpec((None, nq, MLA_HEADS * MLA_V), lambda b, t: (b, 0, 0)),
        out_shape=jax.ShapeDtypeStruct((bs, nq, MLA_HEADS * MLA_V), BF16),
        scratch_shapes=[pltpu.VMEM((rows, kvl), BF16), pltpu.VMEM((rows, LANES), BF16),
                        pltpu.VMEM((1, rows, 1), F32), pltpu.VMEM((1, rows, 1), F32),
                        pltpu.VMEM((1, rows, kvl), F32)],
        compiler_params=_cparams(2),
        name="sample_mla_attn",
    )(q, cc, krc, cn, krn, wuk, wuv)


ROUTER_ROWS = 8 * (1 + N_GROUPS)


def _route(lt):
    g = [lt[i:i + 1] for i in range(N_GROUPS)]
    gmax = functools.reduce(jnp.maximum, g)
    gidx = jnp.full_like(gmax, float(N_GROUPS - 1))
    for i in range(N_GROUPS - 2, -1, -1):
        gidx = jnp.where(g[i] == gmax, float(i), gidx)
    den = functools.reduce(lambda a, b: a + b, [jnp.exp(gi - gmax) for gi in g])
    p_top = 1.0 / den
    e = []
    for j in range(EXPERTS_PER_GROUP):
        ej = lt[8 * N_GROUPS + j:8 * N_GROUPS + j + 1]
        for grp in range(N_GROUPS - 2, -1, -1):
            ej = jnp.where(gidx == float(grp), lt[8 * (grp + 1) + j:8 * (grp + 1) + j + 1], ej)
        e.append(ej)

    def first_argmax(vals):
        vmax = functools.reduce(jnp.maximum, vals)
        idx = jnp.full_like(vmax, float(len(vals) - 1))
        for i in range(len(vals) - 2, -1, -1):
            idx = jnp.where(vals[i] == vmax, float(i), idx)
        return vmax, idx

    v1, i1 = first_argmax(e)
    rest = [jnp.where(i1 == float(j), -jnp.inf, e[j]) for j in range(EXPERTS_PER_GROUP)]
    v2, i2 = first_argmax(rest)
    r = jnp.exp(v2 - v1)
    w1 = p_top / (1.0 + r)
    w2 = p_top * r / (1.0 + r)
    base = gidx * float(EXPERTS_PER_GROUP)
    return w1, w2, base + i1, base + i2


def _merge_kernel(od_ref, om_ref, x_ref, wo_ref, g2_ref, wr_ref, br_ref, hp_ref, xn_ref, rt_ref):
    nd = od_ref.shape[1]
    y = _dot(od_ref[...], wo_ref[0:nd, :]) + _dot(om_ref[...], wo_ref[nd:, :])
    hp = x_ref[...] + y
    hp_ref[...] = hp
    xn = _rms(hp, g2_ref[...]).astype(BF16)
    xn_ref[...] = xn
    lt = _dot_nt(wr_ref[...], xn) + br_ref[...]
    rows = _route(lt)
    for i, r in enumerate(rows):
        rt_ref[i:i + 1, :] = r
    rt_ref[4:8, :] = jnp.zeros((4, rt_ref.shape[1]), F32)


def _merge(od, om, x, wo, g2, wr, br, *, tm):
    m, d = x.shape
    row = lambda w: pl.BlockSpec((tm, w), lambda i: (i, 0))
    return pl.pallas_call(
        _merge_kernel,
        grid=(m // tm,),
        in_specs=[row(od.shape[1]), row(om.shape[1]), row(d), _const_spec(wo.shape),
                  _const_spec(g2.shape), _const_spec(wr.shape), _const_spec(br.shape)],
        out_specs=[row(d), row(d), pl.BlockSpec((8, tm), lambda i: (0, i))],
        out_shape=[jax.ShapeDtypeStruct((m, d), F32), jax.ShapeDtypeStruct((m, d), BF16),
                   jax.ShapeDtypeStruct((8, m), F32)],
        compiler_params=_cparams(1),
        name="merge",
    )(od, om, x, wo, g2, wr, br)


def _swiglu(x, wg, wu, wd):
    g = _dot(x, wg)
    u = _dot(x, wu)
    h = (g * jax.nn.sigmoid(g) * u).astype(BF16)
    return _dot(h, wd)


def _moe_sorted_kernel(te_ref, nu_ref, x_ref, w_ref, wg_ref, wu_ref, wd_ref, y_ref):
    i = pl.program_id(0)

    @pl.when(i < nu_ref[0])
    def _():
        y_ref[...] = w_ref[...] * _swiglu(x_ref[...], wg_ref[...], wu_ref[...], wd_ref[...])

    @pl.when(i >= nu_ref[0])
    def _():
        y_ref[...] = jnp.zeros_like(y_ref)


def _moe_sorted(tile_expert, n_used, xs, ws, wg, wu, wd, *, tm):
    n, d = xs.shape
    f = wg.shape[2]
    nt = n // tm
    return pl.pallas_call(
        _moe_sorted_kernel,
        grid_spec=pltpu.PrefetchScalarGridSpec(
            num_scalar_prefetch=2,
            grid=(nt,),
            in_specs=[pl.BlockSpec((tm, d), lambda i, te, nu: (i, 0)),
                      pl.BlockSpec((tm, 1), lambda i, te, nu: (i, 0)),
                      pl.BlockSpec((None, d, f), lambda i, te, nu: (te[i], 0, 0)),
                      pl.BlockSpec((None, d, f), lambda i, te, nu: (te[i], 0, 0)),
                      pl.BlockSpec((None, f, d), lambda i, te, nu: (te[i], 0, 0))],
            out_specs=pl.BlockSpec((tm, d), lambda i, te, nu: (i, 0)),
        ),
        out_shape=jax.ShapeDtypeStruct((n, d), F32),
        compiler_params=_cparams(1),
        name="moe_sorted",
    )(tile_expert, n_used, xs, ws, wg, wu, wd)


def _moe_dense_kernel(x_ref, hp_ref, gates_ref, wg_ref, wu_ref, wd_ref, gf_ref, o_ref, acc_s):
    e = pl.program_id(0)

    @pl.when(e == 0)
    def _():
        acc_s[...] = jnp.zeros_like(acc_s)

    lane = lax.broadcasted_iota(jnp.int32, gates_ref.shape, 1)
    gate = jnp.sum(jnp.where(lane == e, gates_ref[...], 0.0), axis=1, keepdims=True)
    acc_s[...] += gate * _swiglu(x_ref[...], wg_ref[...], wu_ref[...], wd_ref[...])

    @pl.when(e == pl.num_programs(0) - 1)
    def _():
        o_ref[...] = _rms(hp_ref[...] + acc_s[...], gf_ref[...])


def _moe_dense(xn, hp, gates, wg, wu, wd, gf):
    m, d = xn.shape
    ne, _, f = wg.shape
    full = lambda a: pl.BlockSpec(a.shape, lambda e: (0,) * a.ndim)
    return pl.pallas_call(
        _moe_dense_kernel,
        grid=(ne,),
        in_specs=[full(xn), full(hp), full(gates),
                  pl.BlockSpec((None, d, f), lambda e: (e, 0, 0)),
                  pl.BlockSpec((None, d, f), lambda e: (e, 0, 0)),
                  pl.BlockSpec((None, f, d), lambda e: (e, 0, 0)),
                  full(gf)],
        out_specs=full(hp),
        out_shape=jax.ShapeDtypeStruct((m, d), F32),
        scratch_shapes=[pltpu.VMEM((m, d), F32)],
        compiler_params=_cparams(1),
        name="moe_dense",
    )(xn, hp, gates, wg, wu, wd, gf)


def _combine_kernel(hp_ref, y1_ref, y2_ref, gf_ref, o_ref):
    o_ref[...] = _rms(hp_ref[...] + (y1_ref[...] + y2_ref[...]), gf_ref[...])


def _combine(hp, y1, y2, gf, *, tm):
    m, d = hp.shape
    row = pl.BlockSpec((tm, d), lambda i: (i, 0))
    return pl.pallas_call(
        _combine_kernel,
        grid=(m // tm,),
        in_specs=[row, row, row, _const_spec(gf.shape)],
        out_specs=row,
        out_shape=jax.ShapeDtypeStruct((m, d), F32),
        compiler_params=_cparams(1),
        name="combine",
    )(hp, y1, y2, gf)


def _rope_table(pos):
    half = MLA_ROPE // 2
    inv_freq = ROPE_THETA ** (-jnp.arange(half, dtype=F32) / half)
    ang = pos.astype(F32)[:, None] * inv_freq[None, :]
    c, s = jnp.cos(ang), jnp.sin(ang)
    return jnp.concatenate([c, c, -s, s], axis=1)


def _swap_halves(w):
    half = MLA_ROPE // 2
    return jnp.concatenate([w[..., half:], w[..., :half]], axis=-1)


def _sort_by_expert(eid, w, tm):
    t = eid.shape[1]
    flat_e = eid.reshape(-1)
    onehot = (flat_e[:, None] == jnp.arange(N_EXPERTS, dtype=jnp.int32)[None, :]).astype(jnp.int32)
    rank = jnp.sum((jnp.cumsum(onehot, axis=0) - onehot) * onehot, axis=1)
    counts = jnp.sum(onehot, axis=0)
    tiles_per = (counts + tm - 1) // tm
    tiles_end = jnp.cumsum(tiles_per)
    row_start = (tiles_end - tiles_per) * tm
    pos = row_start[flat_e] + rank
    n_tiles = (2 * t) // tm + N_EXPERTS
    tok = jnp.tile(jnp.arange(t, dtype=jnp.int32), 2)
    sorted_tok = jnp.zeros((n_tiles * tm,), jnp.int32).at[pos].set(tok)
    sorted_w = jnp.zeros((n_tiles * tm,), F32).at[pos].set(w.reshape(-1))
    tile_ids = jnp.arange(n_tiles, dtype=jnp.int32)
    tile_expert = jnp.minimum(jnp.sum((tiles_end[None, :] <= tile_ids[:, None]).astype(jnp.int32), axis=1),
                              N_EXPERTS - 1)
    n_used = tiles_end[-1:].astype(jnp.int32)
    return pos.reshape(2, t), sorted_tok, sorted_w, tile_expert, n_used


def kernel(x_prompt, x_sample, cache_diff_k, cache_diff_v, cache_mla_ckv, cache_mla_kr, meta_tokens, norm1_g, w_in, diff_lam_q1, diff_lam_k1, diff_lam_q2, diff_lam_k2, diff_subln_g, mla_q_norm_g, mla_w_uq, mla_kv_norm_g, mla_w_uk, mla_w_uv, w_o, norm2_g, router_group_w, router_group_b, router_expert_w, router_expert_b, expert_w_gate, expert_w_up, expert_w_down, final_norm_g):
    depth = norm1_g.shape[0]
    assert depth == 1, "single-layer step only"
    lam_init = 0.8 - 0.6 * math.exp(-0.3 * 0)
    b, s, d = x_prompt.shape
    bs, ss, _ = x_sample.shape
    past = cache_mla_kr.shape[2]
    lc = N_META + past
    c_qk = DA_HEADS * 2 * DA_D
    c_v = DA_HEADS * DA_V
    c_ql = mla_q_norm_g.shape[1]
    c_kvl = mla_kv_norm_g.shape[1]
    o5 = 2 * c_qk + c_v + c_ql + c_kvl

    win = w_in[0]
    win_ext = jnp.concatenate([win, _swap_halves(win[:, o5:])], axis=1).astype(BF16)
    wuq = mla_w_uq[0].reshape(c_ql, MLA_HEADS, MLA_NOPE + MLA_ROPE)
    wuq_n = wuq[:, :, :MLA_NOPE].reshape(c_ql, MLA_HEADS * MLA_NOPE)
    wuq_r = jnp.concatenate([wuq[:, :, MLA_NOPE:], _swap_halves(wuq[:, :, MLA_NOPE:])], axis=2)
    wuq_ext = jnp.concatenate([wuq_n, wuq_r.reshape(c_ql, MLA_HEADS * LANES)], axis=1).astype(BF16)
    wuk = mla_w_uk[0].astype(BF16)
    wuv = mla_w_uv[0].astype(BF16)
    wukv = jnp.concatenate([wuk, wuv], axis=1)
    g1 = norm1_g
    qg = mla_q_norm_g
    kvg = mla_kv_norm_g
    wo = w_o[0].astype(BF16)
    wr = jnp.zeros((ROUTER_ROWS, d), F32).at[0:N_GROUPS].set(router_group_w[0].T)
    br = jnp.zeros((ROUTER_ROWS, 1), F32).at[0:N_GROUPS, 0].set(router_group_b[0])
    rew = router_expert_w[0].T.reshape(N_GROUPS, EXPERTS_PER_GROUP, d)
    reb = router_expert_b[0].reshape(N_GROUPS, EXPERTS_PER_GROUP)
    for grp in range(N_GROUPS):
        wr = wr.at[8 * (grp + 1):8 * (grp + 1) + EXPERTS_PER_GROUP].set(rew[grp])
        br = br.at[8 * (grp + 1):8 * (grp + 1) + EXPERTS_PER_GROUP, 0].set(reb[grp])
    wr = wr.astype(BF16)
    wg = expert_w_gate[0].astype(BF16)
    wu = expert_w_up[0].astype(BF16)
    wd = expert_w_down[0].astype(BF16)
    gf = final_norm_g[None, :]
    lamv = jnp.stack([diff_lam_q1[0], diff_lam_k1[0], diff_lam_q2[0], diff_lam_k2[0]])
    subg = diff_subln_g

    dims = (c_qk, c_v, c_ql, c_kvl)
    inproj = functools.partial(_inproj, g1=g1, win=win_ext, qg=qg, wuq=wuq_ext, kvg=kvg, wukv=wukv, dims=dims)

    (_, mdk32, mdk, mdv32, mdv, _, mckv, _, mkm, mvm) = inproj(
        meta_tokens, jnp.zeros((N_META, LANES), F32), tm=N_META, tab_blocks=1)

    tm_p = min(256, s)
    tab_p = _rope_table(jnp.arange(s, dtype=jnp.int32))
    (pqdt, pkd32, pkd, pvd32, pvdt, pqmt, pckv, pkr, pkm, pvmt) = inproj(
        x_prompt.reshape(b * s, d), tab_p, tm=tm_p, tab_blocks=s // tm_p, batch=b)
    tq = min(512, s)
    r3 = lambda a: a.reshape(b, s, a.shape[-1])
    od = _diff_attn(pqdt, r3(pkd), pvdt, mdk, mdv.T, lamv, subg.T, tq=tq, lam_init=lam_init)
    om = _mla_attn(pqmt, r3(pkm), pvmt, mkm, mvm.T, tq=tq)
    t = b * s
    hp, xn2, rt = _merge(od.reshape(t, -1), om.reshape(t, -1), x_prompt.reshape(t, d), wo, norm2_g, wr, br,
                         tm=min(512, t))
    tm_e = min(512, t)
    pos, sorted_tok, sorted_w, tile_expert, n_used = _sort_by_expert(rt[2:4].astype(jnp.int32), rt[0:2], tm_e)
    xs = jnp.take(xn2, sorted_tok, axis=0)
    ys = _moe_sorted(tile_expert, n_used, xs, sorted_w[:, None], wg, wu, wd, tm=tm_e)
    y_prompt = _combine(hp, jnp.take(ys, pos[0], axis=0), jnp.take(ys, pos[1], axis=0), gf,
                        tm=min(512, t)).reshape(b, s, d)

    bc = lambda a: jnp.broadcast_to(a[None], (b,) + a.shape)
    p_dk = jnp.concatenate([bc(mdk32), r3(pkd32)], axis=1).reshape(1, b, N_META + s, DA_HEADS, 2 * DA_D)
    p_dv = jnp.concatenate([bc(mdv32), r3(pvd32)], axis=1).reshape(1, b, N_META + s, DA_HEADS, DA_V)
    p_ckv = jnp.concatenate([bc(mckv), r3(pckv)], axis=1)[None]
    p_kr = pkr.reshape(1, b, s, MLA_ROPE)

    ts = bs * ss
    s_pos = past + jnp.arange(ss, dtype=jnp.int32)
    (sqd, skd32, skd, svd32, svd, sqm, sckv, skr, _, _) = inproj(
        x_sample.reshape(ts, d), _rope_table(s_pos), tm=ss, tab_blocks=1)
    q3 = lambda a: a.reshape(bs, ss, a.shape[-1])
    kpos_c = np.arange(lc) - N_META
    dist_c = np.where(kpos_c[None, :] >= 0, np.abs(past + np.arange(ss)[:, None] - kpos_c[None, :]), 0)
    dist_n = np.abs(np.arange(ss)[:, None] - np.arange(ss)[None, :])
    tk_s = lc // 2 if (lc // 2) % 8 == 0 and lc % 2 == 0 else lc
    dist_c = np.tile(dist_c, (2, 1)).astype(np.float32).reshape(2 * ss, lc // tk_s, tk_s)
    dist_c = jnp.asarray(np.moveaxis(dist_c, 1, 0))
    dist_n = jnp.asarray(np.tile(dist_n, (2, 1)).astype(np.float32))
    sod = _sample_diff_attn(q3(sqd), cache_diff_k[0].reshape(bs, lc, c_qk), cache_diff_v[0].reshape(bs, lc, c_v),
                            q3(skd), q3(svd), dist_c, dist_n, lamv, subg, tk=tk_s, lam_init=lam_init)
    krc = jnp.concatenate([jnp.zeros((bs, N_META, MLA_ROPE), F32), cache_mla_kr[0]], axis=1)
    som = _sample_mla_attn(q3(sqm), cache_mla_ckv[0], krc, q3(sckv), q3(skr), wuk, wuv, tk=tk_s)
    hs, xn2s, rts = _merge(sod.reshape(ts, -1), som.reshape(ts, -1), x_sample.reshape(ts, d), wo, norm2_g, wr, br,
                           tm=ts)
    eids = rts[2:4].astype(jnp.int32)
    gates = (jnp.where(eids[0][:, None] == jnp.arange(LANES)[None, :], rts[0][:, None], 0.0)
             + jnp.where(eids[1][:, None] == jnp.arange(LANES)[None, :], rts[1][:, None], 0.0))
    y_sample = _moe_dense(xn2s, hs, gates, wg, wu, wd, gf).reshape(bs, ss, d)

    s_dk = skd32.reshape(1, bs, ss, DA_HEADS, 2 * DA_D)
    s_dv = svd32.reshape(1, bs, ss, DA_HEADS, DA_V)
    s_ckv = sckv.reshape(1, bs, ss, c_kvl)
    s_kr = skr.reshape(1, bs, ss, MLA_ROPE)
    return (y_prompt, y_sample, p_dk, p_dv, p_ckv, p_kr, s_dk, s_dv, s_ckv, s_kr)
```

```python
import functools
import math

import numpy as np
import jax
import jax.numpy as jnp
from jax import lax
from jax.experimental import pallas as pl
from jax.experimental.pallas import tpu as pltpu

CHUNK = 64
N_META = 16
EPS = 1e-6
DA_HEADS = 8
DA_D = 64
DA_V = 2 * DA_D
MLA_HEADS = 8
MLA_NOPE = 128
MLA_ROPE = 64
MLA_V = 128
ROPE_THETA = 10000.0
MLA_SCALE = (MLA_NOPE + MLA_ROPE) ** -0.5
N_GROUPS = 4
EXPERTS_PER_GROUP = 4
N_EXPERTS = N_GROUPS * EXPERTS_PER_GROUP
LOG2E = math.log2(math.e)
LANES = 128
SUBLANES = 8
BF16_ROWS = 16
MXU_DIM = 256
HEAD_W = 128
MLA_W = 256
VT_W = DA_V + BF16_ROWS
NEG_BIG = -1e30
VMEM_LIMIT = 56 * 1024 * 1024
ATTN_TILE = 512
INPROJ_ROWS = 256
TOKEN_ROWS = 512

BF16 = jnp.bfloat16
F32 = jnp.float32


def _dot(a, b):
    return jnp.dot(a, b, preferred_element_type=F32)


def _dot_nt(a, b):
    return lax.dot_general(a, b, (((1,), (1,)), ((), ())), preferred_element_type=F32)


def _rms(x, g):
    return x * lax.rsqrt(jnp.mean(x * x, axis=-1, keepdims=True) + EPS) * g


def _cparams(n_axes):
    return pltpu.CompilerParams(dimension_semantics=("arbitrary",) * n_axes,
                                vmem_limit_bytes=VMEM_LIMIT)


def _const_spec(shape):
    nd = len(shape)
    return pl.BlockSpec(shape, lambda *_: (0,) * nd, pipeline_mode=pl.Buffered(1))


def _alibi_slope(h):
    return 2.0 ** (-8.0 * (h + 1) / DA_HEADS)


def _inproj_kernel(x_ref, tab_ref, g1_ref, win_ref, qg_ref, wuq_ref, kvg_ref, wukv_ref,
                   qd_ref, kd32_ref, kdb_ref, vd32_ref, vdb_ref, qm_ref, ckv_ref, kr_ref,
                   km_ref, vm_ref, *, c_qk, c_v, c_ql, c_kvl, prompt):
    x = x_ref[...]
    tm = x.shape[0]
    xn = _rms(x, g1_ref[...]).astype(BF16)
    tab = tab_ref[...]
    o1 = c_qk
    o2 = o1 + c_qk
    o3 = o2 + c_v
    o4 = o3 + c_ql
    o5 = o4 + c_kvl
    ones = jnp.ones((BF16_ROWS, tm), BF16)

    def put_heads32(ref, z):
        for h in range(DA_HEADS):
            if prompt:
                ref[0, :, h, :] = z[:, h * HEAD_W:(h + 1) * HEAD_W]
            else:
                ref[:, h, :] = z[:, h * HEAD_W:(h + 1) * HEAD_W]

    def put_values_t(ref, z):
        for h in range(DA_HEADS):
            ref[h * VT_W:h * VT_W + DA_V, :] = z[:, h * DA_V:(h + 1) * DA_V].T.astype(BF16)
            ref[h * VT_W + DA_V:(h + 1) * VT_W, :] = ones

    zq = _dot(xn, win_ref[:, 0:o1]) * (DA_D ** -0.5 * LOG2E)
    qd_ref[...] = zq.T.astype(BF16) if prompt else zq.astype(BF16)
    zk = _dot(xn, win_ref[:, o1:o2])
    put_heads32(kd32_ref, zk)
    kdb_ref[...] = zk.astype(BF16)
    zv = _dot(xn, win_ref[:, o2:o3])
    put_heads32(vd32_ref, zv)
    if prompt:
        put_values_t(vdb_ref, zv)
    else:
        vdb_ref[...] = zv.astype(BF16)

    cq = _rms(_dot(xn, win_ref[:, o3:o4]), qg_ref[...]).astype(BF16)
    nq = MLA_HEADS * MLA_NOPE
    qn = _dot(cq, wuq_ref[:, 0:nq]) * (MLA_SCALE * LOG2E)
    qr = _dot(cq, wuq_ref[:, nq:2 * nq]) * (MLA_SCALE * LOG2E)
    for h in range(MLA_HEADS):
        u = qr[:, h * LANES:(h + 1) * LANES] * tab
        rot = u + pltpu.roll(u, MLA_ROPE, 1)
        nope = qn[:, h * LANES:(h + 1) * LANES]
        if prompt:
            qm_ref[h * MLA_W:h * MLA_W + LANES, :] = nope.T.astype(BF16)
            qm_ref[h * MLA_W + LANES:(h + 1) * MLA_W, :] = rot.T.astype(BF16)
        else:
            qm_ref[:, h * MLA_W:h * MLA_W + LANES] = nope.astype(BF16)
            qm_ref[:, h * MLA_W + LANES:(h + 1) * MLA_W] = rot.astype(BF16)

    ckv = _rms(_dot(xn, win_ref[:, o4:o5]), kvg_ref[...])
    if prompt:
        ckv_ref[0] = ckv
    else:
        ckv_ref[...] = ckv
    ckvb = ckv.astype(BF16)
    nk = MLA_HEADS * MLA_NOPE
    kn = _dot(ckvb, wukv_ref[:, 0:nk])
    vm = _dot(ckvb, wukv_ref[:, nk:nk + MLA_HEADS * MLA_V])
    if prompt:
        put_values_t(vm_ref, vm)
    else:
        vm_ref[...] = vm.astype(BF16)

    u = _dot(xn, win_ref[:, o5:o5 + LANES]) * tab
    rot = u + pltpu.roll(u, MLA_ROPE, 1)
    kr_ref[...] = rot[:, 0:MLA_ROPE]
    lane = lax.broadcasted_iota(jnp.int32, rot.shape, 1)
    krp = jnp.where(lane < MLA_ROPE, rot, 0.0).astype(BF16)
    for h in range(MLA_HEADS):
        km_ref[:, h * MLA_W:h * MLA_W + LANES] = kn[:, h * LANES:(h + 1) * LANES].astype(BF16)
        km_ref[:, h * MLA_W + LANES:(h + 1) * MLA_W] = krp


def _inproj(x, tab, g1, win, qg, wuq, kvg, wukv, *, tm, tab_blocks, dims, batch=None):
    m, d = x.shape
    c_qk, c_v, c_ql, c_kvl = dims
    assert m % tm == 0
    prompt = batch is not None
    row = lambda w: pl.BlockSpec((tm, w), lambda i: (i, 0))
    hw = DA_HEADS * VT_W
    if prompt:
        seq = tab_blocks * tm
        assert m == batch * seq
        E = pl.Element
        off = lambda i: pl.multiple_of(N_META + (i % tab_blocks) * tm, BF16_ROWS)
        col = lambda w: pl.BlockSpec((None, w, tm), lambda i: (i // tab_blocks, 0, i % tab_blocks))
        heads32 = pl.BlockSpec((E(1), E(tm), E(DA_HEADS), E(HEAD_W)), lambda i: (i // tab_blocks, off(i), 0, 0))
        ckv_spec = pl.BlockSpec((E(1), E(tm), E(c_kvl)), lambda i: (i // tab_blocks, off(i), 0))
        tall = (batch, N_META + seq)
        out_specs = [col(c_qk), heads32, row(c_qk), heads32, col(hw), col(MLA_HEADS * MLA_W), ckv_spec,
                     row(MLA_ROPE), row(MLA_HEADS * MLA_W), col(hw)]
        out_shape = [
            jax.ShapeDtypeStruct((batch, c_qk, seq), BF16),
            jax.ShapeDtypeStruct(tall + (DA_HEADS, HEAD_W), F32),
            jax.ShapeDtypeStruct((m, c_qk), BF16),
            jax.ShapeDtypeStruct(tall + (DA_HEADS, HEAD_W), F32),
            jax.ShapeDtypeStruct((batch, hw, seq), BF16),
            jax.ShapeDtypeStruct((batch, MLA_HEADS * MLA_W, seq), BF16),
            jax.ShapeDtypeStruct(tall + (c_kvl,), F32),
            jax.ShapeDtypeStruct((m, MLA_ROPE), F32),
            jax.ShapeDtypeStruct((m, MLA_HEADS * MLA_W), BF16),
            jax.ShapeDtypeStruct((batch, hw, seq), BF16),
        ]
    else:
        heads32 = pl.BlockSpec((tm, DA_HEADS, HEAD_W), lambda i: (i, 0, 0))
        out_specs = [row(c_qk), heads32, row(c_qk), heads32, row(c_v), row(MLA_HEADS * MLA_W), row(c_kvl),
                     row(MLA_ROPE), row(MLA_HEADS * MLA_W), row(MLA_HEADS * MLA_V)]
        out_shape = [
            jax.ShapeDtypeStruct((m, c_qk), BF16),
            jax.ShapeDtypeStruct((m, DA_HEADS, HEAD_W), F32),
            jax.ShapeDtypeStruct((m, c_qk), BF16),
            jax.ShapeDtypeStruct((m, DA_HEADS, HEAD_W), F32),
            jax.ShapeDtypeStruct((m, c_v), BF16),
            jax.ShapeDtypeStruct((m, MLA_HEADS * MLA_W), BF16),
            jax.ShapeDtypeStruct((m, c_kvl), F32),
            jax.ShapeDtypeStruct((m, MLA_ROPE), F32),
            jax.ShapeDtypeStruct((m, MLA_HEADS * MLA_W), BF16),
            jax.ShapeDtypeStruct((m, MLA_HEADS * MLA_V), BF16),
        ]
    return pl.pallas_call(
        functools.partial(_inproj_kernel, c_qk=c_qk, c_v=c_v, c_ql=c_ql, c_kvl=c_kvl, prompt=prompt),
        grid=(m // tm,),
        in_specs=[
            row(d),
            pl.BlockSpec((tm, LANES), lambda i: (i % tab_blocks, 0)),
            _const_spec(g1.shape), _const_spec(win.shape), _const_spec(qg.shape),
            _const_spec(wuq.shape), _const_spec(kvg.shape), _const_spec(wukv.shape),
        ],
        out_specs=out_specs,
        out_shape=out_shape,
        compiler_params=_cparams(1),
        name="inproj",
    )(x, tab, g1, win, qg, wuq, kvg, wukv)


def _meta_fill_kernel(rows_ref, big_ref, o_ref):
    del big_ref
    o_ref[...] = rows_ref[...]


def _meta_fill(rows, big):
    nb = big.shape[0]
    tail = big.shape[2:]
    zeros = (0,) * len(tail)
    return pl.pallas_call(
        _meta_fill_kernel,
        grid=(nb,),
        in_specs=[pl.BlockSpec(rows.shape, lambda b: (0,) + zeros),
                  pl.BlockSpec(memory_space=pl.ANY)],
        out_specs=pl.BlockSpec((None, N_META) + tail, lambda b: (b, 0) + zeros),
        out_shape=jax.ShapeDtypeStruct(big.shape, big.dtype),
        input_output_aliases={1: 0},
        compiler_params=_cparams(1),
        name="meta_fill",
    )(rows, big)


def _softmax_seed(s, v, m_ref, l_ref, acc_ref, j):
    m = jnp.max(s, axis=1, keepdims=True)
    p = jnp.exp2(s - m)
    m_ref[j] = m
    l_ref[j] = jnp.sum(p, axis=1, keepdims=True)
    acc_ref[j] = _dot(p.astype(BF16), v)


def _softmax_step(s, v, m_ref, l_ref, acc_ref, j):
    m_old = m_ref[j]
    m_new = jnp.maximum(m_old, jnp.max(s, axis=1, keepdims=True))
    alpha = jnp.exp2(m_old - m_new)
    p = jnp.exp2(s - m_new)
    l_ref[j] = alpha * l_ref[j] + jnp.sum(p, axis=1, keepdims=True)
    acc_ref[j] = alpha * acc_ref[j] + _dot(p.astype(BF16), v)
    m_ref[j] = m_new


def _diff_lambda(lamv, lam_init):
    a = jnp.sum(lamv[0:1] * lamv[1:2], axis=1, keepdims=True)
    b = jnp.sum(lamv[2:3] * lamv[3:4], axis=1, keepdims=True)
    return jnp.exp(a) - jnp.exp(b) + lam_init


def _split_maps(q):
    lane = lax.broadcasted_iota(jnp.int32, q.shape, 1)
    zero = jnp.zeros_like(q)
    return jnp.where(lane < DA_D, q, zero), jnp.where(lane >= DA_D, q, zero)


def _seed_t(st, vt, m_ref, acc_ref, j):
    m = jnp.max(st, axis=0, keepdims=True)
    m_ref[j] = m
    acc_ref[j] = _dot(vt, jnp.exp2(st - m).astype(BF16))


def _step_t(st, shift, vt, m_ref, acc_ref, j):
    m_old = m_ref[j]
    m_new = jnp.maximum(m_old, jnp.max(st, axis=0, keepdims=True) + shift)
    p = jnp.exp2(st - (m_new - shift))
    acc_ref[j] = jnp.exp2(m_old - m_new) * acc_ref[j] + _dot(vt, p.astype(BF16))
    m_ref[j] = m_new


def _diff_attn_kernel(qi_ref, ki_ref, qt_ref, k_ref, vt_ref, mk_ref, mvt_ref, pos_ref, cq_ref, corr_ref,
                      lamv_ref, g_ref, o_ref, m_s, acc_s, *, tq, lam_init):
    t = pl.program_id(1)
    qi = qi_ref[t]
    ki = ki_ref[t]
    qrow = lax.broadcasted_iota(jnp.int32, (HEAD_W, tq), 0) < DA_D
    klane = lax.broadcasted_iota(jnp.int32, (tq, HEAD_W), 1) < DA_D

    @pl.when(ki == 0)
    def _seed():
        for h in range(DA_HEADS):
            hs = slice(h * HEAD_W, (h + 1) * HEAD_W)
            qt = qt_ref[hs, :]
            zero = jnp.zeros_like(qt)
            for c in range(2):
                qc = jnp.where(qrow, qt, zero) if c == 0 else jnp.where(qrow, zero, qt)
                _seed_t(_dot(mk_ref[:, hs], qc), mvt_ref[h * VT_W:(h + 1) * VT_W, :], m_s, acc_s, 2 * h + c)

    def body(diag):
        qpos = (lax.broadcasted_iota(jnp.int32, (1, tq), 1) + (qi - ki) * tq).astype(F32)
        pos = pos_ref[...]
        for h in range(DA_HEADS):
            hs = slice(h * HEAD_W, (h + 1) * HEAD_W)
            c_h = _alibi_slope(h) * LOG2E
            shift = qpos * (-c_h)
            qt = qt_ref[hs, :]
            cq = cq_ref[h]
            kk = k_ref[:, hs]
            vt = vt_ref[h * VT_W:(h + 1) * VT_W, :]
            for c in range(2):
                qc = jnp.where(qrow, qt, cq) if c == 0 else jnp.where(qrow, cq, qt)
                kc = jnp.where(klane, kk, pos) if c == 0 else jnp.where(klane, pos, kk)
                st = _dot(kc, qc)
                if diag:
                    st = st + corr_ref[...] * c_h
                _step_t(st, shift, vt, m_s, acc_s, 2 * h + c)

    @pl.when(ki != qi)
    def _off_diagonal():
        body(False)

    @pl.when(ki == qi)
    def _diagonal():
        body(True)
        lam = _diff_lambda(lamv_ref[...], lam_init)
        g = g_ref[...]
        for h in range(DA_HEADS):
            a0 = acc_s[2 * h]
            a1 = acc_s[2 * h + 1]
            ot = a0[0:DA_V] / a0[DA_V:DA_V + 1] - lam * (a1[0:DA_V] / a1[DA_V:DA_V + 1])
            ot = ot * lax.rsqrt(jnp.mean(ot * ot, axis=0, keepdims=True) + EPS) * g * (1.0 - lam_init)
            o_ref[:, h * HEAD_W:(h + 1) * HEAD_W] = ot.T.astype(BF16)


def _mla_attn_kernel(qi_ref, ki_ref, qt_ref, k_ref, vt_ref, mk_ref, mvt_ref, mask_ref, o_ref,
                     m_s, acc_s):
    t = pl.program_id(1)
    qi = qi_ref[t]
    ki = ki_ref[t]

    @pl.when(ki == 0)
    def _seed():
        for h in range(MLA_HEADS):
            st = _dot(mk_ref[:, h * MLA_W:(h + 1) * MLA_W], qt_ref[h * MLA_W:(h + 1) * MLA_W, :])
            _seed_t(st, mvt_ref[h * VT_W:(h + 1) * VT_W, :], m_s, acc_s, h)

    def body(diag):
        for h in range(MLA_HEADS):
            st = _dot(k_ref[:, h * MLA_W:(h + 1) * MLA_W], qt_ref[h * MLA_W:(h + 1) * MLA_W, :])
            if diag:
                st = st + mask_ref[...]
            _step_t(st, 0.0, vt_ref[h * VT_W:(h + 1) * VT_W, :], m_s, acc_s, h)

    @pl.when(ki != qi)
    def _off_diagonal():
        body(False)

    @pl.when(ki == qi)
    def _diagonal():
        body(True)
        for h in range(MLA_HEADS):
            a = acc_s[h]
            o_ref[:, h * MLA_V:(h + 1) * MLA_V] = (a[0:MLA_V] / a[MLA_V:MLA_V + 1]).T.astype(BF16)


def _pair_tables(nq):
    qi = np.concatenate([np.full((i + 1,), i, np.int32) for i in range(nq)])
    ki = np.concatenate([np.arange(i + 1, dtype=np.int32) for i in range(nq)])
    return jnp.asarray(qi), jnp.asarray(ki)


def _tile_geometry(tq):
    j = np.arange(tq)[:, None]
    i = np.arange(tq)[None, :]
    visible = (j // CHUNK) <= (i // CHUNK)
    return i, j, visible


def _bf16_split3(x):
    parts = []
    for _ in range(3):
        p = float(np.asarray(x, np.float32).astype(BF16).astype(np.float32))
        parts.append(p)
        x = x - p
    return parts


def _alibi_operands(tq):
    assert tq <= 2 * MXU_DIM
    j = np.arange(tq)
    jlo = (j % MXU_DIM).astype(np.float32)
    jhi = (j - j % MXU_DIM).astype(np.float32)
    pos = np.zeros((tq, HEAD_W), np.float32)
    cq = np.zeros((DA_HEADS, HEAD_W, tq), np.float32)
    for base in (0, DA_D):
        for r in range(3):
            pos[:, base + 2 * r] = jlo
            pos[:, base + 2 * r + 1] = jhi
    for h in range(DA_HEADS):
        parts = _bf16_split3(_alibi_slope(h) * LOG2E)
        for base in (0, DA_D):
            for r in range(3):
                cq[h, base + 2 * r, :] = parts[r]
                cq[h, base + 2 * r + 1, :] = parts[r]
    return jnp.asarray(pos, BF16), jnp.asarray(cq, BF16)


def _prompt_attn_specs(tq, wq, wk, wv):
    qt_spec = pl.BlockSpec((None, wq, tq), lambda b, t, qi, ki: (b, 0, qi[t]))
    k_spec = pl.BlockSpec((None, tq, wk), lambda b, t, qi, ki: (b, ki[t], 0))
    vt_spec = pl.BlockSpec((None, wv, tq), lambda b, t, qi, ki: (b, 0, ki[t]))
    mk_spec = pl.BlockSpec((N_META, wk), lambda b, t, qi, ki: (0, 0))
    mvt_spec = pl.BlockSpec((wv, N_META), lambda b, t, qi, ki: (0, 0))
    return qt_spec, k_spec, vt_spec, mk_spec, mvt_spec


def _diff_attn(qt, k, vt, mk, mvt, lamv, g, *, tq, lam_init):
    b, s, w = k.shape
    nq = s // tq
    qi, ki = _pair_tables(nq)
    i, j, visible = _tile_geometry(tq)
    corr = jnp.asarray(np.where(visible, np.where(j > i, -2.0 * (j - i), 0.0), NEG_BIG).astype(np.float32))
    pos, cq = _alibi_operands(tq)
    qt_spec, k_spec, vt_spec, mk_spec, mvt_spec = _prompt_attn_specs(tq, w, w, vt.shape[1])
    full = lambda a: pl.BlockSpec(a.shape, lambda b_, t, qi_, ki_: (0,) * a.ndim)
    return pl.pallas_call(
        functools.partial(_diff_attn_kernel, tq=tq, lam_init=lam_init),
        grid_spec=pltpu.PrefetchScalarGridSpec(
            num_scalar_prefetch=2,
            grid=(b, int(qi.shape[0])),
            in_specs=[qt_spec, k_spec, vt_spec, mk_spec, mvt_spec, full(pos), full(cq), full(corr),
                      full(lamv), full(g)],
            out_specs=pl.BlockSpec((None, tq, w), lambda b_, t, qi_, ki_: (b_, qi_[t], 0)),
            scratch_shapes=[pltpu.VMEM((2 * DA_HEADS, 1, tq), F32), pltpu.VMEM((2 * DA_HEADS, VT_W, tq), F32)],
        ),
        out_shape=jax.ShapeDtypeStruct((b, s, w), BF16),
        compiler_params=_cparams(2),
        name="diff_attn",
    )(qi, ki, qt, k, vt, mk, mvt, pos, cq, corr, lamv, g)


def _mla_attn(qt, k, vt, mk, mvt, *, tq):
    b, s, wq = k.shape
    nq = s // tq
    qi, ki = _pair_tables(nq)
    _, _, visible = _tile_geometry(tq)
    mask = jnp.asarray(np.where(visible, 0.0, NEG_BIG).astype(np.float32))
    qt_spec, k_spec, vt_spec, mk_spec, mvt_spec = _prompt_attn_specs(tq, wq, wq, vt.shape[1])
    wo = MLA_HEADS * MLA_V
    return pl.pallas_call(
        _mla_attn_kernel,
        grid_spec=pltpu.PrefetchScalarGridSpec(
            num_scalar_prefetch=2,
            grid=(b, int(qi.shape[0])),
            in_specs=[qt_spec, k_spec, vt_spec, mk_spec, mvt_spec,
                      pl.BlockSpec(mask.shape, lambda b_, t, qi_, ki_: (0, 0))],
            out_specs=pl.BlockSpec((None, tq, wo), lambda b_, t, qi_, ki_: (b_, qi_[t], 0)),
            scratch_shapes=[pltpu.VMEM((MLA_HEADS, 1, tq), F32), pltpu.VMEM((MLA_HEADS, VT_W, tq), F32)],
        ),
        out_shape=jax.ShapeDtypeStruct((b, s, wo), BF16),
        compiler_params=_cparams(2),
        name="mla_attn",
    )(qi, ki, qt, k, vt, mk, mvt, mask)


def _sample_diff_kernel(q_ref, kc_ref, vc_ref, kn_ref, vn_ref, dc_ref, dn_ref, lamv_ref, g_ref,
                        o_ref, m_s, l_s, acc_s, *, lam_init):
    kt = pl.program_id(1)
    nkt = pl.num_programs(1)

    def heads(get_k, get_v, dist, first):
        for h in range(DA_HEADS):
            q1, q2 = _split_maps(q_ref[:, h * HEAD_W:(h + 1) * HEAD_W])
            qq = jnp.concatenate([q1, q2], axis=0)
            s = _dot_nt(qq, get_k(h)) + dist * (-_alibi_slope(h) * LOG2E)
            if first:
                _softmax_seed(s, get_v(h), m_s, l_s, acc_s, h)
            else:
                _softmax_step(s, get_v(h), m_s, l_s, acc_s, h)

    cache_k = lambda h: kc_ref[:, h, :].astype(BF16)
    cache_v = lambda h: vc_ref[:, h, :].astype(BF16)

    @pl.when(kt == 0)
    def _first():
        heads(cache_k, cache_v, dc_ref[...], True)

    @pl.when(kt > 0)
    def _rest():
        heads(cache_k, cache_v, dc_ref[...], False)

    @pl.when(kt == nkt - 1)
    def _finish():
        heads(lambda h: kn_ref[:, h * HEAD_W:(h + 1) * HEAD_W], lambda h: vn_ref[:, h * HEAD_W:(h + 1) * HEAD_W],
              dn_ref[...], False)
        lam = _diff_lambda(lamv_ref[...], lam_init)
        g = g_ref[...]
        nq = q_ref.shape[0]
        for h in range(DA_HEADS):
            a = acc_s[h] / l_s[h]
            o = a[0:nq] - lam * a[nq:2 * nq]
            o_ref[:, h * HEAD_W:(h + 1) * HEAD_W] = (_rms(o, g) * (1.0 - lam_init)).astype(BF16)


def _sample_diff_attn(q, kc, vc, kn, vn, dist_c, dist_n, lamv, g, *, tk, lam_init):
    bs, nq, w = q.shape
    lc = kc.shape[1]
    assert lc % tk == 0
    full = lambda a: pl.BlockSpec(a.shape, lambda b, t: (0,) * a.ndim)
    per_stream = lambda a: pl.BlockSpec((None,) + a.shape[1:], lambda b, t: (b,) + (0,) * (a.ndim - 1))
    cache = pl.BlockSpec((None, tk, DA_HEADS, HEAD_W), lambda b, t: (b, t, 0, 0))
    return pl.pallas_call(
        functools.partial(_sample_diff_kernel, lam_init=lam_init),
        grid=(bs, lc // tk),
        in_specs=[per_stream(q), cache, cache, per_stream(kn), per_stream(vn),
                  pl.BlockSpec((None, 2 * nq, tk), lambda b, t: (t, 0, 0)),
                  full(dist_n), full(lamv), full(g)],
        out_specs=per_stream(q),
        out_shape=jax.ShapeDtypeStruct((bs, nq, w), BF16),
        scratch_shapes=[pltpu.VMEM((DA_HEADS, 2 * nq, 1), F32), pltpu.VMEM((DA_HEADS, 2 * nq, 1), F32),
                        pltpu.VMEM((DA_HEADS, 2 * nq, DA_V), F32)],
        compiler_params=_cparams(2),
        name="sample_diff_attn",
    )(q, kc, vc, kn, vn, dist_c, dist_n, lamv, g)


def _sample_mla_kernel(q_ref, cc_ref, krc_ref, cn_ref, krn_ref, wuk_ref, wuv_ref, o_ref,
                       ql_s, qr_s, m_s, l_s, acc_s):
    kt = pl.program_id(1)
    nkt = pl.num_programs(1)
    nq = q_ref.shape[0]

    @pl.when(kt == 0)
    def _prep():
        for h in range(MLA_HEADS):
            qn = q_ref[:, h * MLA_W:h * MLA_W + MLA_NOPE]
            ql_s[h * nq:(h + 1) * nq, :] = _dot_nt(qn, wuk_ref[:, h * MLA_NOPE:(h + 1) * MLA_NOPE]).astype(BF16)
            qr_s[h * nq:(h + 1) * nq, :] = q_ref[:, h * MLA_W + MLA_NOPE:(h + 1) * MLA_W]

    def scores(c_ref, kr_ref):
        cb = c_ref[...].astype(BF16)
        krb = kr_ref[...].astype(BF16)
        s = _dot_nt(ql_s[...], cb) + _dot_nt(qr_s[:, 0:MLA_ROPE], krb)
        return s, cb

    @pl.when(kt == 0)
    def _first():
        s, cb = scores(cc_ref, krc_ref)
        _softmax_seed(s, cb, m_s, l_s, acc_s, 0)

    @pl.when(kt > 0)
    def _rest():
        s, cb = scores(cc_ref, krc_ref)
        _softmax_step(s, cb, m_s, l_s, acc_s, 0)

    @pl.when(kt == nkt - 1)
    def _finish():
        s, cb = scores(cn_ref, krn_ref)
        _softmax_step(s, cb, m_s, l_s, acc_s, 0)
        ol = (acc_s[0] / l_s[0]).astype(BF16)
        for h in range(MLA_HEADS):
            o_ref[:, h * MLA_V:(h + 1) * MLA_V] = _dot(
                ol[h * nq:(h + 1) * nq, :], wuv_ref[:, h * MLA_V:(h + 1) * MLA_V]).astype(BF16)


def _sample_mla_attn(q, cc, krc, cn, krn, wuk, wuv, *, tk):
    bs, nq, wq = q.shape
    lc, kvl = cc.shape[1], cc.shape[2]
    assert lc % tk == 0
    full = lambda a: pl.BlockSpec(a.shape, lambda b, t: (0,) * a.ndim)
    per_stream = lambda a: pl.BlockSpec((None,) + a.shape[1:], lambda b, t: (b,) + (0,) * (a.ndim - 1))
    rows = MLA_HEADS * nq
    return pl.pallas_call(
        _sample_mla_kernel,
        grid=(bs, lc // tk),
        in_specs=[per_stream(q),
                  pl.BlockSpec((None, tk, kvl), lambda b, t: (b, t, 0)),
                  pl.BlockSpec((None, tk, MLA_ROPE), lambda b, t: (b, t, 0)),
                  per_stream(cn), per_stream(krn), full(wuk), full(wuv)],
        out_specs=pl.BlockSpec((None, nq, MLA_HEADS * MLA_V), lambda b, t: (b, 0, 0)),
        out_shape=jax.ShapeDtypeStruct((bs, nq, MLA_HEADS * MLA_V), BF16),
        scratch_shapes=[pltpu.VMEM((rows, kvl), BF16), pltpu.VMEM((rows, LANES), BF16),
                        pltpu.VMEM((1, rows, 1), F32), pltpu.VMEM((1, rows, 1), F32),
                        pltpu.VMEM((1, rows, kvl), F32)],
        compiler_params=_cparams(2),
        name="sample_mla_attn",
    )(q, cc, krc, cn, krn, wuk, wuv)


ROUTER_ROWS = SUBLANES * (1 + N_GROUPS)


def _route(lt):
    g = [lt[i:i + 1] for i in range(N_GROUPS)]
    gmax = functools.reduce(jnp.maximum, g)
    gidx = jnp.full_like(gmax, float(N_GROUPS - 1))
    for i in range(N_GROUPS - 2, -1, -1):
        gidx = jnp.where(g[i] == gmax, float(i), gidx)
    den = functools.reduce(lambda a, b: a + b, [jnp.exp(gi - gmax) for gi in g])
    p_top = 1.0 / den
    e = []
    for j in range(EXPERTS_PER_GROUP):
        ej = lt[SUBLANES * N_GROUPS + j:SUBLANES * N_GROUPS + j + 1]
        for grp in range(N_GROUPS - 2, -1, -1):
            ej = jnp.where(gidx == float(grp), lt[SUBLANES * (grp + 1) + j:SUBLANES * (grp + 1) + j + 1], ej)
        e.append(ej)

    def first_argmax(vals):
        vmax = functools.reduce(jnp.maximum, vals)
        idx = jnp.full_like(vmax, float(len(vals) - 1))
        for i in range(len(vals) - 2, -1, -1):
            idx = jnp.where(vals[i] == vmax, float(i), idx)
        return vmax, idx

    v1, i1 = first_argmax(e)
    rest = [jnp.where(i1 == float(j), -jnp.inf, e[j]) for j in range(EXPERTS_PER_GROUP)]
    v2, i2 = first_argmax(rest)
    r = jnp.exp(v2 - v1)
    w1 = p_top / (1.0 + r)
    w2 = p_top * r / (1.0 + r)
    base = gidx * float(EXPERTS_PER_GROUP)
    return w1, w2, base + i1, base + i2


def _merge_kernel(od_ref, om_ref, x_ref, wo_ref, g2_ref, wr_ref, br_ref, hp_ref, xn_ref, rt_ref):
    nd = od_ref.shape[1]
    y = _dot(od_ref[...], wo_ref[0:nd, :]) + _dot(om_ref[...], wo_ref[nd:, :])
    hp = x_ref[...] + y
    hp_ref[...] = hp
    xn = _rms(hp, g2_ref[...]).astype(BF16)
    xn_ref[...] = xn
    lt = _dot_nt(wr_ref[...], xn) + br_ref[...]
    rows = _route(lt)
    for i, r in enumerate(rows):
        rt_ref[i:i + 1, :] = r
    rt_ref[4:8, :] = jnp.zeros((4, rt_ref.shape[1]), F32)


def _merge(od, om, x, wo, g2, wr, br, *, tm):
    m, d = x.shape
    row = lambda w: pl.BlockSpec((tm, w), lambda i: (i, 0))
    return pl.pallas_call(
        _merge_kernel,
        grid=(m // tm,),
        in_specs=[row(od.shape[1]), row(om.shape[1]), row(d), _const_spec(wo.shape),
                  _const_spec(g2.shape), _const_spec(wr.shape), _const_spec(br.shape)],
        out_specs=[row(d), row(d), pl.BlockSpec((SUBLANES, tm), lambda i: (0, i))],
        out_shape=[jax.ShapeDtypeStruct((m, d), F32), jax.ShapeDtypeStruct((m, d), BF16),
                   jax.ShapeDtypeStruct((SUBLANES, m), F32)],
        compiler_params=_cparams(1),
        name="merge",
    )(od, om, x, wo, g2, wr, br)


def _swiglu(x, wg, wu, wd):
    g = _dot(x, wg)
    u = _dot(x, wu)
    h = (g * jax.nn.sigmoid(g) * u).astype(BF16)
    return _dot(h, wd)


def _moe_sorted_kernel(te_ref, nu_ref, x_ref, w_ref, wg_ref, wu_ref, wd_ref, y_ref):
    i = pl.program_id(0)

    @pl.when(i < nu_ref[0])
    def _():
        y_ref[...] = w_ref[...] * _swiglu(x_ref[...], wg_ref[...], wu_ref[...], wd_ref[...])

    @pl.when(i >= nu_ref[0])
    def _():
        y_ref[...] = jnp.zeros_like(y_ref)


def _moe_sorted(tile_expert, n_used, xs, ws, wg, wu, wd, *, tm):
    n, d = xs.shape
    f = wg.shape[2]
    nt = n // tm
    return pl.pallas_call(
        _moe_sorted_kernel,
        grid_spec=pltpu.PrefetchScalarGridSpec(
            num_scalar_prefetch=2,
            grid=(nt,),
            in_specs=[pl.BlockSpec((tm, d), lambda i, te, nu: (i, 0)),
                      pl.BlockSpec((tm, 1), lambda i, te, nu: (i, 0)),
                      pl.BlockSpec((None, d, f), lambda i, te, nu: (te[i], 0, 0)),
                      pl.BlockSpec((None, d, f), lambda i, te, nu: (te[i], 0, 0)),
                      pl.BlockSpec((None, f, d), lambda i, te, nu: (te[i], 0, 0))],
            out_specs=pl.BlockSpec((tm, d), lambda i, te, nu: (i, 0)),
        ),
        out_shape=jax.ShapeDtypeStruct((n, d), F32),
        compiler_params=_cparams(1),
        name="moe_sorted",
    )(tile_expert, n_used, xs, ws, wg, wu, wd)


def _moe_dense_kernel(x_ref, hp_ref, gates_ref, wg_ref, wu_ref, wd_ref, gf_ref, o_ref, acc_s):
    e = pl.program_id(0)

    @pl.when(e == 0)
    def _():
        acc_s[...] = jnp.zeros_like(acc_s)

    lane = lax.broadcasted_iota(jnp.int32, gates_ref.shape, 1)
    gate = jnp.sum(jnp.where(lane == e, gates_ref[...], 0.0), axis=1, keepdims=True)
    acc_s[...] += gate * _swiglu(x_ref[...], wg_ref[...], wu_ref[...], wd_ref[...])

    @pl.when(e == pl.num_programs(0) - 1)
    def _():
        o_ref[...] = _rms(hp_ref[...] + acc_s[...], gf_ref[...])


def _moe_dense(xn, hp, gates, wg, wu, wd, gf):
    m, d = xn.shape
    ne, _, f = wg.shape
    full = lambda a: pl.BlockSpec(a.shape, lambda e: (0,) * a.ndim)
    return pl.pallas_call(
        _moe_dense_kernel,
        grid=(ne,),
        in_specs=[full(xn), full(hp), full(gates),
                  pl.BlockSpec((None, d, f), lambda e: (e, 0, 0)),
                  pl.BlockSpec((None, d, f), lambda e: (e, 0, 0)),
                  pl.BlockSpec((None, f, d), lambda e: (e, 0, 0)),
                  full(gf)],
        out_specs=full(hp),
        out_shape=jax.ShapeDtypeStruct((m, d), F32),
        scratch_shapes=[pltpu.VMEM((m, d), F32)],
        compiler_params=_cparams(1),
        name="moe_dense",
    )(xn, hp, gates, wg, wu, wd, gf)


def _combine_kernel(hp_ref, y1_ref, y2_ref, gf_ref, o_ref):
    o_ref[...] = _rms(hp_ref[...] + (y1_ref[...] + y2_ref[...]), gf_ref[...])


def _combine(hp, y1, y2, gf, *, tm):
    m, d = hp.shape
    row = pl.BlockSpec((tm, d), lambda i: (i, 0))
    return pl.pallas_call(
        _combine_kernel,
        grid=(m // tm,),
        in_specs=[row, row, row, _const_spec(gf.shape)],
        out_specs=row,
        out_shape=jax.ShapeDtypeStruct((m, d), F32),
        compiler_params=_cparams(1),
        name="combine",
    )(hp, y1, y2, gf)


def _rope_table(pos):
    half = MLA_ROPE // 2
    inv_freq = ROPE_THETA ** (-jnp.arange(half, dtype=F32) / half)
    ang = pos.astype(F32)[:, None] * inv_freq[None, :]
    c, s = jnp.cos(ang), jnp.sin(ang)
    return jnp.concatenate([c, c, -s, s], axis=1)


def _swap_halves(w):
    half = MLA_ROPE // 2
    return jnp.concatenate([w[..., half:], w[..., :half]], axis=-1)


def _values_t(v):
    n = v.shape[0]
    vt = v.reshape(n, DA_HEADS, DA_V).transpose(1, 2, 0)
    return jnp.concatenate([vt, jnp.ones((DA_HEADS, BF16_ROWS, n), v.dtype)], axis=1).reshape(DA_HEADS * VT_W, n)


def _sort_by_expert(eid, w, tm):
    t = eid.shape[1]
    flat_e = eid.reshape(-1)
    onehot = (flat_e[:, None] == jnp.arange(N_EXPERTS, dtype=jnp.int32)[None, :]).astype(jnp.int32)
    rank = jnp.sum((jnp.cumsum(onehot, axis=0) - onehot) * onehot, axis=1)
    counts = jnp.sum(onehot, axis=0)
    tiles_per = (counts + tm - 1) // tm
    tiles_end = jnp.cumsum(tiles_per)
    row_start = (tiles_end - tiles_per) * tm
    pos = row_start[flat_e] + rank
    n_tiles = (2 * t) // tm + N_EXPERTS
    tok = jnp.tile(jnp.arange(t, dtype=jnp.int32), 2)
    sorted_tok = jnp.zeros((n_tiles * tm,), jnp.int32).at[pos].set(tok)
    sorted_w = jnp.zeros((n_tiles * tm,), F32).at[pos].set(w.reshape(-1))
    tile_ids = jnp.arange(n_tiles, dtype=jnp.int32)
    tile_expert = jnp.minimum(jnp.sum((tiles_end[None, :] <= tile_ids[:, None]).astype(jnp.int32), axis=1),
                              N_EXPERTS - 1)
    n_used = tiles_end[-1:].astype(jnp.int32)
    return pos.reshape(2, t), sorted_tok, sorted_w, tile_expert, n_used


def kernel(x_prompt, x_sample, cache_diff_k, cache_diff_v, cache_mla_ckv, cache_mla_kr, meta_tokens, norm1_g, w_in, diff_lam_q1, diff_lam_k1, diff_lam_q2, diff_lam_k2, diff_subln_g, mla_q_norm_g, mla_w_uq, mla_kv_norm_g, mla_w_uk, mla_w_uv, w_o, norm2_g, router_group_w, router_group_b, router_expert_w, router_expert_b, expert_w_gate, expert_w_up, expert_w_down, final_norm_g):
    depth = norm1_g.shape[0]
    assert depth == 1, "single-layer step only"
    assert MLA_HEADS == DA_HEADS and MLA_V == DA_V
    lam_init = 0.8 - 0.6 * math.exp(-0.3 * 0)
    b, s, d = x_prompt.shape
    bs, ss, _ = x_sample.shape
    past = cache_mla_kr.shape[2]
    lc = N_META + past
    c_qk = DA_HEADS * 2 * DA_D
    c_v = DA_HEADS * DA_V
    c_ql = mla_q_norm_g.shape[1]
    c_kvl = mla_kv_norm_g.shape[1]
    o5 = 2 * c_qk + c_v + c_ql + c_kvl

    win = w_in[0]
    win_ext = jnp.concatenate([win, _swap_halves(win[:, o5:])], axis=1).astype(BF16)
    wuq = mla_w_uq[0].reshape(c_ql, MLA_HEADS, MLA_NOPE + MLA_ROPE)
    wuq_n = wuq[:, :, :MLA_NOPE].reshape(c_ql, MLA_HEADS * MLA_NOPE)
    wuq_r = jnp.concatenate([wuq[:, :, MLA_NOPE:], _swap_halves(wuq[:, :, MLA_NOPE:])], axis=2)
    wuq_ext = jnp.concatenate([wuq_n, wuq_r.reshape(c_ql, MLA_HEADS * LANES)], axis=1).astype(BF16)
    wuk = mla_w_uk[0].astype(BF16)
    wuv = mla_w_uv[0].astype(BF16)
    wukv = jnp.concatenate([wuk, wuv], axis=1)
    wo = w_o[0].astype(BF16)
    wr = jnp.zeros((ROUTER_ROWS, d), F32).at[0:N_GROUPS].set(router_group_w[0].T)
    br = jnp.zeros((ROUTER_ROWS, 1), F32).at[0:N_GROUPS, 0].set(router_group_b[0])
    rew = router_expert_w[0].T.reshape(N_GROUPS, EXPERTS_PER_GROUP, d)
    reb = router_expert_b[0].reshape(N_GROUPS, EXPERTS_PER_GROUP)
    for grp in range(N_GROUPS):
        wr = wr.at[SUBLANES * (grp + 1):SUBLANES * (grp + 1) + EXPERTS_PER_GROUP].set(rew[grp])
        br = br.at[SUBLANES * (grp + 1):SUBLANES * (grp + 1) + EXPERTS_PER_GROUP, 0].set(reb[grp])
    wr = wr.astype(BF16)
    wg = expert_w_gate[0].astype(BF16)
    wu = expert_w_up[0].astype(BF16)
    wd = expert_w_down[0].astype(BF16)
    gf = final_norm_g[None, :]
    lamv = jnp.stack([diff_lam_q1[0], diff_lam_k1[0], diff_lam_q2[0], diff_lam_k2[0]])
    subg = diff_subln_g

    dims = (c_qk, c_v, c_ql, c_kvl)
    inproj = functools.partial(_inproj, g1=norm1_g, win=win_ext, qg=mla_q_norm_g, wuq=wuq_ext,
                               kvg=mla_kv_norm_g, wukv=wukv, dims=dims)

    (_, mdk32, mdk, mdv32, mdv, _, mckv, _, mkm, mvm) = inproj(
        meta_tokens, jnp.zeros((N_META, LANES), F32), tm=N_META, tab_blocks=1)

    tm_p = min(INPROJ_ROWS, s)
    tab_p = _rope_table(jnp.arange(s, dtype=jnp.int32))
    (pqdt, p_dk, pkd, p_dv, pvdt, pqmt, p_ckv, pkr, pkm, pvmt) = inproj(
        x_prompt.reshape(b * s, d), tab_p, tm=tm_p, tab_blocks=s // tm_p, batch=b)
    p_dk = _meta_fill(mdk32, p_dk)
    p_dv = _meta_fill(mdv32, p_dv)
    p_ckv = _meta_fill(mckv, p_ckv)
    tq = min(ATTN_TILE, s)
    r3 = lambda a: a.reshape(b, s, a.shape[-1])
    od = _diff_attn(pqdt, r3(pkd), pvdt, mdk, _values_t(mdv), lamv, subg.T, tq=tq, lam_init=lam_init)
    om = _mla_attn(pqmt, r3(pkm), pvmt, mkm, _values_t(mvm), tq=tq)
    t = b * s
    tm_t = min(TOKEN_ROWS, t)
    hp, xn2, rt = _merge(od.reshape(t, -1), om.reshape(t, -1), x_prompt.reshape(t, d), wo, norm2_g, wr, br, tm=tm_t)
    pos, sorted_tok, sorted_w, tile_expert, n_used = _sort_by_expert(rt[2:4].astype(jnp.int32), rt[0:2], tm_t)
    xs = jnp.take(xn2, sorted_tok, axis=0)
    ys = _moe_sorted(tile_expert, n_used, xs, sorted_w[:, None], wg, wu, wd, tm=tm_t)
    y_prompt = _combine(hp, jnp.take(ys, pos[0], axis=0), jnp.take(ys, pos[1], axis=0), gf, tm=tm_t).reshape(b, s, d)

    ts = bs * ss
    s_pos = past + jnp.arange(ss, dtype=jnp.int32)
    (sqd, s_dk, skd, s_dv, svd, sqm, sckv, skr, _, _) = inproj(
        x_sample.reshape(ts, d), _rope_table(s_pos), tm=ss, tab_blocks=1)
    q3 = lambda a: a.reshape(bs, ss, a.shape[-1])
    kpos_c = np.arange(lc) - N_META
    dist_c = np.where(kpos_c[None, :] >= 0, np.abs(past + np.arange(ss)[:, None] - kpos_c[None, :]), 0)
    dist_n = np.abs(np.arange(ss)[:, None] - np.arange(ss)[None, :])
    tk_s = lc // 2 if (lc // 2) % SUBLANES == 0 and lc % 2 == 0 else lc
    dist_c = np.tile(dist_c, (2, 1)).astype(np.float32).reshape(2 * ss, lc // tk_s, tk_s)
    dist_c = jnp.asarray(np.moveaxis(dist_c, 1, 0))
    dist_n = jnp.asarray(np.tile(dist_n, (2, 1)).astype(np.float32))
    sod = _sample_diff_attn(q3(sqd), cache_diff_k[0], cache_diff_v[0], q3(skd), q3(svd), dist_c, dist_n, lamv, subg,
                            tk=tk_s, lam_init=lam_init)
    krc = jnp.concatenate([jnp.zeros((bs, N_META, MLA_ROPE), F32), cache_mla_kr[0]], axis=1)
    som = _sample_mla_attn(q3(sqm), cache_mla_ckv[0], krc, q3(sckv), q3(skr), wuk, wuv, tk=tk_s)
    hs, xn2s, rts = _merge(sod.reshape(ts, -1), som.reshape(ts, -1), x_sample.reshape(ts, d), wo, norm2_g, wr, br,
                           tm=ts)
    eids = rts[2:4].astype(jnp.int32)
    gates = (jnp.where(eids[0][:, None] == jnp.arange(LANES)[None, :], rts[0][:, None], 0.0)
             + jnp.where(eids[1][:, None] == jnp.arange(LANES)[None, :], rts[1][:, None], 0.0))
    y_sample = _moe_dense(xn2s, hs, gates, wg, wu, wd, gf).reshape(bs, ss, d)

    return (y_prompt, y_sample,
            p_dk[None], p_dv[None], p_ckv[None], pkr.reshape(1, b, s, MLA_ROPE),
            s_dk.reshape(1, bs, ss, DA_HEADS, 2 * DA_D), s_dv.reshape(1, bs, ss, DA_HEADS, DA_V),
            sckv.reshape(1, bs, ss, c_kvl), skr.reshape(1, bs, ss, MLA_ROPE))
```

```python
import functools
import math

import numpy as np
import jax
import jax.numpy as jnp
from jax import lax
from jax.experimental import pallas as pl
from jax.experimental.pallas import tpu as pltpu

CHUNK = 64
N_META = 16
EPS = 1e-6
DA_HEADS = 8
DA_D = 64
DA_V = 2 * DA_D
MLA_HEADS = 8
MLA_NOPE = 128
MLA_ROPE = 64
MLA_V = 128
ROPE_THETA = 10000.0
MLA_SCALE = (MLA_NOPE + MLA_ROPE) ** -0.5
N_GROUPS = 4
EXPERTS_PER_GROUP = 4
N_EXPERTS = N_GROUPS * EXPERTS_PER_GROUP
LOG2E = math.log2(math.e)
LANES = 128
SUBLANES = 8
BF16_ROWS = 16
MXU_DIM = 256
HEAD_W = 128
MLA_W = 256
VT_W = DA_V + BF16_ROWS
NEG_BIG = -1e30
VMEM_LIMIT = 56 * 1024 * 1024
ATTN_TILE = 512
INPROJ_ROWS = 256
TOKEN_ROWS = 512

BF16 = jnp.bfloat16
F32 = jnp.float32


def _dot(a, b):
    return jnp.dot(a, b, preferred_element_type=F32)


def _dot_nt(a, b):
    return lax.dot_general(a, b, (((1,), (1,)), ((), ())), preferred_element_type=F32)


def _rms(x, g):
    return x * lax.rsqrt(jnp.mean(x * x, axis=-1, keepdims=True) + EPS) * g


def _cparams(n_axes):
    return pltpu.CompilerParams(dimension_semantics=("arbitrary",) * n_axes,
                                vmem_limit_bytes=VMEM_LIMIT)


def _const_spec(shape):
    nd = len(shape)
    return pl.BlockSpec(shape, lambda *_: (0,) * nd, pipeline_mode=pl.Buffered(1))


def _alibi_slope(h):
    return 2.0 ** (-8.0 * (h + 1) / DA_HEADS)


def _inproj_kernel(*refs, c_qk, c_v, c_ql, c_kvl, tab_blocks, prompt):
    (x_ref, tab_ref, g1_ref, win_ref, qg_ref, wuq_ref, kvg_ref, wukv_ref), refs = refs[:8], refs[8:]
    if prompt:
        (mk32_ref, mv32_ref, mckv_ref), refs = refs[:3], refs[3:]
    (qd_ref, kd32_ref, kdb_ref, vd32_ref, vdb_ref, qm_ref, ckv_ref, kr_ref, km_ref, vm_ref), refs = refs[:10], refs[10:]
    x = x_ref[...]
    tm = x.shape[0]
    if prompt:
        kbuf, vbuf, cbuf, sem, msem = refs
        i = pl.program_id(0)
        n = pl.num_programs(0)
        slot = i % 2

        def tile_copies(step, s):
            sb = step // tab_blocks
            rows = pl.ds(N_META + (step % tab_blocks) * tm, tm)
            return [pltpu.make_async_copy(kbuf.at[s], kd32_ref.at[sb, rows], sem.at[s, 0]),
                    pltpu.make_async_copy(vbuf.at[s], vd32_ref.at[sb, rows], sem.at[s, 1]),
                    pltpu.make_async_copy(cbuf.at[s], ckv_ref.at[sb, rows], sem.at[s, 2])]

        @pl.when(i >= 2)
        def _slot_free():
            for cp in tile_copies(i - 2, slot):
                cp.wait()

    xn = _rms(x, g1_ref[...]).astype(BF16)
    tab = tab_ref[...]
    o1 = c_qk
    o2 = o1 + c_qk
    o3 = o2 + c_v
    o4 = o3 + c_ql
    o5 = o4 + c_kvl
    ones = jnp.ones((BF16_ROWS, tm), BF16)

    def put_heads32(ref, buf, z):
        for h in range(DA_HEADS):
            if prompt:
                buf[slot, :, h, :] = z[:, h * HEAD_W:(h + 1) * HEAD_W]
            else:
                ref[:, h, :] = z[:, h * HEAD_W:(h + 1) * HEAD_W]

    def put_values_t(ref, z):
        for h in range(DA_HEADS):
            ref[h * VT_W:h * VT_W + DA_V, :] = z[:, h * DA_V:(h + 1) * DA_V].T.astype(BF16)
            ref[h * VT_W + DA_V:(h + 1) * VT_W, :] = ones

    zq = _dot(xn, win_ref[:, 0:o1]) * (DA_D ** -0.5 * LOG2E)
    qd_ref[...] = zq.T.astype(BF16) if prompt else zq.astype(BF16)
    zk = _dot(xn, win_ref[:, o1:o2])
    put_heads32(kd32_ref, kbuf if prompt else None, zk)
    kdb_ref[...] = zk.astype(BF16)
    zv = _dot(xn, win_ref[:, o2:o3])
    put_heads32(vd32_ref, vbuf if prompt else None, zv)
    if prompt:
        put_values_t(vdb_ref, zv)
    else:
        vdb_ref[...] = zv.astype(BF16)

    cq = _rms(_dot(xn, win_ref[:, o3:o4]), qg_ref[...]).astype(BF16)
    nq = MLA_HEADS * MLA_NOPE
    qn = _dot(cq, wuq_ref[:, 0:nq]) * (MLA_SCALE * LOG2E)
    qr = _dot(cq, wuq_ref[:, nq:2 * nq]) * (MLA_SCALE * LOG2E)
    for h in range(MLA_HEADS):
        u = qr[:, h * LANES:(h + 1) * LANES] * tab
        rot = u + pltpu.roll(u, MLA_ROPE, 1)
        nope = qn[:, h * LANES:(h + 1) * LANES]
        if prompt:
            qm_ref[h * MLA_W:h * MLA_W + LANES, :] = nope.T.astype(BF16)
            qm_ref[h * MLA_W + LANES:(h + 1) * MLA_W, :] = rot.T.astype(BF16)
        else:
            qm_ref[:, h * MLA_W:h * MLA_W + LANES] = nope.astype(BF16)
            qm_ref[:, h * MLA_W + LANES:(h + 1) * MLA_W] = rot.astype(BF16)

    ckv = _rms(_dot(xn, win_ref[:, o4:o5]), kvg_ref[...])
    if prompt:
        cbuf[slot] = ckv
    else:
        ckv_ref[...] = ckv
    ckvb = ckv.astype(BF16)
    nk = MLA_HEADS * MLA_NOPE
    kn = _dot(ckvb, wukv_ref[:, 0:nk])
    vm = _dot(ckvb, wukv_ref[:, nk:nk + MLA_HEADS * MLA_V])
    if prompt:
        put_values_t(vm_ref, vm)
    else:
        vm_ref[...] = vm.astype(BF16)

    u = _dot(xn, win_ref[:, o5:o5 + LANES]) * tab
    rot = u + pltpu.roll(u, MLA_ROPE, 1)
    kr_ref[...] = rot[:, 0:MLA_ROPE]
    lane = lax.broadcasted_iota(jnp.int32, rot.shape, 1)
    krp = jnp.where(lane < MLA_ROPE, rot, 0.0).astype(BF16)
    for h in range(MLA_HEADS):
        km_ref[:, h * MLA_W:h * MLA_W + LANES] = kn[:, h * LANES:(h + 1) * LANES].astype(BF16)
        km_ref[:, h * MLA_W + LANES:(h + 1) * MLA_W] = krp

    if prompt:
        for cp in tile_copies(i, slot):
            cp.start()

        @pl.when(i % tab_blocks == 0)
        def _meta_rows():
            head = pl.ds(0, N_META)
            sb = i // tab_blocks
            cps = [pltpu.make_async_copy(mk32_ref, kd32_ref.at[sb, head], msem.at[0]),
                   pltpu.make_async_copy(mv32_ref, vd32_ref.at[sb, head], msem.at[1]),
                   pltpu.make_async_copy(mckv_ref, ckv_ref.at[sb, head], msem.at[2])]
            for cp in cps:
                cp.start()
            for cp in cps:
                cp.wait()

        @pl.when(jnp.logical_and(i == n - 1, i >= 1))
        def _drain_previous():
            for cp in tile_copies(i - 1, 1 - slot):
                cp.wait()

        @pl.when(i == n - 1)
        def _drain_last():
            for cp in tile_copies(i, slot):
                cp.wait()


def _inproj(x, tab, g1, win, qg, wuq, kvg, wukv, *, tm, tab_blocks, dims, batch=None, meta=None):
    m, d = x.shape
    c_qk, c_v, c_ql, c_kvl = dims
    assert m % tm == 0
    prompt = batch is not None
    row = lambda w: pl.BlockSpec((tm, w), lambda i: (i, 0))
    hw = DA_HEADS * VT_W
    extra_in, extra_specs, scratch = [], [], []
    if prompt:
        seq = tab_blocks * tm
        assert m == batch * seq
        col = lambda w: pl.BlockSpec((None, w, tm), lambda i: (i // tab_blocks, 0, i % tab_blocks))
        hbm = pl.BlockSpec(memory_space=pl.ANY)
        tall = (batch, N_META + seq)
        extra_in = list(meta)
        extra_specs = [_const_spec(a.shape) for a in meta]
        scratch = [pltpu.VMEM((2, tm, DA_HEADS, HEAD_W), F32), pltpu.VMEM((2, tm, DA_HEADS, HEAD_W), F32),
                   pltpu.VMEM((2, tm, c_kvl), F32), pltpu.SemaphoreType.DMA((2, 3)), pltpu.SemaphoreType.DMA((3,))]
        out_specs = [col(c_qk), hbm, row(c_qk), hbm, col(hw), col(MLA_HEADS * MLA_W), hbm,
                     row(MLA_ROPE), row(MLA_HEADS * MLA_W), col(hw)]
        out_shape = [
            jax.ShapeDtypeStruct((batch, c_qk, seq), BF16),
            jax.ShapeDtypeStruct(tall + (DA_HEADS, HEAD_W), F32),
            jax.ShapeDtypeStruct((m, c_qk), BF16),
            jax.ShapeDtypeStruct(tall + (DA_HEADS, HEAD_W), F32),
            jax.ShapeDtypeStruct((batch, hw, seq), BF16),
            jax.ShapeDtypeStruct((batch, MLA_HEADS * MLA_W, seq), BF16),
            jax.ShapeDtypeStruct(tall + (c_kvl,), F32),
            jax.ShapeDtypeStruct((m, MLA_ROPE), F32),
            jax.ShapeDtypeStruct((m, MLA_HEADS * MLA_W), BF16),
            jax.ShapeDtypeStruct((batch, hw, seq), BF16),
        ]
    else:
        heads32 = pl.BlockSpec((tm, DA_HEADS, HEAD_W), lambda i: (i, 0, 0))
        out_specs = [row(c_qk), heads32, row(c_qk), heads32, row(c_v), row(MLA_HEADS * MLA_W), row(c_kvl),
                     row(MLA_ROPE), row(MLA_HEADS * MLA_W), row(MLA_HEADS * MLA_V)]
        out_shape = [
            jax.ShapeDtypeStruct((m, c_qk), BF16),
            jax.ShapeDtypeStruct((m, DA_HEADS, HEAD_W), F32),
            jax.ShapeDtypeStruct((m, c_qk), BF16),
            jax.ShapeDtypeStruct((m, DA_HEADS, HEAD_W), F32),
            jax.ShapeDtypeStruct((m, c_v), BF16),
            jax.ShapeDtypeStruct((m, MLA_HEADS * MLA_W), BF16),
            jax.ShapeDtypeStruct((m, c_kvl), F32),
            jax.ShapeDtypeStruct((m, MLA_ROPE), F32),
            jax.ShapeDtypeStruct((m, MLA_HEADS * MLA_W), BF16),
            jax.ShapeDtypeStruct((m, MLA_HEADS * MLA_V), BF16),
        ]
    return pl.pallas_call(
        functools.partial(_inproj_kernel, c_qk=c_qk, c_v=c_v, c_ql=c_ql, c_kvl=c_kvl, tab_blocks=tab_blocks,
                          prompt=prompt),
        grid=(m // tm,),
        in_specs=[
            row(d),
            pl.BlockSpec((tm, LANES), lambda i: (i % tab_blocks, 0)),
            _const_spec(g1.shape), _const_spec(win.shape), _const_spec(qg.shape),
            _const_spec(wuq.shape), _const_spec(kvg.shape), _const_spec(wukv.shape),
        ] + extra_specs,
        out_specs=out_specs,
        out_shape=out_shape,
        scratch_shapes=scratch,
        compiler_params=_cparams(1),
        name="inproj",
    )(x, tab, g1, win, qg, wuq, kvg, wukv, *extra_in)


def _softmax_seed(s, v, m_ref, l_ref, acc_ref, j):
    m = jnp.max(s, axis=1, keepdims=True)
    p = jnp.exp2(s - m)
    m_ref[j] = m
    l_ref[j] = jnp.sum(p, axis=1, keepdims=True)
    acc_ref[j] = _dot(p.astype(BF16), v)


def _softmax_step(s, v, m_ref, l_ref, acc_ref, j):
    m_old = m_ref[j]
    m_new = jnp.maximum(m_old, jnp.max(s, axis=1, keepdims=True))
    alpha = jnp.exp2(m_old - m_new)
    p = jnp.exp2(s - m_new)
    l_ref[j] = alpha * l_ref[j] + jnp.sum(p, axis=1, keepdims=True)
    acc_ref[j] = alpha * acc_ref[j] + _dot(p.astype(BF16), v)
    m_ref[j] = m_new


def _diff_lambda(lamv, lam_init):
    a = jnp.sum(lamv[0:1] * lamv[1:2], axis=1, keepdims=True)
    b = jnp.sum(lamv[2:3] * lamv[3:4], axis=1, keepdims=True)
    return jnp.exp(a) - jnp.exp(b) + lam_init


def _split_maps(q):
    lane = lax.broadcasted_iota(jnp.int32, q.shape, 1)
    zero = jnp.zeros_like(q)
    return jnp.where(lane < DA_D, q, zero), jnp.where(lane >= DA_D, q, zero)


def _seed_t(st, vt, m_ref, acc_ref, j):
    m = jnp.max(st, axis=0, keepdims=True)
    m_ref[j] = m
    acc_ref[j] = _dot(vt, jnp.exp2(st - m).astype(BF16))


def _step_t(st, shift, vt, m_ref, acc_ref, j):
    m_old = m_ref[j]
    m_new = jnp.maximum(m_old, jnp.max(st, axis=0, keepdims=True) + shift)
    p = jnp.exp2(st - (m_new - shift))
    acc_ref[j] = jnp.exp2(m_old - m_new) * acc_ref[j] + _dot(vt, p.astype(BF16))
    m_ref[j] = m_new


def _diff_attn_kernel(qi_ref, ki_ref, qt_ref, k_ref, vt_ref, mk_ref, mvt_ref, pos_ref, cq_ref, corr_ref,
                      lamv_ref, g_ref, o_ref, m_s, acc_s, *, tq, lam_init):
    t = pl.program_id(1)
    qi = qi_ref[t]
    ki = ki_ref[t]
    qrow = lax.broadcasted_iota(jnp.int32, (HEAD_W, tq), 0) < DA_D
    klane = lax.broadcasted_iota(jnp.int32, (tq, HEAD_W), 1) < DA_D

    @pl.when(ki == 0)
    def _seed():
        for h in range(DA_HEADS):
            hs = slice(h * HEAD_W, (h + 1) * HEAD_W)
            qt = qt_ref[hs, :]
            zero = jnp.zeros_like(qt)
            for c in range(2):
                qc = jnp.where(qrow, qt, zero) if c == 0 else jnp.where(qrow, zero, qt)
                _seed_t(_dot(mk_ref[:, hs], qc), mvt_ref[h * VT_W:(h + 1) * VT_W, :], m_s, acc_s, 2 * h + c)

    def body(diag):
        qpos = (lax.broadcasted_iota(jnp.int32, (1, tq), 1) + (qi - ki) * tq).astype(F32)
        pos = pos_ref[...]
        for h in range(DA_HEADS):
            hs = slice(h * HEAD_W, (h + 1) * HEAD_W)
            c_h = _alibi_slope(h) * LOG2E
            shift = qpos * (-c_h)
            qt = qt_ref[hs, :]
            cq = cq_ref[h]
            kk = k_ref[:, hs]
            vt = vt_ref[h * VT_W:(h + 1) * VT_W, :]
            for c in range(2):
                qc = jnp.where(qrow, qt, cq) if c == 0 else jnp.where(qrow, cq, qt)
                kc = jnp.where(klane, kk, pos) if c == 0 else jnp.where(klane, pos, kk)
                st = _dot(kc, qc)
                if diag:
                    st = st + corr_ref[...] * c_h
                _step_t(st, shift, vt, m_s, acc_s, 2 * h + c)

    @pl.when(ki != qi)
    def _off_diagonal():
        body(False)

    @pl.when(ki == qi)
    def _diagonal():
        body(True)
        lam = _diff_lambda(lamv_ref[...], lam_init)
        g = g_ref[...]
        for h in range(DA_HEADS):
            a0 = acc_s[2 * h]
            a1 = acc_s[2 * h + 1]
            ot = a0[0:DA_V] / a0[DA_V:DA_V + 1] - lam * (a1[0:DA_V] / a1[DA_V:DA_V + 1])
            ot = ot * lax.rsqrt(jnp.mean(ot * ot, axis=0, keepdims=True) + EPS) * g * (1.0 - lam_init)
            o_ref[:, h * HEAD_W:(h + 1) * HEAD_W] = ot.T.astype(BF16)


def _mla_attn_kernel(qi_ref, ki_ref, qt_ref, k_ref, vt_ref, mk_ref, mvt_ref, mask_ref, o_ref,
                     m_s, acc_s):
    t = pl.program_id(1)
    qi = qi_ref[t]
    ki = ki_ref[t]

    @pl.when(ki == 0)
    def _seed():
        for h in range(MLA_HEADS):
            st = _dot(mk_ref[:, h * MLA_W:(h + 1) * MLA_W], qt_ref[h * MLA_W:(h + 1) * MLA_W, :])
            _seed_t(st, mvt_ref[h * VT_W:(h + 1) * VT_W, :], m_s, acc_s, h)

    def body(diag):
        for h in range(MLA_HEADS):
            st = _dot(k_ref[:, h * MLA_W:(h + 1) * MLA_W], qt_ref[h * MLA_W:(h + 1) * MLA_W, :])
            if diag:
                st = st + mask_ref[...]
            _step_t(st, 0.0, vt_ref[h * VT_W:(h + 1) * VT_W, :], m_s, acc_s, h)

    @pl.when(ki != qi)
    def _off_diagonal():
        body(False)

    @pl.when(ki == qi)
    def _diagonal():
        body(True)
        for h in range(MLA_HEADS):
            a = acc_s[h]
            o_ref[:, h * MLA_V:(h + 1) * MLA_V] = (a[0:MLA_V] / a[MLA_V:MLA_V + 1]).T.astype(BF16)


def _pair_tables(nq):
    qi = np.concatenate([np.full((i + 1,), i, np.int32) for i in range(nq)])
    ki = np.concatenate([np.arange(i + 1, dtype=np.int32) for i in range(nq)])
    return jnp.asarray(qi), jnp.asarray(ki)


def _tile_geometry(tq):
    j = np.arange(tq)[:, None]
    i = np.arange(tq)[None, :]
    visible = (j // CHUNK) <= (i // CHUNK)
    return i, j, visible


def _bf16_split3(x):
    parts = []
    for _ in range(3):
        p = float(np.asarray(x, np.float32).astype(BF16).astype(np.float32))
        parts.append(p)
        x = x - p
    return parts


def _alibi_operands(tq):
    assert tq <= 2 * MXU_DIM
    j = np.arange(tq)
    jlo = (j % MXU_DIM).astype(np.float32)
    jhi = (j - j % MXU_DIM).astype(np.float32)
    pos = np.zeros((tq, HEAD_W), np.float32)
    cq = np.zeros((DA_HEADS, HEAD_W, tq), np.float32)
    for base in (0, DA_D):
        for r in range(3):
            pos[:, base + 2 * r] = jlo
            pos[:, base + 2 * r + 1] = jhi
    for h in range(DA_HEADS):
        parts = _bf16_split3(_alibi_slope(h) * LOG2E)
        for base in (0, DA_D):
            for r in range(3):
                cq[h, base + 2 * r, :] = parts[r]
                cq[h, base + 2 * r + 1, :] = parts[r]
    return jnp.asarray(pos, BF16), jnp.asarray(cq, BF16)


def _prompt_attn_specs(tq, wq, wk, wv):
    qt_spec = pl.BlockSpec((None, wq, tq), lambda b, t, qi, ki: (b, 0, qi[t]))
    k_spec = pl.BlockSpec((None, tq, wk), lambda b, t, qi, ki: (b, ki[t], 0))
    vt_spec = pl.BlockSpec((None, wv, tq), lambda b, t, qi, ki: (b, 0, ki[t]))
    mk_spec = pl.BlockSpec((N_META, wk), lambda b, t, qi, ki: (0, 0))
    mvt_spec = pl.BlockSpec((wv, N_META), lambda b, t, qi, ki: (0, 0))
    return qt_spec, k_spec, vt_spec, mk_spec, mvt_spec


def _diff_attn(qt, k, vt, mk, mvt, lamv, g, *, tq, lam_init):
    b, s, w = k.shape
    nq = s // tq
    qi, ki = _pair_tables(nq)
    i, j, visible = _tile_geometry(tq)
    corr = jnp.asarray(np.where(visible, np.where(j > i, -2.0 * (j - i), 0.0), NEG_BIG).astype(np.float32))
    pos, cq = _alibi_operands(tq)
    qt_spec, k_spec, vt_spec, mk_spec, mvt_spec = _prompt_attn_specs(tq, w, w, vt.shape[1])
    full = lambda a: pl.BlockSpec(a.shape, lambda b_, t, qi_, ki_: (0,) * a.ndim)
    return pl.pallas_call(
        functools.partial(_diff_attn_kernel, tq=tq, lam_init=lam_init),
        grid_spec=pltpu.PrefetchScalarGridSpec(
            num_scalar_prefetch=2,
            grid=(b, int(qi.shape[0])),
            in_specs=[qt_spec, k_spec, vt_spec, mk_spec, mvt_spec, full(pos), full(cq), full(corr),
                      full(lamv), full(g)],
            out_specs=pl.BlockSpec((None, tq, w), lambda b_, t, qi_, ki_: (b_, qi_[t], 0)),
            scratch_shapes=[pltpu.VMEM((2 * DA_HEADS, 1, tq), F32), pltpu.VMEM((2 * DA_HEADS, VT_W, tq), F32)],
        ),
        out_shape=jax.ShapeDtypeStruct((b, s, w), BF16),
        compiler_params=_cparams(2),
        name="diff_attn",
    )(qi, ki, qt, k, vt, mk, mvt, pos, cq, corr, lamv, g)


def _mla_attn(qt, k, vt, mk, mvt, *, tq):
    b, s, wq = k.shape
    nq = s // tq
    qi, ki = _pair_tables(nq)
    _, _, visible = _tile_geometry(tq)
    mask = jnp.asarray(np.where(visible, 0.0, NEG_BIG).astype(np.float32))
    qt_spec, k_spec, vt_spec, mk_spec, mvt_spec = _prompt_attn_specs(tq, wq, wq, vt.shape[1])
    wo = MLA_HEADS * MLA_V
    return pl.pallas_call(
        _mla_attn_kernel,
        grid_spec=pltpu.PrefetchScalarGridSpec(
            num_scalar_prefetch=2,
            grid=(b, int(qi.shape[0])),
            in_specs=[qt_spec, k_spec, vt_spec, mk_spec, mvt_spec,
                      pl.BlockSpec(mask.shape, lambda b_, t, qi_, ki_: (0, 0))],
            out_specs=pl.BlockSpec((None, tq, wo), lambda b_, t, qi_, ki_: (b_, qi_[t], 0)),
            scratch_shapes=[pltpu.VMEM((MLA_HEADS, 1, tq), F32), pltpu.VMEM((MLA_HEADS, VT_W, tq), F32)],
        ),
        out_shape=jax.ShapeDtypeStruct((b, s, wo), BF16),
        compiler_params=_cparams(2),
        name="mla_attn",
    )(qi, ki, qt, k, vt, mk, mvt, mask)


def _sample_diff_kernel(q_ref, kc_hbm, vc_hbm, kn_ref, vn_ref, dc_ref, dn_ref, lamv_ref, g_ref,
                        o_ref, kbuf, vbuf, sem, m_s, l_s, acc_s, *, tk, lam_init):
    kt = pl.program_id(1)
    nkt = pl.num_programs(1)
    step = pl.program_id(0) * nkt + kt
    n_steps = pl.num_programs(0) * nkt

    def tile_copies(s, slot):
        sb = s // nkt
        rows = pl.ds((s % nkt) * tk, tk)
        cps = []
        for h in range(DA_HEADS):
            cps.append(pltpu.make_async_copy(kc_hbm.at[sb, rows, h, :], kbuf.at[slot, h], sem.at[slot, 0]))
            cps.append(pltpu.make_async_copy(vc_hbm.at[sb, rows, h, :], vbuf.at[slot, h], sem.at[slot, 1]))
        return cps

    @pl.when(step == 0)
    def _prime():
        for cp in tile_copies(0, 0):
            cp.start()

    @pl.when(step + 1 < n_steps)
    def _prefetch():
        for cp in tile_copies(step + 1, (step + 1) % 2):
            cp.start()

    slot = step % 2
    for cp in tile_copies(step, slot):
        cp.wait()

    def heads(get_k, get_v, dist, first):
        for h in range(DA_HEADS):
            q1, q2 = _split_maps(q_ref[:, h * HEAD_W:(h + 1) * HEAD_W])
            qq = jnp.concatenate([q1, q2], axis=0)
            s = _dot_nt(qq, get_k(h)) + dist * (-_alibi_slope(h) * LOG2E)
            if first:
                _softmax_seed(s, get_v(h), m_s, l_s, acc_s, h)
            else:
                _softmax_step(s, get_v(h), m_s, l_s, acc_s, h)

    cache_k = lambda h: kbuf[slot, h].astype(BF16)
    cache_v = lambda h: vbuf[slot, h].astype(BF16)

    @pl.when(kt == 0)
    def _first():
        heads(cache_k, cache_v, dc_ref[...], True)

    @pl.when(kt > 0)
    def _rest():
        heads(cache_k, cache_v, dc_ref[...], False)

    @pl.when(kt == nkt - 1)
    def _finish():
        heads(lambda h: kn_ref[:, h * HEAD_W:(h + 1) * HEAD_W], lambda h: vn_ref[:, h * HEAD_W:(h + 1) * HEAD_W],
              dn_ref[...], False)
        lam = _diff_lambda(lamv_ref[...], lam_init)
        g = g_ref[...]
        nq = q_ref.shape[0]
        for h in range(DA_HEADS):
            a = acc_s[h] / l_s[h]
            o = a[0:nq] - lam * a[nq:2 * nq]
            o_ref[:, h * HEAD_W:(h + 1) * HEAD_W] = (_rms(o, g) * (1.0 - lam_init)).astype(BF16)


def _sample_diff_attn(q, kc, vc, kn, vn, dist_c, dist_n, lamv, g, *, tk, lam_init):
    bs, nq, w = q.shape
    lc = kc.shape[1]
    assert lc % tk == 0
    full = lambda a: pl.BlockSpec(a.shape, lambda b, t: (0,) * a.ndim)
    per_stream = lambda a: pl.BlockSpec((None,) + a.shape[1:], lambda b, t: (b,) + (0,) * (a.ndim - 1))
    cache = pl.BlockSpec(memory_space=pl.ANY)
    return pl.pallas_call(
        functools.partial(_sample_diff_kernel, tk=tk, lam_init=lam_init),
        grid=(bs, lc // tk),
        in_specs=[per_stream(q), cache, cache, per_stream(kn), per_stream(vn),
                  pl.BlockSpec((None, 2 * nq, tk), lambda b, t: (t, 0, 0)),
                  full(dist_n), full(lamv), full(g)],
        out_specs=per_stream(q),
        out_shape=jax.ShapeDtypeStruct((bs, nq, w), BF16),
        scratch_shapes=[pltpu.VMEM((2, DA_HEADS, tk, HEAD_W), F32), pltpu.VMEM((2, DA_HEADS, tk, HEAD_W), F32),
                        pltpu.SemaphoreType.DMA((2, 2)),
                        pltpu.VMEM((DA_HEADS, 2 * nq, 1), F32), pltpu.VMEM((DA_HEADS, 2 * nq, 1), F32),
                        pltpu.VMEM((DA_HEADS, 2 * nq, DA_V), F32)],
        compiler_params=_cparams(2),
        name="sample_diff_attn",
    )(q, kc, vc, kn, vn, dist_c, dist_n, lamv, g)


def _sample_mla_kernel(q_ref, cc_ref, krc_ref, cn_ref, krn_ref, wuk_ref, wuv_ref, o_ref,
                       ql_s, qr_s, m_s, l_s, acc_s):
    kt = pl.program_id(1)
    nkt = pl.num_programs(1)
    nq = q_ref.shape[0]

    @pl.when(kt == 0)
    def _prep():
        for h in range(MLA_HEADS):
            qn = q_ref[:, h * MLA_W:h * MLA_W + MLA_NOPE]
            ql_s[h * nq:(h + 1) * nq, :] = _dot_nt(qn, wuk_ref[:, h * MLA_NOPE:(h + 1) * MLA_NOPE]).astype(BF16)
            qr_s[h * nq:(h + 1) * nq, :] = q_ref[:, h * MLA_W + MLA_NOPE:(h + 1) * MLA_W]

    def scores(c_ref, kr_ref):
        cb = c_ref[...].astype(BF16)
        krb = kr_ref[...].astype(BF16)
        s = _dot_nt(ql_s[...], cb) + _dot_nt(qr_s[:, 0:MLA_ROPE], krb)
        return s, cb

    @pl.when(kt == 0)
    def _first():
        s, cb = scores(cc_ref, krc_ref)
        _softmax_seed(s, cb, m_s, l_s, acc_s, 0)

    @pl.when(kt > 0)
    def _rest():
        s, cb = scores(cc_ref, krc_ref)
        _softmax_step(s, cb, m_s, l_s, acc_s, 0)

    @pl.when(kt == nkt - 1)
    def _finish():
        s, cb = scores(cn_ref, krn_ref)
        _softmax_step(s, cb, m_s, l_s, acc_s, 0)
        ol = (acc_s[0] / l_s[0]).astype(BF16)
        for h in range(MLA_HEADS):
            o_ref[:, h * MLA_V:(h + 1) * MLA_V] = _dot(
                ol[h * nq:(h + 1) * nq, :], wuv_ref[:, h * MLA_V:(h + 1) * MLA_V]).astype(BF16)


def _sample_mla_attn(q, cc, krc, cn, krn, wuk, wuv, *, tk):
    bs, nq, wq = q.shape
    lc, kvl = cc.shape[1], cc.shape[2]
    assert lc % tk == 0
    full = lambda a: pl.BlockSpec(a.shape, lambda b, t: (0,) * a.ndim)
    per_stream = lambda a: pl.BlockSpec((None,) + a.shape[1:], lambda b, t: (b,) + (0,) * (a.ndim - 1))
    rows = MLA_HEADS * nq
    return pl.pallas_call(
        _sample_mla_kernel,
        grid=(bs, lc // tk),
        in_specs=[per_stream(q),
                  pl.BlockSpec((None, tk, kvl), lambda b, t: (b, t, 0)),
                  pl.BlockSpec((None, tk, MLA_ROPE), lambda b, t: (b, t, 0)),
                  per_stream(cn), per_stream(krn), full(wuk), full(wuv)],
        out_specs=pl.BlockSpec((None, nq, MLA_HEADS * MLA_V), lambda b, t: (b, 0, 0)),
        out_shape=jax.ShapeDtypeStruct((bs, nq, MLA_HEADS * MLA_V), BF16),
        scratch_shapes=[pltpu.VMEM((rows, kvl), BF16), pltpu.VMEM((rows, LANES), BF16),
                        pltpu.VMEM((1, rows, 1), F32), pltpu.VMEM((1, rows, 1), F32),
                        pltpu.VMEM((1, rows, kvl), F32)],
        compiler_params=_cparams(2),
        name="sample_mla_attn",
    )(q, cc, krc, cn, krn, wuk, wuv)


ROUTER_ROWS = SUBLANES * (1 + N_GROUPS)


def _route(lt):
    g = [lt[i:i + 1] for i in range(N_GROUPS)]
    gmax = functools.reduce(jnp.maximum, g)
    gidx = jnp.full_like(gmax, float(N_GROUPS - 1))
    for i in range(N_GROUPS - 2, -1, -1):
        gidx = jnp.where(g[i] == gmax, float(i), gidx)
    den = functools.reduce(lambda a, b: a + b, [jnp.exp(gi - gmax) for gi in g])
    p_top = 1.0 / den
    e = []
    for j in range(EXPERTS_PER_GROUP):
        ej = lt[SUBLANES * N_GROUPS + j:SUBLANES * N_GROUPS + j + 1]
        for grp in range(N_GROUPS - 2, -1, -1):
            ej = jnp.where(gidx == float(grp), lt[SUBLANES * (grp + 1) + j:SUBLANES * (grp + 1) + j + 1], ej)
        e.append(ej)

    def first_argmax(vals):
        vmax = functools.reduce(jnp.maximum, vals)
        idx = jnp.full_like(vmax, float(len(vals) - 1))
        for i in range(len(vals) - 2, -1, -1):
            idx = jnp.where(vals[i] == vmax, float(i), idx)
        return vmax, idx

    v1, i1 = first_argmax(e)
    rest = [jnp.where(i1 == float(j), -jnp.inf, e[j]) for j in range(EXPERTS_PER_GROUP)]
    v2, i2 = first_argmax(rest)
    r = jnp.exp(v2 - v1)
    w1 = p_top / (1.0 + r)
    w2 = p_top * r / (1.0 + r)
    base = gidx * float(EXPERTS_PER_GROUP)
    return w1, w2, base + i1, base + i2


def _merge_kernel(od_ref, om_ref, x_ref, wo_ref, g2_ref, wr_ref, br_ref, hp_ref, xn_ref, rt_ref):
    nd = od_ref.shape[1]
    y = _dot(od_ref[...], wo_ref[0:nd, :]) + _dot(om_ref[...], wo_ref[nd:, :])
    hp = x_ref[...] + y
    hp_ref[...] = hp
    xn = _rms(hp, g2_ref[...]).astype(BF16)
    xn_ref[...] = xn
    lt = _dot_nt(wr_ref[...], xn) + br_ref[...]
    rows = _route(lt)
    for i, r in enumerate(rows):
        rt_ref[i:i + 1, :] = r
    rt_ref[4:8, :] = jnp.zeros((4, rt_ref.shape[1]), F32)


def _merge(od, om, x, wo, g2, wr, br, *, tm):
    m, d = x.shape
    row = lambda w: pl.BlockSpec((tm, w), lambda i: (i, 0))
    return pl.pallas_call(
        _merge_kernel,
        grid=(m // tm,),
        in_specs=[row(od.shape[1]), row(om.shape[1]), row(d), _const_spec(wo.shape),
                  _const_spec(g2.shape), _const_spec(wr.shape), _const_spec(br.shape)],
        out_specs=[row(d), row(d), pl.BlockSpec((SUBLANES, tm), lambda i: (0, i))],
        out_shape=[jax.ShapeDtypeStruct((m, d), F32), jax.ShapeDtypeStruct((m, d), BF16),
                   jax.ShapeDtypeStruct((SUBLANES, m), F32)],
        compiler_params=_cparams(1),
        name="merge",
    )(od, om, x, wo, g2, wr, br)


def _swiglu(x, wg, wu, wd):
    g = _dot(x, wg)
    u = _dot(x, wu)
    h = (g * jax.nn.sigmoid(g) * u).astype(BF16)
    return _dot(h, wd)


def _moe_sorted_kernel(te_ref, nu_ref, x_ref, w_ref, wg_ref, wu_ref, wd_ref, y_ref):
    i = pl.program_id(0)

    @pl.when(i < nu_ref[0])
    def _():
        y_ref[...] = w_ref[...] * _swiglu(x_ref[...], wg_ref[...], wu_ref[...], wd_ref[...])

    @pl.when(i >= nu_ref[0])
    def _():
        y_ref[...] = jnp.zeros_like(y_ref)


def _moe_sorted(tile_expert, n_used, xs, ws, wg, wu, wd, *, tm):
    n, d = xs.shape
    f = wg.shape[2]
    nt = n // tm
    return pl.pallas_call(
        _moe_sorted_kernel,
        grid_spec=pltpu.PrefetchScalarGridSpec(
            num_scalar_prefetch=2,
            grid=(nt,),
            in_specs=[pl.BlockSpec((tm, d), lambda i, te, nu: (i, 0)),
                      pl.BlockSpec((tm, 1), lambda i, te, nu: (i, 0)),
                      pl.BlockSpec((None, d, f), lambda i, te, nu: (te[i], 0, 0)),
                      pl.BlockSpec((None, d, f), lambda i, te, nu: (te[i], 0, 0)),
                      pl.BlockSpec((None, f, d), lambda i, te, nu: (te[i], 0, 0))],
            out_specs=pl.BlockSpec((tm, d), lambda i, te, nu: (i, 0)),
        ),
        out_shape=jax.ShapeDtypeStruct((n, d), F32),
        compiler_params=_cparams(1),
        name="moe_sorted",
    )(tile_expert, n_used, xs, ws, wg, wu, wd)


def _moe_dense_kernel(x_ref, hp_ref, gates_ref, wg_ref, wu_ref, wd_ref, gf_ref, o_ref, acc_s):
    e = pl.program_id(0)

    @pl.when(e == 0)
    def _():
        acc_s[...] = jnp.zeros_like(acc_s)

    lane = lax.broadcasted_iota(jnp.int32, gates_ref.shape, 1)
    gate = jnp.sum(jnp.where(lane == e, gates_ref[...], 0.0), axis=1, keepdims=True)
    acc_s[...] += gate * _swiglu(x_ref[...], wg_ref[...], wu_ref[...], wd_ref[...])

    @pl.when(e == pl.num_programs(0) - 1)
    def _():
        o_ref[...] = _rms(hp_ref[...] + acc_s[...], gf_ref[...])


def _moe_dense(xn, hp, gates, wg, wu, wd, gf):
    m, d = xn.shape
    ne, _, f = wg.shape
    full = lambda a: pl.BlockSpec(a.shape, lambda e: (0,) * a.ndim)
    return pl.pallas_call(
        _moe_dense_kernel,
        grid=(ne,),
        in_specs=[full(xn), full(hp), full(gates),
                  pl.BlockSpec((None, d, f), lambda e: (e, 0, 0)),
                  pl.BlockSpec((None, d, f), lambda e: (e, 0, 0)),
                  pl.BlockSpec((None, f, d), lambda e: (e, 0, 0)),
                  full(gf)],
        out_specs=full(hp),
        out_shape=jax.ShapeDtypeStruct((m, d), F32),
        scratch_shapes=[pltpu.VMEM((m, d), F32)],
        compiler_params=_cparams(1),
        name="moe_dense",
    )(xn, hp, gates, wg, wu, wd, gf)


def _combine_kernel(hp_ref, y1_ref, y2_ref, gf_ref, o_ref):
    o_ref[...] = _rms(hp_ref[...] + (y1_ref[...] + y2_ref[...]), gf_ref[...])


def _combine(hp, y1, y2, gf, *, tm):
    m, d = hp.shape
    row = pl.BlockSpec((tm, d), lambda i: (i, 0))
    return pl.pallas_call(
        _combine_kernel,
        grid=(m // tm,),
        in_specs=[row, row, row, _const_spec(gf.shape)],
        out_specs=row,
        out_shape=jax.ShapeDtypeStruct((m, d), F32),
        compiler_params=_cparams(1),
        name="combine",
    )(hp, y1, y2, gf)


def _rope_table(pos):
    half = MLA_ROPE // 2
    inv_freq = ROPE_THETA ** (-jnp.arange(half, dtype=F32) / half)
    ang = pos.astype(F32)[:, None] * inv_freq[None, :]
    c, s = jnp.cos(ang), jnp.sin(ang)
    return jnp.concatenate([c, c, -s, s], axis=1)


def _swap_halves(w):
    half = MLA_ROPE // 2
    return jnp.concatenate([w[..., half:], w[..., :half]], axis=-1)


def _values_t(v):
    n = v.shape[0]
    vt = v.reshape(n, DA_HEADS, DA_V).transpose(1, 2, 0)
    return jnp.concatenate([vt, jnp.ones((DA_HEADS, BF16_ROWS, n), v.dtype)], axis=1).reshape(DA_HEADS * VT_W, n)


def _sort_by_expert(eid, w, tm):
    t = eid.shape[1]
    flat_e = eid.reshape(-1)
    onehot = (flat_e[:, None] == jnp.arange(N_EXPERTS, dtype=jnp.int32)[None, :]).astype(jnp.int32)
    rank = jnp.sum((jnp.cumsum(onehot, axis=0) - onehot) * onehot, axis=1)
    counts = jnp.sum(onehot, axis=0)
    tiles_per = (counts + tm - 1) // tm
    tiles_end = jnp.cumsum(tiles_per)
    row_start = (tiles_end - tiles_per) * tm
    pos = row_start[flat_e] + rank
    n_tiles = (2 * t) // tm + N_EXPERTS
    slot_a = jnp.full((n_tiles * tm,), -1, jnp.int32).at[pos].set(jnp.arange(2 * t, dtype=jnp.int32),
                                                                   unique_indices=True, mode="promise_in_bounds")
    used = slot_a >= 0
    safe_a = jnp.maximum(slot_a, 0)
    sorted_tok = jnp.where(used, safe_a % t, 0)
    sorted_w = jnp.where(used, w.reshape(-1).at[safe_a].get(mode="promise_in_bounds"), 0.0)
    tile_ids = jnp.arange(n_tiles, dtype=jnp.int32)
    tile_expert = jnp.minimum(jnp.sum((tiles_end[None, :] <= tile_ids[:, None]).astype(jnp.int32), axis=1),
                              N_EXPERTS - 1)
    n_used = tiles_end[-1:].astype(jnp.int32)
    return pos.reshape(2, t), sorted_tok, sorted_w, tile_expert, n_used


def kernel(x_prompt, x_sample, cache_diff_k, cache_diff_v, cache_mla_ckv, cache_mla_kr, meta_tokens, norm1_g, w_in, diff_lam_q1, diff_lam_k1, diff_lam_q2, diff_lam_k2, diff_subln_g, mla_q_norm_g, mla_w_uq, mla_kv_norm_g, mla_w_uk, mla_w_uv, w_o, norm2_g, router_group_w, router_group_b, router_expert_w, router_expert_b, expert_w_gate, expert_w_up, expert_w_down, final_norm_g):
    depth = norm1_g.shape[0]
    assert depth == 1, "single-layer step only"
    assert MLA_HEADS == DA_HEADS and MLA_V == DA_V
    lam_init = 0.8 - 0.6 * math.exp(-0.3 * 0)
    b, s, d = x_prompt.shape
    bs, ss, _ = x_sample.shape
    past = cache_mla_kr.shape[2]
    lc = N_META + past
    c_qk = DA_HEADS * 2 * DA_D
    c_v = DA_HEADS * DA_V
    c_ql = mla_q_norm_g.shape[1]
    c_kvl = mla_kv_norm_g.shape[1]
    o5 = 2 * c_qk + c_v + c_ql + c_kvl

    win = w_in[0]
    win_ext = jnp.concatenate([win, _swap_halves(win[:, o5:])], axis=1).astype(BF16)
    wuq = mla_w_uq[0].reshape(c_ql, MLA_HEADS, MLA_NOPE + MLA_ROPE)
    wuq_n = wuq[:, :, :MLA_NOPE].reshape(c_ql, MLA_HEADS * MLA_NOPE)
    wuq_r = jnp.concatenate([wuq[:, :, MLA_NOPE:], _swap_halves(wuq[:, :, MLA_NOPE:])], axis=2)
    wuq_ext = jnp.concatenate([wuq_n, wuq_r.reshape(c_ql, MLA_HEADS * LANES)], axis=1).astype(BF16)
    wuk = mla_w_uk[0].astype(BF16)
    wuv = mla_w_uv[0].astype(BF16)
    wukv = jnp.concatenate([wuk, wuv], axis=1)
    wo = w_o[0].astype(BF16)
    wr = jnp.zeros((ROUTER_ROWS, d), F32).at[0:N_GROUPS].set(router_group_w[0].T)
    br = jnp.zeros((ROUTER_ROWS, 1), F32).at[0:N_GROUPS, 0].set(router_group_b[0])
    rew = router_expert_w[0].T.reshape(N_GROUPS, EXPERTS_PER_GROUP, d)
    reb = router_expert_b[0].reshape(N_GROUPS, EXPERTS_PER_GROUP)
    for grp in range(N_GROUPS):
        wr = wr.at[SUBLANES * (grp + 1):SUBLANES * (grp + 1) + EXPERTS_PER_GROUP].set(rew[grp])
        br = br.at[SUBLANES * (grp + 1):SUBLANES * (grp + 1) + EXPERTS_PER_GROUP, 0].set(reb[grp])
    wr = wr.astype(BF16)
    wg = expert_w_gate[0].astype(BF16)
    wu = expert_w_up[0].astype(BF16)
    wd = expert_w_down[0].astype(BF16)
    gf = final_norm_g[None, :]
    lamv = jnp.stack([diff_lam_q1[0], diff_lam_k1[0], diff_lam_q2[0], diff_lam_k2[0]])
    subg = diff_subln_g

    dims = (c_qk, c_v, c_ql, c_kvl)
    inproj = functools.partial(_inproj, g1=norm1_g, win=win_ext, qg=mla_q_norm_g, wuq=wuq_ext,
                               kvg=mla_kv_norm_g, wukv=wukv, dims=dims)

    (_, mdk32, mdk, mdv32, mdv, _, mckv, _, mkm, mvm) = inproj(
        meta_tokens, jnp.zeros((N_META, LANES), F32), tm=N_META, tab_blocks=1)

    ts = bs * ss
    s_pos = past + jnp.arange(ss, dtype=jnp.int32)
    (sqd, s_dk, skd, s_dv, svd, sqm, sckv, skr, _, _) = inproj(
        x_sample.reshape(ts, d), _rope_table(s_pos), tm=ss, tab_blocks=1)
    q3 = lambda a: a.reshape(bs, ss, a.shape[-1])
    kpos_c = np.arange(lc) - N_META
    dist_c = np.where(kpos_c[None, :] >= 0, np.abs(past + np.arange(ss)[:, None] - kpos_c[None, :]), 0)
    dist_n = np.abs(np.arange(ss)[:, None] - np.arange(ss)[None, :])
    tk_s = lc // 2 if (lc // 2) % SUBLANES == 0 and lc % 2 == 0 else lc
    dist_c = np.tile(dist_c, (2, 1)).astype(np.float32).reshape(2 * ss, lc // tk_s, tk_s)
    dist_c = jnp.asarray(np.moveaxis(dist_c, 1, 0))
    dist_n = jnp.asarray(np.tile(dist_n, (2, 1)).astype(np.float32))
    sod = _sample_diff_attn(q3(sqd), cache_diff_k[0], cache_diff_v[0], q3(skd), q3(svd), dist_c, dist_n, lamv, subg,
                            tk=tk_s, lam_init=lam_init)
    krc = jnp.concatenate([jnp.zeros((bs, N_META, MLA_ROPE), F32), cache_mla_kr[0]], axis=1)
    som = _sample_mla_attn(q3(sqm), cache_mla_ckv[0], krc, q3(sckv), q3(skr), wuk, wuv, tk=tk_s)
    hs, xn2s, rts = _merge(sod.reshape(ts, -1), som.reshape(ts, -1), x_sample.reshape(ts, d), wo, norm2_g, wr, br,
                           tm=ts)
    eids = rts[2:4].astype(jnp.int32)
    gates = (jnp.where(eids[0][:, None] == jnp.arange(LANES)[None, :], rts[0][:, None], 0.0)
             + jnp.where(eids[1][:, None] == jnp.arange(LANES)[None, :], rts[1][:, None], 0.0))
    y_sample = _moe_dense(xn2s, hs, gates, wg, wu, wd, gf).reshape(bs, ss, d)

    tm_p = min(INPROJ_ROWS, s)
    tab_p = _rope_table(jnp.arange(s, dtype=jnp.int32))
    (pqdt, p_dk, pkd, p_dv, pvdt, pqmt, p_ckv, pkr, pkm, pvmt) = inproj(
        x_prompt.reshape(b * s, d), tab_p, tm=tm_p, tab_blocks=s // tm_p, batch=b, meta=(mdk32, mdv32, mckv))
    tq = min(ATTN_TILE, s)
    r3 = lambda a: a.reshape(b, s, a.shape[-1])
    od = _diff_attn(pqdt, r3(pkd), pvdt, mdk, _values_t(mdv), lamv, subg.T, tq=tq, lam_init=lam_init)
    om = _mla_attn(pqmt, r3(pkm), pvmt, mkm, _values_t(mvm), tq=tq)
    t = b * s
    tm_t = min(TOKEN_ROWS, t)
    hp, xn2, rt = _merge(od.reshape(t, -1), om.reshape(t, -1), x_prompt.reshape(t, d), wo, norm2_g, wr, br, tm=tm_t)
    pos, sorted_tok, sorted_w, tile_expert, n_used = _sort_by_expert(rt[2:4].astype(jnp.int32), rt[0:2], tm_t)
    rows = lambda a, idx: a.at[idx].get(mode="promise_in_bounds")
    ys = _moe_sorted(tile_expert, n_used, rows(xn2, sorted_tok), sorted_w[:, None], wg, wu, wd, tm=tm_t)
    y_prompt = _combine(hp, rows(ys, pos[0]), rows(ys, pos[1]), gf, tm=tm_t).reshape(b, s, d)

    return (y_prompt, y_sample,
            p_dk[None], p_dv[None], p_ckv[None], pkr.reshape(1, b, s, MLA_ROPE),
            s_dk.reshape(1, bs, ss, DA_HEADS, 2 * DA_D), s_dv.reshape(1, bs, ss, DA_HEADS, DA_V),
            sckv.reshape(1, bs, ss, c_kvl), skr.reshape(1, bs, ss, MLA_ROPE))
```

```python
import functools
import math

import numpy as np
import jax
import jax.numpy as jnp
from jax import lax
from jax.experimental import pallas as pl
from jax.experimental.pallas import tpu as pltpu

CHUNK = 64
N_META = 16
EPS = 1e-6
DA_HEADS = 8
DA_D = 64
DA_V = 2 * DA_D
MLA_HEADS = 8
MLA_NOPE = 128
MLA_ROPE = 64
MLA_V = 128
ROPE_THETA = 10000.0
MLA_SCALE = (MLA_NOPE + MLA_ROPE) ** -0.5
N_GROUPS = 4
EXPERTS_PER_GROUP = 4
N_EXPERTS = N_GROUPS * EXPERTS_PER_GROUP
LOG2E = math.log2(math.e)
LANES = 128
SUBLANES = 8
BF16_ROWS = 16
MXU_DIM = 256
HEAD_W = 128
MLA_W = 256
VT_W = DA_V + BF16_ROWS
NEG_BIG = -1e30
VMEM_LIMIT = 56 * 1024 * 1024
ATTN_TILE = 512
SCORE_LOOKAHEAD = 2
INPROJ_ROWS = 256
TOKEN_ROWS = 512

BF16 = jnp.bfloat16
F32 = jnp.float32


def _dot(a, b):
    return jnp.dot(a, b, preferred_element_type=F32)


def _dot_nt(a, b):
    return lax.dot_general(a, b, (((1,), (1,)), ((), ())), preferred_element_type=F32)


def _rms(x, g):
    return x * lax.rsqrt(jnp.mean(x * x, axis=-1, keepdims=True) + EPS) * g


def _pack_halves(x):
    w = x.shape[1] // 2
    bits = lax.bitcast_convert_type(x.astype(BF16).astype(F32), jnp.uint32)
    return (bits[:, :w] >> 16) | (bits[:, w:] & jnp.uint32(0xFFFF0000))


def _unpack_halves(u):
    lo = lax.bitcast_convert_type(u << 16, F32)
    hi = lax.bitcast_convert_type(u & jnp.uint32(0xFFFF0000), F32)
    return lo, hi


def _cparams(n_axes):
    return pltpu.CompilerParams(dimension_semantics=("arbitrary",) * n_axes,
                                vmem_limit_bytes=VMEM_LIMIT)


def _const_spec(shape):
    nd = len(shape)
    return pl.BlockSpec(shape, lambda *_: (0,) * nd, pipeline_mode=pl.Buffered(1))


def _alibi_slope(h):
    return 2.0 ** (-8.0 * (h + 1) / DA_HEADS)


def _inproj_kernel(*refs, c_qk, c_v, c_ql, c_kvl, tab_blocks, prompt):
    (x_ref, tab_ref, g1_ref, win_ref, qg_ref, wuq_ref, kvg_ref, wukv_ref), refs = refs[:8], refs[8:]
    if prompt:
        (mk32_ref, mv32_ref, mckv_ref), refs = refs[:3], refs[3:]
    (qd_ref, kd32_ref, kdb_ref, vd32_ref, vdb_ref, qm_ref, ckv_ref, kr_ref, km_ref, vm_ref), refs = refs[:10], refs[10:]
    x = x_ref[...]
    tm = x.shape[0]
    if prompt:
        kbuf, vbuf, cbuf, sem, msem = refs
        i = pl.program_id(0)
        n = pl.num_programs(0)
        slot = i % 2

        def tile_copies(step, s):
            sb = step // tab_blocks
            rows = pl.ds(N_META + (step % tab_blocks) * tm, tm)
            return [pltpu.make_async_copy(kbuf.at[s], kd32_ref.at[sb, rows], sem.at[s, 0]),
                    pltpu.make_async_copy(vbuf.at[s], vd32_ref.at[sb, rows], sem.at[s, 1]),
                    pltpu.make_async_copy(cbuf.at[s], ckv_ref.at[sb, rows], sem.at[s, 2])]

        @pl.when(i >= 2)
        def _slot_free():
            for cp in tile_copies(i - 2, slot):
                cp.wait()

    xn = _rms(x, g1_ref[...]).astype(BF16)
    tab = tab_ref[...]
    o1 = c_qk
    o2 = o1 + c_qk
    o3 = o2 + c_v
    o4 = o3 + c_ql
    o5 = o4 + c_kvl
    ones = jnp.ones((BF16_ROWS, tm), BF16)

    def put_heads32(ref, buf, z):
        for h in range(DA_HEADS):
            if prompt:
                buf[slot, :, h, :] = z[:, h * HEAD_W:(h + 1) * HEAD_W]
            else:
                ref[:, h, :] = z[:, h * HEAD_W:(h + 1) * HEAD_W]

    def put_values_t(ref, z):
        for h in range(DA_HEADS):
            ref[h * VT_W:h * VT_W + DA_V, :] = z[:, h * DA_V:(h + 1) * DA_V].T.astype(BF16)
            ref[h * VT_W + DA_V:(h + 1) * VT_W, :] = ones

    zq = _dot(xn, win_ref[:, 0:o1]) * (DA_D ** -0.5 * LOG2E)
    qd_ref[...] = zq.T.astype(BF16) if prompt else zq.astype(BF16)
    zk = _dot(xn, win_ref[:, o1:o2])
    put_heads32(kd32_ref, kbuf if prompt else None, zk)
    kdb_ref[...] = zk.astype(BF16)
    zv = _dot(xn, win_ref[:, o2:o3])
    put_heads32(vd32_ref, vbuf if prompt else None, zv)
    if prompt:
        put_values_t(vdb_ref, zv)
    else:
        vdb_ref[...] = zv.astype(BF16)

    cq = _rms(_dot(xn, win_ref[:, o3:o4]), qg_ref[...]).astype(BF16)
    nq = MLA_HEADS * MLA_NOPE
    qn = _dot(cq, wuq_ref[:, 0:nq]) * (MLA_SCALE * LOG2E)
    qr = _dot(cq, wuq_ref[:, nq:2 * nq]) * (MLA_SCALE * LOG2E)
    for h in range(MLA_HEADS):
        u = qr[:, h * LANES:(h + 1) * LANES] * tab
        rot = u + pltpu.roll(u, MLA_ROPE, 1)
        nope = qn[:, h * LANES:(h + 1) * LANES]
        if prompt:
            qm_ref[h * MLA_W:h * MLA_W + LANES, :] = nope.T.astype(BF16)
            qm_ref[h * MLA_W + LANES:(h + 1) * MLA_W, :] = rot.T.astype(BF16)
        else:
            qm_ref[:, h * MLA_W:h * MLA_W + LANES] = nope.astype(BF16)
            qm_ref[:, h * MLA_W + LANES:(h + 1) * MLA_W] = rot.astype(BF16)

    ckv = _rms(_dot(xn, win_ref[:, o4:o5]), kvg_ref[...])
    if prompt:
        cbuf[slot] = ckv
    else:
        ckv_ref[...] = ckv
    ckvb = ckv.astype(BF16)
    nk = MLA_HEADS * MLA_NOPE
    kn = _dot(ckvb, wukv_ref[:, 0:nk])
    vm = _dot(ckvb, wukv_ref[:, nk:nk + MLA_HEADS * MLA_V])
    if prompt:
        put_values_t(vm_ref, vm)
    else:
        vm_ref[...] = vm.astype(BF16)

    u = _dot(xn, win_ref[:, o5:o5 + LANES]) * tab
    rot = u + pltpu.roll(u, MLA_ROPE, 1)
    kr_ref[...] = rot[:, 0:MLA_ROPE]
    lane = lax.broadcasted_iota(jnp.int32, rot.shape, 1)
    krp = jnp.where(lane < MLA_ROPE, rot, 0.0).astype(BF16)
    for h in range(MLA_HEADS):
        km_ref[:, h * MLA_W:h * MLA_W + LANES] = kn[:, h * LANES:(h + 1) * LANES].astype(BF16)
        km_ref[:, h * MLA_W + LANES:(h + 1) * MLA_W] = krp

    if prompt:
        for cp in tile_copies(i, slot):
            cp.start()

        @pl.when(i % tab_blocks == 0)
        def _meta_rows():
            head = pl.ds(0, N_META)
            sb = i // tab_blocks
            cps = [pltpu.make_async_copy(mk32_ref, kd32_ref.at[sb, head], msem.at[0]),
                   pltpu.make_async_copy(mv32_ref, vd32_ref.at[sb, head], msem.at[1]),
                   pltpu.make_async_copy(mckv_ref, ckv_ref.at[sb, head], msem.at[2])]
            for cp in cps:
                cp.start()
            for cp in cps:
                cp.wait()

        @pl.when(jnp.logical_and(i == n - 1, i >= 1))
        def _drain_previous():
            for cp in tile_copies(i - 1, 1 - slot):
                cp.wait()

        @pl.when(i == n - 1)
        def _drain_last():
            for cp in tile_copies(i, slot):
                cp.wait()


def _inproj(x, tab, g1, win, qg, wuq, kvg, wukv, *, tm, tab_blocks, dims, batch=None, meta=None):
    m, d = x.shape
    c_qk, c_v, c_ql, c_kvl = dims
    assert m % tm == 0
    prompt = batch is not None
    row = lambda w: pl.BlockSpec((tm, w), lambda i: (i, 0))
    hw = DA_HEADS * VT_W
    extra_in, extra_specs, scratch = [], [], []
    if prompt:
        seq = tab_blocks * tm
        assert m == batch * seq
        col = lambda w: pl.BlockSpec((None, w, tm), lambda i: (i // tab_blocks, 0, i % tab_blocks))
        hbm = pl.BlockSpec(memory_space=pl.ANY)
        tall = (batch, N_META + seq)
        extra_in = list(meta)
        extra_specs = [_const_spec(a.shape) for a in meta]
        scratch = [pltpu.VMEM((2, tm, DA_HEADS, HEAD_W), F32), pltpu.VMEM((2, tm, DA_HEADS, HEAD_W), F32),
                   pltpu.VMEM((2, tm, c_kvl), F32), pltpu.SemaphoreType.DMA((2, 3)), pltpu.SemaphoreType.DMA((3,))]
        out_specs = [col(c_qk), hbm, row(c_qk), hbm, col(hw), col(MLA_HEADS * MLA_W), hbm,
                     row(MLA_ROPE), row(MLA_HEADS * MLA_W), col(hw)]
        out_shape = [
            jax.ShapeDtypeStruct((batch, c_qk, seq), BF16),
            jax.ShapeDtypeStruct(tall + (DA_HEADS, HEAD_W), F32),
            jax.ShapeDtypeStruct((m, c_qk), BF16),
            jax.ShapeDtypeStruct(tall + (DA_HEADS, HEAD_W), F32),
            jax.ShapeDtypeStruct((batch, hw, seq), BF16),
            jax.ShapeDtypeStruct((batch, MLA_HEADS * MLA_W, seq), BF16),
            jax.ShapeDtypeStruct(tall + (c_kvl,), F32),
            jax.ShapeDtypeStruct((m, MLA_ROPE), F32),
            jax.ShapeDtypeStruct((m, MLA_HEADS * MLA_W), BF16),
            jax.ShapeDtypeStruct((batch, hw, seq), BF16),
        ]
    else:
        heads32 = pl.BlockSpec((tm, DA_HEADS, HEAD_W), lambda i: (i, 0, 0))
        out_specs = [row(c_qk), heads32, row(c_qk), heads32, row(c_v), row(MLA_HEADS * MLA_W), row(c_kvl),
                     row(MLA_ROPE), row(MLA_HEADS * MLA_W), row(MLA_HEADS * MLA_V)]
        out_shape = [
            jax.ShapeDtypeStruct((m, c_qk), BF16),
            jax.ShapeDtypeStruct((m, DA_HEADS, HEAD_W), F32),
            jax.ShapeDtypeStruct((m, c_qk), BF16),
            jax.ShapeDtypeStruct((m, DA_HEADS, HEAD_W), F32),
            jax.ShapeDtypeStruct((m, c_v), BF16),
            jax.ShapeDtypeStruct((m, MLA_HEADS * MLA_W), BF16),
            jax.ShapeDtypeStruct((m, c_kvl), F32),
            jax.ShapeDtypeStruct((m, MLA_ROPE), F32),
            jax.ShapeDtypeStruct((m, MLA_HEADS * MLA_W), BF16),
            jax.ShapeDtypeStruct((m, MLA_HEADS * MLA_V), BF16),
        ]
    return pl.pallas_call(
        functools.partial(_inproj_kernel, c_qk=c_qk, c_v=c_v, c_ql=c_ql, c_kvl=c_kvl, tab_blocks=tab_blocks,
                          prompt=prompt),
        grid=(m // tm,),
        in_specs=[
            row(d),
            pl.BlockSpec((tm, LANES), lambda i: (i % tab_blocks, 0)),
            _const_spec(g1.shape), _const_spec(win.shape), _const_spec(qg.shape),
            _const_spec(wuq.shape), _const_spec(kvg.shape), _const_spec(wukv.shape),
        ] + extra_specs,
        out_specs=out_specs,
        out_shape=out_shape,
        scratch_shapes=scratch,
        compiler_params=_cparams(1),
        name="inproj",
    )(x, tab, g1, win, qg, wuq, kvg, wukv, *extra_in)


def _softmax_seed(s, v, m_ref, l_ref, acc_ref, j):
    m = jnp.max(s, axis=1, keepdims=True)
    p = jnp.exp2(s - m)
    m_ref[j] = m
    l_ref[j] = jnp.sum(p, axis=1, keepdims=True)
    acc_ref[j] = _dot(p.astype(BF16), v)


def _softmax_step(s, v, m_ref, l_ref, acc_ref, j):
    m_old = m_ref[j]
    m_new = jnp.maximum(m_old, jnp.max(s, axis=1, keepdims=True))
    alpha = jnp.exp2(m_old - m_new)
    p = jnp.exp2(s - m_new)
    l_ref[j] = alpha * l_ref[j] + jnp.sum(p, axis=1, keepdims=True)
    acc_ref[j] = alpha * acc_ref[j] + _dot(p.astype(BF16), v)
    m_ref[j] = m_new


def _diff_lambda(lamv, lam_init):
    a = jnp.sum(lamv[0:1] * lamv[1:2], axis=1, keepdims=True)
    b = jnp.sum(lamv[2:3] * lamv[3:4], axis=1, keepdims=True)
    return jnp.exp(a) - jnp.exp(b) + lam_init


def _split_maps(q):
    lane = lax.broadcasted_iota(jnp.int32, q.shape, 1)
    zero = jnp.zeros_like(q)
    return jnp.where(lane < DA_D, q, zero), jnp.where(lane >= DA_D, q, zero)


def _seed_t(st, vt, m_ref, acc_ref, j):
    m = jnp.max(st, axis=0, keepdims=True)
    m_ref[j] = m
    acc_ref[j] = _dot(vt, jnp.exp2(st - m).astype(BF16))


def _step_t(st, shift, vt, m_ref, acc_ref, j):
    m_old = m_ref[j]
    m_new = jnp.maximum(m_old, jnp.max(st, axis=0, keepdims=True) + shift)
    p = jnp.exp2(st - (m_new - shift))
    acc_ref[j] = jnp.exp2(m_old - m_new) * acc_ref[j] + _dot(vt, p.astype(BF16))
    m_ref[j] = m_new


def _diff_attn_kernel(qi_ref, ki_ref, qt_ref, k_ref, vt_ref, mk_ref, mvt_ref, pos_ref, cq_ref, corr_ref,
                      lamv_ref, g_ref, o_ref, m_s, acc_s, *, tq, lam_init):
    t = pl.program_id(1)
    qi = qi_ref[t]
    ki = ki_ref[t]
    qrow = lax.broadcasted_iota(jnp.int32, (HEAD_W, tq), 0) < DA_D
    klane = lax.broadcasted_iota(jnp.int32, (tq, HEAD_W), 1) < DA_D

    @pl.when(ki == 0)
    def _seed():
        for h in range(DA_HEADS):
            hs = slice(h * HEAD_W, (h + 1) * HEAD_W)
            qt = qt_ref[hs, :]
            zero = jnp.zeros_like(qt)
            for c in range(2):
                qc = jnp.where(qrow, qt, zero) if c == 0 else jnp.where(qrow, zero, qt)
                _seed_t(_dot(mk_ref[:, hs], qc), mvt_ref[h * VT_W:(h + 1) * VT_W, :], m_s, acc_s, 2 * h + c)

    def body(diag):
        qpos = (lax.broadcasted_iota(jnp.int32, (1, tq), 1) + (qi - ki) * tq).astype(F32)
        pos = pos_ref[...]

        def scores(h):
            hs = slice(h * HEAD_W, (h + 1) * HEAD_W)
            qt = qt_ref[hs, :]
            cq = cq_ref[h]
            kk = k_ref[:, hs]
            corr = corr_ref[...] * (_alibi_slope(h) * LOG2E) if diag else None
            out = []
            for c in range(2):
                qc = jnp.where(qrow, qt, cq) if c == 0 else jnp.where(qrow, cq, qt)
                kc = jnp.where(klane, kk, pos) if c == 0 else jnp.where(klane, pos, kk)
                st = _dot(kc, qc)
                out.append(st + corr if diag else st)
            return out

        queue = [scores(h) for h in range(SCORE_LOOKAHEAD)]
        for h in range(DA_HEADS):
            cur = queue.pop(0)
            if h + SCORE_LOOKAHEAD < DA_HEADS:
                queue.append(scores(h + SCORE_LOOKAHEAD))
            shift = qpos * (-(_alibi_slope(h) * LOG2E))
            vt = vt_ref[h * VT_W:(h + 1) * VT_W, :]
            for c in range(2):
                _step_t(cur[c], shift, vt, m_s, acc_s, 2 * h + c)

    @pl.when(ki != qi)
    def _off_diagonal():
        body(False)

    @pl.when(ki == qi)
    def _diagonal():
        body(True)
        lam = _diff_lambda(lamv_ref[...], lam_init)
        g = g_ref[...]
        for h in range(DA_HEADS):
            a0 = acc_s[2 * h]
            a1 = acc_s[2 * h + 1]
            ot = a0[0:DA_V] / a0[DA_V:DA_V + 1] - lam * (a1[0:DA_V] / a1[DA_V:DA_V + 1])
            ot = ot * lax.rsqrt(jnp.mean(ot * ot, axis=0, keepdims=True) + EPS) * g * (1.0 - lam_init)
            o_ref[:, h * HEAD_W:(h + 1) * HEAD_W] = ot.T.astype(BF16)


def _mla_attn_kernel(qi_ref, ki_ref, qt_ref, k_ref, vt_ref, mk_ref, mvt_ref, mask_ref, o_ref,
                     m_s, acc_s):
    t = pl.program_id(1)
    qi = qi_ref[t]
    ki = ki_ref[t]

    @pl.when(ki == 0)
    def _seed():
        for h in range(MLA_HEADS):
            st = _dot(mk_ref[:, h * MLA_W:(h + 1) * MLA_W], qt_ref[h * MLA_W:(h + 1) * MLA_W, :])
            _seed_t(st, mvt_ref[h * VT_W:(h + 1) * VT_W, :], m_s, acc_s, h)

    def body(diag):
        def scores(h):
            st = _dot(k_ref[:, h * MLA_W:(h + 1) * MLA_W], qt_ref[h * MLA_W:(h + 1) * MLA_W, :])
            return st + mask_ref[...] if diag else st

        queue = [scores(h) for h in range(SCORE_LOOKAHEAD)]
        for h in range(MLA_HEADS):
            st = queue.pop(0)
            if h + SCORE_LOOKAHEAD < MLA_HEADS:
                queue.append(scores(h + SCORE_LOOKAHEAD))
            _step_t(st, 0.0, vt_ref[h * VT_W:(h + 1) * VT_W, :], m_s, acc_s, h)

    @pl.when(ki != qi)
    def _off_diagonal():
        body(False)

    @pl.when(ki == qi)
    def _diagonal():
        body(True)
        for h in range(MLA_HEADS):
            a = acc_s[h]
            o_ref[:, h * MLA_V:(h + 1) * MLA_V] = (a[0:MLA_V] / a[MLA_V:MLA_V + 1]).T.astype(BF16)


def _pair_tables(nq):
    qi = np.concatenate([np.full((i + 1,), i, np.int32) for i in range(nq)])
    ki = np.concatenate([np.arange(i + 1, dtype=np.int32) for i in range(nq)])
    return jnp.asarray(qi), jnp.asarray(ki)


def _tile_geometry(tq):
    j = np.arange(tq)[:, None]
    i = np.arange(tq)[None, :]
    visible = (j // CHUNK) <= (i // CHUNK)
    return i, j, visible


def _bf16_split3(x):
    parts = []
    for _ in range(3):
        p = float(np.asarray(x, np.float32).astype(BF16).astype(np.float32))
        parts.append(p)
        x = x - p
    return parts


def _alibi_operands(tq):
    assert tq <= 2 * MXU_DIM
    j = np.arange(tq)
    jlo = (j % MXU_DIM).astype(np.float32)
    jhi = (j - j % MXU_DIM).astype(np.float32)
    pos = np.zeros((tq, HEAD_W), np.float32)
    cq = np.zeros((DA_HEADS, HEAD_W, tq), np.float32)
    for base in (0, DA_D):
        for r in range(3):
            pos[:, base + 2 * r] = jlo
            pos[:, base + 2 * r + 1] = jhi
    for h in range(DA_HEADS):
        parts = _bf16_split3(_alibi_slope(h) * LOG2E)
        for base in (0, DA_D):
            for r in range(3):
                cq[h, base + 2 * r, :] = parts[r]
                cq[h, base + 2 * r + 1, :] = parts[r]
    return jnp.asarray(pos, BF16), jnp.asarray(cq, BF16)


def _prompt_attn_specs(tq, wq, wk, wv):
    qt_spec = pl.BlockSpec((None, wq, tq), lambda b, t, qi, ki: (b, 0, qi[t]))
    k_spec = pl.BlockSpec((None, tq, wk), lambda b, t, qi, ki: (b, ki[t], 0))
    vt_spec = pl.BlockSpec((None, wv, tq), lambda b, t, qi, ki: (b, 0, ki[t]))
    mk_spec = pl.BlockSpec((N_META, wk), lambda b, t, qi, ki: (0, 0))
    mvt_spec = pl.BlockSpec((wv, N_META), lambda b, t, qi, ki: (0, 0))
    return qt_spec, k_spec, vt_spec, mk_spec, mvt_spec


def _diff_attn(qt, k, vt, mk, mvt, lamv, g, *, tq, lam_init):
    b, s, w = k.shape
    nq = s // tq
    qi, ki = _pair_tables(nq)
    i, j, visible = _tile_geometry(tq)
    corr = jnp.asarray(np.where(visible, np.where(j > i, -2.0 * (j - i), 0.0), NEG_BIG).astype(np.float32))
    pos, cq = _alibi_operands(tq)
    qt_spec, k_spec, vt_spec, mk_spec, mvt_spec = _prompt_attn_specs(tq, w, w, vt.shape[1])
    full = lambda a: pl.BlockSpec(a.shape, lambda b_, t, qi_, ki_: (0,) * a.ndim)
    return pl.pallas_call(
        functools.partial(_diff_attn_kernel, tq=tq, lam_init=lam_init),
        grid_spec=pltpu.PrefetchScalarGridSpec(
            num_scalar_prefetch=2,
            grid=(b, int(qi.shape[0])),
            in_specs=[qt_spec, k_spec, vt_spec, mk_spec, mvt_spec, full(pos), full(cq), full(corr),
                      full(lamv), full(g)],
            out_specs=pl.BlockSpec((None, tq, w), lambda b_, t, qi_, ki_: (b_, qi_[t], 0)),
            scratch_shapes=[pltpu.VMEM((2 * DA_HEADS, 1, tq), F32), pltpu.VMEM((2 * DA_HEADS, VT_W, tq), F32)],
        ),
        out_shape=jax.ShapeDtypeStruct((b, s, w), BF16),
        compiler_params=_cparams(2),
        name="diff_attn",
    )(qi, ki, qt, k, vt, mk, mvt, pos, cq, corr, lamv, g)


def _mla_attn(qt, k, vt, mk, mvt, *, tq):
    b, s, wq = k.shape
    nq = s // tq
    qi, ki = _pair_tables(nq)
    _, _, visible = _tile_geometry(tq)
    mask = jnp.asarray(np.where(visible, 0.0, NEG_BIG).astype(np.float32))
    qt_spec, k_spec, vt_spec, mk_spec, mvt_spec = _prompt_attn_specs(tq, wq, wq, vt.shape[1])
    wo = MLA_HEADS * MLA_V
    return pl.pallas_call(
        _mla_attn_kernel,
        grid_spec=pltpu.PrefetchScalarGridSpec(
            num_scalar_prefetch=2,
            grid=(b, int(qi.shape[0])),
            in_specs=[qt_spec, k_spec, vt_spec, mk_spec, mvt_spec,
                      pl.BlockSpec(mask.shape, lambda b_, t, qi_, ki_: (0, 0))],
            out_specs=pl.BlockSpec((None, tq, wo), lambda b_, t, qi_, ki_: (b_, qi_[t], 0)),
            scratch_shapes=[pltpu.VMEM((MLA_HEADS, 1, tq), F32), pltpu.VMEM((MLA_HEADS, VT_W, tq), F32)],
        ),
        out_shape=jax.ShapeDtypeStruct((b, s, wo), BF16),
        compiler_params=_cparams(2),
        name="mla_attn",
    )(qi, ki, qt, k, vt, mk, mvt, mask)


def _sample_diff_kernel(q_ref, kc_hbm, vc_hbm, kn_ref, vn_ref, dc_ref, dn_ref, lamv_ref, g_ref,
                        o_ref, kbuf, vbuf, sem, m_s, l_s, acc_s, *, tk, lam_init):
    kt = pl.program_id(1)
    nkt = pl.num_programs(1)
    step = pl.program_id(0) * nkt + kt
    n_steps = pl.num_programs(0) * nkt

    def tile_copies(s, slot):
        sb = s // nkt
        rows = pl.ds((s % nkt) * tk, tk)
        cps = []
        for h in range(DA_HEADS):
            cps.append(pltpu.make_async_copy(kc_hbm.at[sb, rows, h, :], kbuf.at[slot, h], sem.at[slot, 0]))
            cps.append(pltpu.make_async_copy(vc_hbm.at[sb, rows, h, :], vbuf.at[slot, h], sem.at[slot, 1]))
        return cps

    @pl.when(step == 0)
    def _prime():
        for cp in tile_copies(0, 0):
            cp.start()

    @pl.when(step + 1 < n_steps)
    def _prefetch():
        for cp in tile_copies(step + 1, (step + 1) % 2):
            cp.start()

    slot = step % 2
    for cp in tile_copies(step, slot):
        cp.wait()

    def heads(get_k, get_v, dist, first):
        for h in range(DA_HEADS):
            q1, q2 = _split_maps(q_ref[:, h * HEAD_W:(h + 1) * HEAD_W])
            qq = jnp.concatenate([q1, q2], axis=0)
            s = _dot_nt(qq, get_k(h)) + dist * (-_alibi_slope(h) * LOG2E)
            if first:
                _softmax_seed(s, get_v(h), m_s, l_s, acc_s, h)
            else:
                _softmax_step(s, get_v(h), m_s, l_s, acc_s, h)

    cache_k = lambda h: kbuf[slot, h].astype(BF16)
    cache_v = lambda h: vbuf[slot, h].astype(BF16)

    @pl.when(kt == 0)
    def _first():
        heads(cache_k, cache_v, dc_ref[...], True)

    @pl.when(kt > 0)
    def _rest():
        heads(cache_k, cache_v, dc_ref[...], False)

    @pl.when(kt == nkt - 1)
    def _finish():
        heads(lambda h: kn_ref[:, h * HEAD_W:(h + 1) * HEAD_W], lambda h: vn_ref[:, h * HEAD_W:(h + 1) * HEAD_W],
              dn_ref[...], False)
        lam = _diff_lambda(lamv_ref[...], lam_init)
        g = g_ref[...]
        nq = q_ref.shape[0]
        for h in range(DA_HEADS):
            a = acc_s[h] / l_s[h]
            o = a[0:nq] - lam * a[nq:2 * nq]
            o_ref[:, h * HEAD_W:(h + 1) * HEAD_W] = (_rms(o, g) * (1.0 - lam_init)).astype(BF16)


def _sample_diff_attn(q, kc, vc, kn, vn, dist_c, dist_n, lamv, g, *, tk, lam_init):
    bs, nq, w = q.shape
    lc = kc.shape[1]
    assert lc % tk == 0
    full = lambda a: pl.BlockSpec(a.shape, lambda b, t: (0,) * a.ndim)
    per_stream = lambda a: pl.BlockSpec((None,) + a.shape[1:], lambda b, t: (b,) + (0,) * (a.ndim - 1))
    cache = pl.BlockSpec(memory_space=pl.ANY)
    return pl.pallas_call(
        functools.partial(_sample_diff_kernel, tk=tk, lam_init=lam_init),
        grid=(bs, lc // tk),
        in_specs=[per_stream(q), cache, cache, per_stream(kn), per_stream(vn),
                  pl.BlockSpec((None, 2 * nq, tk), lambda b, t: (t, 0, 0)),
                  full(dist_n), full(lamv), full(g)],
        out_specs=per_stream(q),
        out_shape=jax.ShapeDtypeStruct((bs, nq, w), BF16),
        scratch_shapes=[pltpu.VMEM((2, DA_HEADS, tk, HEAD_W), F32), pltpu.VMEM((2, DA_HEADS, tk, HEAD_W), F32),
                        pltpu.SemaphoreType.DMA((2, 2)),
                        pltpu.VMEM((DA_HEADS, 2 * nq, 1), F32), pltpu.VMEM((DA_HEADS, 2 * nq, 1), F32),
                        pltpu.VMEM((DA_HEADS, 2 * nq, DA_V), F32)],
        compiler_params=_cparams(2),
        name="sample_diff_attn",
    )(q, kc, vc, kn, vn, dist_c, dist_n, lamv, g)


def _sample_mla_kernel(q_ref, cc_ref, krc_ref, cn_ref, krn_ref, wuk_ref, wuv_ref, o_ref,
                       ql_s, qr_s, m_s, l_s, acc_s):
    kt = pl.program_id(1)
    nkt = pl.num_programs(1)
    nq = q_ref.shape[0]

    @pl.when(kt == 0)
    def _prep():
        for h in range(MLA_HEADS):
            qn = q_ref[:, h * MLA_W:h * MLA_W + MLA_NOPE]
            ql_s[h * nq:(h + 1) * nq, :] = _dot_nt(qn, wuk_ref[:, h * MLA_NOPE:(h + 1) * MLA_NOPE]).astype(BF16)
            qr_s[h * nq:(h + 1) * nq, :] = q_ref[:, h * MLA_W + MLA_NOPE:(h + 1) * MLA_W]

    def scores(c_ref, kr_ref):
        cb = c_ref[...].astype(BF16)
        krb = kr_ref[...].astype(BF16)
        s = _dot_nt(ql_s[...], cb) + _dot_nt(qr_s[:, 0:MLA_ROPE], krb)
        return s, cb

    @pl.when(kt == 0)
    def _first():
        s, cb = scores(cc_ref, krc_ref)
        _softmax_seed(s, cb, m_s, l_s, acc_s, 0)

    @pl.when(kt > 0)
    def _rest():
        s, cb = scores(cc_ref, krc_ref)
        _softmax_step(s, cb, m_s, l_s, acc_s, 0)

    @pl.when(kt == nkt - 1)
    def _finish():
        s, cb = scores(cn_ref, krn_ref)
        _softmax_step(s, cb, m_s, l_s, acc_s, 0)
        ol = (acc_s[0] / l_s[0]).astype(BF16)
        for h in range(MLA_HEADS):
            o_ref[:, h * MLA_V:(h + 1) * MLA_V] = _dot(
                ol[h * nq:(h + 1) * nq, :], wuv_ref[:, h * MLA_V:(h + 1) * MLA_V]).astype(BF16)


def _sample_mla_attn(q, cc, krc, cn, krn, wuk, wuv, *, tk):
    bs, nq, wq = q.shape
    lc, kvl = cc.shape[1], cc.shape[2]
    assert lc % tk == 0
    full = lambda a: pl.BlockSpec(a.shape, lambda b, t: (0,) * a.ndim)
    per_stream = lambda a: pl.BlockSpec((None,) + a.shape[1:], lambda b, t: (b,) + (0,) * (a.ndim - 1))
    rows = MLA_HEADS * nq
    return pl.pallas_call(
        _sample_mla_kernel,
        grid=(bs, lc // tk),
        in_specs=[per_stream(q),
                  pl.BlockSpec((None, tk, kvl), lambda b, t: (b, t, 0)),
                  pl.BlockSpec((None, tk, MLA_ROPE), lambda b, t: (b, t, 0)),
                  per_stream(cn), per_stream(krn), full(wuk), full(wuv)],
        out_specs=pl.BlockSpec((None, nq, MLA_HEADS * MLA_V), lambda b, t: (b, 0, 0)),
        out_shape=jax.ShapeDtypeStruct((bs, nq, MLA_HEADS * MLA_V), BF16),
        scratch_shapes=[pltpu.VMEM((rows, kvl), BF16), pltpu.VMEM((rows, LANES), BF16),
                        pltpu.VMEM((1, rows, 1), F32), pltpu.VMEM((1, rows, 1), F32),
                        pltpu.VMEM((1, rows, kvl), F32)],
        compiler_params=_cparams(2),
        name="sample_mla_attn",
    )(q, cc, krc, cn, krn, wuk, wuv)


ROUTER_ROWS = SUBLANES * (1 + N_GROUPS)


def _route(lt):
    g = [lt[i:i + 1] for i in range(N_GROUPS)]
    gmax = functools.reduce(jnp.maximum, g)
    gidx = jnp.full_like(gmax, float(N_GROUPS - 1))
    for i in range(N_GROUPS - 2, -1, -1):
        gidx = jnp.where(g[i] == gmax, float(i), gidx)
    den = functools.reduce(lambda a, b: a + b, [jnp.exp(gi - gmax) for gi in g])
    p_top = 1.0 / den
    e = []
    for j in range(EXPERTS_PER_GROUP):
        ej = lt[SUBLANES * N_GROUPS + j:SUBLANES * N_GROUPS + j + 1]
        for grp in range(N_GROUPS - 2, -1, -1):
            ej = jnp.where(gidx == float(grp), lt[SUBLANES * (grp + 1) + j:SUBLANES * (grp + 1) + j + 1], ej)
        e.append(ej)

    def first_argmax(vals):
        vmax = functools.reduce(jnp.maximum, vals)
        idx = jnp.full_like(vmax, float(len(vals) - 1))
        for i in range(len(vals) - 2, -1, -1):
            idx = jnp.where(vals[i] == vmax, float(i), idx)
        return vmax, idx

    v1, i1 = first_argmax(e)
    rest = [jnp.where(i1 == float(j), -jnp.inf, e[j]) for j in range(EXPERTS_PER_GROUP)]
    v2, i2 = first_argmax(rest)
    r = jnp.exp(v2 - v1)
    w1 = p_top / (1.0 + r)
    w2 = p_top * r / (1.0 + r)
    base = gidx * float(EXPERTS_PER_GROUP)
    return w1, w2, base + i1, base + i2


def _merge_kernel(od_ref, om_ref, x_ref, wo_ref, g2_ref, wr_ref, br_ref, hp_ref, xn_ref, rt_ref):
    nd = od_ref.shape[1]
    y = _dot(od_ref[...], wo_ref[0:nd, :]) + _dot(om_ref[...], wo_ref[nd:, :])
    hp = x_ref[...] + y
    hp_ref[...] = hp
    xn = _rms(hp, g2_ref[...])
    xn_ref[...] = _pack_halves(xn)
    lt = _dot_nt(wr_ref[...], xn.astype(BF16)) + br_ref[...]
    rows = _route(lt)
    for i, r in enumerate(rows):
        rt_ref[i:i + 1, :] = r
    rt_ref[4:8, :] = jnp.zeros((4, rt_ref.shape[1]), F32)


def _merge(od, om, x, wo, g2, wr, br, *, tm):
    m, d = x.shape
    row = lambda w: pl.BlockSpec((tm, w), lambda i: (i, 0))
    return pl.pallas_call(
        _merge_kernel,
        grid=(m // tm,),
        in_specs=[row(od.shape[1]), row(om.shape[1]), row(d), _const_spec(wo.shape),
                  _const_spec(g2.shape), _const_spec(wr.shape), _const_spec(br.shape)],
        out_specs=[row(d), row(d // 2), pl.BlockSpec((SUBLANES, tm), lambda i: (0, i))],
        out_shape=[jax.ShapeDtypeStruct((m, d), F32), jax.ShapeDtypeStruct((m, d // 2), jnp.uint32),
                   jax.ShapeDtypeStruct((SUBLANES, m), F32)],
        compiler_params=_cparams(1),
        name="merge",
    )(od, om, x, wo, g2, wr, br)


def _swiglu(xp, wg_ref, wu_ref, wd_ref):
    lo, hi = _unpack_halves(xp)
    lo = lo.astype(BF16)
    hi = hi.astype(BF16)
    w = lo.shape[1]
    g = _dot(lo, wg_ref[0:w, :]) + _dot(hi, wg_ref[w:, :])
    u = _dot(lo, wu_ref[0:w, :]) + _dot(hi, wu_ref[w:, :])
    h = (g * jax.nn.sigmoid(g) * u).astype(BF16)
    return _dot(h, wd_ref[...])


def _moe_sorted_kernel(te_ref, nu_ref, x_ref, w_ref, wg_ref, wu_ref, wd_ref, y_ref):
    i = pl.program_id(0)

    @pl.when(i < nu_ref[0])
    def _():
        y_ref[...] = _pack_halves(w_ref[...] * _swiglu(x_ref[...], wg_ref, wu_ref, wd_ref))

    @pl.when(i >= nu_ref[0])
    def _():
        y_ref[...] = jnp.zeros_like(y_ref)


def _moe_sorted(tile_expert, n_used, xs, ws, wg, wu, wd, *, tm):
    n, dh = xs.shape
    d = 2 * dh
    f = wg.shape[2]
    nt = n // tm
    return pl.pallas_call(
        _moe_sorted_kernel,
        grid_spec=pltpu.PrefetchScalarGridSpec(
            num_scalar_prefetch=2,
            grid=(nt,),
            in_specs=[pl.BlockSpec((tm, dh), lambda i, te, nu: (i, 0)),
                      pl.BlockSpec((tm, 1), lambda i, te, nu: (i, 0)),
                      pl.BlockSpec((None, d, f), lambda i, te, nu: (te[i], 0, 0)),
                      pl.BlockSpec((None, d, f), lambda i, te, nu: (te[i], 0, 0)),
                      pl.BlockSpec((None, f, d), lambda i, te, nu: (te[i], 0, 0))],
            out_specs=pl.BlockSpec((tm, dh), lambda i, te, nu: (i, 0)),
        ),
        out_shape=jax.ShapeDtypeStruct((n, dh), jnp.uint32),
        compiler_params=_cparams(1),
        name="moe_sorted",
    )(tile_expert, n_used, xs, ws, wg, wu, wd)


def _moe_dense_kernel(x_ref, hp_ref, gates_ref, wg_ref, wu_ref, wd_ref, gf_ref, o_ref, acc_s):
    e = pl.program_id(0)

    @pl.when(e == 0)
    def _():
        acc_s[...] = jnp.zeros_like(acc_s)

    lane = lax.broadcasted_iota(jnp.int32, gates_ref.shape, 1)
    gate = jnp.sum(jnp.where(lane == e, gates_ref[...], 0.0), axis=1, keepdims=True)
    acc_s[...] += gate * _swiglu(x_ref[...], wg_ref, wu_ref, wd_ref)

    @pl.when(e == pl.num_programs(0) - 1)
    def _():
        o_ref[...] = _rms(hp_ref[...] + acc_s[...], gf_ref[...])


def _moe_dense(xn, hp, gates, wg, wu, wd, gf):
    m, d = hp.shape
    ne, _, f = wg.shape
    full = lambda a: pl.BlockSpec(a.shape, lambda e: (0,) * a.ndim)
    return pl.pallas_call(
        _moe_dense_kernel,
        grid=(ne,),
        in_specs=[full(xn), full(hp), full(gates),
                  pl.BlockSpec((None, d, f), lambda e: (e, 0, 0)),
                  pl.BlockSpec((None, d, f), lambda e: (e, 0, 0)),
                  pl.BlockSpec((None, f, d), lambda e: (e, 0, 0)),
                  full(gf)],
        out_specs=full(hp),
        out_shape=jax.ShapeDtypeStruct((m, d), F32),
        scratch_shapes=[pltpu.VMEM((m, d), F32)],
        compiler_params=_cparams(1),
        name="moe_dense",
    )(xn, hp, gates, wg, wu, wd, gf)


def _combine_kernel(hp_ref, y1_ref, y2_ref, gf_ref, o_ref):
    w = y1_ref.shape[1]
    a_lo, a_hi = _unpack_halves(y1_ref[...])
    b_lo, b_hi = _unpack_halves(y2_ref[...])
    s_lo = hp_ref[:, 0:w] + (a_lo + b_lo)
    s_hi = hp_ref[:, w:] + (a_hi + b_hi)
    ms = (jnp.sum(s_lo * s_lo, axis=-1, keepdims=True) + jnp.sum(s_hi * s_hi, axis=-1, keepdims=True)) / (2 * w)
    inv = lax.rsqrt(ms + EPS)
    o_ref[:, 0:w] = s_lo * inv * gf_ref[:, 0:w]
    o_ref[:, w:] = s_hi * inv * gf_ref[:, w:]


def _combine(hp, y1, y2, gf, *, tm):
    m, d = hp.shape
    row = pl.BlockSpec((tm, d), lambda i: (i, 0))
    half = pl.BlockSpec((tm, d // 2), lambda i: (i, 0))
    return pl.pallas_call(
        _combine_kernel,
        grid=(m // tm,),
        in_specs=[row, half, half, _const_spec(gf.shape)],
        out_specs=row,
        out_shape=jax.ShapeDtypeStruct((m, d), F32),
        compiler_params=_cparams(1),
        name="combine",
    )(hp, y1, y2, gf)


def _rope_table(pos):
    half = MLA_ROPE // 2
    inv_freq = ROPE_THETA ** (-jnp.arange(half, dtype=F32) / half)
    ang = pos.astype(F32)[:, None] * inv_freq[None, :]
    c, s = jnp.cos(ang), jnp.sin(ang)
    return jnp.concatenate([c, c, -s, s], axis=1)


def _swap_halves(w):
    half = MLA_ROPE // 2
    return jnp.concatenate([w[..., half:], w[..., :half]], axis=-1)


def _values_t(v):
    n = v.shape[0]
    vt = v.reshape(n, DA_HEADS, DA_V).transpose(1, 2, 0)
    return jnp.concatenate([vt, jnp.ones((DA_HEADS, BF16_ROWS, n), v.dtype)], axis=1).reshape(DA_HEADS * VT_W, n)


def _sort_by_expert(eid, w, tm):
    t = eid.shape[1]
    flat_e = eid.reshape(-1)
    onehot = (flat_e[:, None] == jnp.arange(N_EXPERTS, dtype=jnp.int32)[None, :]).astype(jnp.int32)
    rank = jnp.sum((jnp.cumsum(onehot, axis=0) - onehot) * onehot, axis=1)
    counts = jnp.sum(onehot, axis=0)
    tiles_per = (counts + tm - 1) // tm
    tiles_end = jnp.cumsum(tiles_per)
    row_start = (tiles_end - tiles_per) * tm
    pos = row_start[flat_e] + rank
    n_tiles = (2 * t) // tm + N_EXPERTS
    slot_a = jnp.full((n_tiles * tm,), -1, jnp.int32).at[pos].set(jnp.arange(2 * t, dtype=jnp.int32),
                                                                   unique_indices=True, mode="promise_in_bounds")
    used = slot_a >= 0
    safe_a = jnp.maximum(slot_a, 0)
    sorted_tok = jnp.where(used, safe_a % t, 0)
    sorted_w = jnp.where(used, w.reshape(-1).at[safe_a].get(mode="promise_in_bounds"), 0.0)
    tile_ids = jnp.arange(n_tiles, dtype=jnp.int32)
    tile_expert = jnp.minimum(jnp.sum((tiles_end[None, :] <= tile_ids[:, None]).astype(jnp.int32), axis=1),
                              N_EXPERTS - 1)
    n_used = tiles_end[-1:].astype(jnp.int32)
    return pos.reshape(2, t), sorted_tok, sorted_w, tile_expert, n_used


def kernel(x_prompt, x_sample, cache_diff_k, cache_diff_v, cache_mla_ckv, cache_mla_kr, meta_tokens, norm1_g, w_in, diff_lam_q1, diff_lam_k1, diff_lam_q2, diff_lam_k2, diff_subln_g, mla_q_norm_g, mla_w_uq, mla_kv_norm_g, mla_w_uk, mla_w_uv, w_o, norm2_g, router_group_w, router_group_b, router_expert_w, router_expert_b, expert_w_gate, expert_w_up, expert_w_down, final_norm_g):
    depth = norm1_g.shape[0]
    assert depth == 1, "single-layer step only"
    assert MLA_HEADS == DA_HEADS and MLA_V == DA_V
    lam_init = 0.8 - 0.6 * math.exp(-0.3 * 0)
    b, s, d = x_prompt.shape
    bs, ss, _ = x_sample.shape
    past = cache_mla_kr.shape[2]
    lc = N_META + past
    c_qk = DA_HEADS * 2 * DA_D
    c_v = DA_HEADS * DA_V
    c_ql = mla_q_norm_g.shape[1]
    c_kvl = mla_kv_norm_g.shape[1]
    o5 = 2 * c_qk + c_v + c_ql + c_kvl

    win = w_in[0]
    win_ext = jnp.concatenate([win, _swap_halves(win[:, o5:])], axis=1).astype(BF16)
    wuq = mla_w_uq[0].reshape(c_ql, MLA_HEADS, MLA_NOPE + MLA_ROPE)
    wuq_n = wuq[:, :, :MLA_NOPE].reshape(c_ql, MLA_HEADS * MLA_NOPE)
    wuq_r = jnp.concatenate([wuq[:, :, MLA_NOPE:], _swap_halves(wuq[:, :, MLA_NOPE:])], axis=2)
    wuq_ext = jnp.concatenate([wuq_n, wuq_r.reshape(c_ql, MLA_HEADS * LANES)], axis=1).astype(BF16)
    wuk = mla_w_uk[0].astype(BF16)
    wuv = mla_w_uv[0].astype(BF16)
    wukv = jnp.concatenate([wuk, wuv], axis=1)
    wo = w_o[0].astype(BF16)
    wr = jnp.zeros((ROUTER_ROWS, d), F32).at[0:N_GROUPS].set(router_group_w[0].T)
    br = jnp.zeros((ROUTER_ROWS, 1), F32).at[0:N_GROUPS, 0].set(router_group_b[0])
    rew = router_expert_w[0].T.reshape(N_GROUPS, EXPERTS_PER_GROUP, d)
    reb = router_expert_b[0].reshape(N_GROUPS, EXPERTS_PER_GROUP)
    for grp in range(N_GROUPS):
        wr = wr.at[SUBLANES * (grp + 1):SUBLANES * (grp + 1) + EXPERTS_PER_GROUP].set(rew[grp])
        br = br.at[SUBLANES * (grp + 1):SUBLANES * (grp + 1) + EXPERTS_PER_GROUP, 0].set(reb[grp])
    wr = wr.astype(BF16)
    wg = expert_w_gate[0].astype(BF16)
    wu = expert_w_up[0].astype(BF16)
    wd = expert_w_down[0].astype(BF16)
    gf = final_norm_g[None, :]
    lamv = jnp.stack([diff_lam_q1[0], diff_lam_k1[0], diff_lam_q2[0], diff_lam_k2[0]])
    subg = diff_subln_g

    dims = (c_qk, c_v, c_ql, c_kvl)
    inproj = functools.partial(_inproj, g1=norm1_g, win=win_ext, qg=mla_q_norm_g, wuq=wuq_ext,
                               kvg=mla_kv_norm_g, wukv=wukv, dims=dims)

    (_, mdk32, mdk, mdv32, mdv, _, mckv, _, mkm, mvm) = inproj(
        meta_tokens, jnp.zeros((N_META, LANES), F32), tm=N_META, tab_blocks=1)

    ts = bs * ss
    s_pos = past + jnp.arange(ss, dtype=jnp.int32)
    (sqd, s_dk, skd, s_dv, svd, sqm, sckv, skr, _, _) = inproj(
        x_sample.reshape(ts, d), _rope_table(s_pos), tm=ss, tab_blocks=1)
    q3 = lambda a: a.reshape(bs, ss, a.shape[-1])
    kpos_c = np.arange(lc) - N_META
    dist_c = np.where(kpos_c[None, :] >= 0, np.abs(past + np.arange(ss)[:, None] - kpos_c[None, :]), 0)
    dist_n = np.abs(np.arange(ss)[:, None] - np.arange(ss)[None, :])
    tk_s = lc // 2 if (lc // 2) % SUBLANES == 0 and lc % 2 == 0 else lc
    dist_c = np.tile(dist_c, (2, 1)).astype(np.float32).reshape(2 * ss, lc // tk_s, tk_s)
    dist_c = jnp.asarray(np.moveaxis(dist_c, 1, 0))
    dist_n = jnp.asarray(np.tile(dist_n, (2, 1)).astype(np.float32))
    sod = _sample_diff_attn(q3(sqd), cache_diff_k[0], cache_diff_v[0], q3(skd), q3(svd), dist_c, dist_n, lamv, subg,
                            tk=tk_s, lam_init=lam_init)
    krc = jnp.concatenate([jnp.zeros((bs, N_META, MLA_ROPE), F32), cache_mla_kr[0]], axis=1)
    som = _sample_mla_attn(q3(sqm), cache_mla_ckv[0], krc, q3(sckv), q3(skr), wuk, wuv, tk=tk_s)
    hs, xn2s, rts = _merge(sod.reshape(ts, -1), som.reshape(ts, -1), x_sample.reshape(ts, d), wo, norm2_g, wr, br,
                           tm=ts)
    eids = rts[2:4].astype(jnp.int32)
    gates = (jnp.where(eids[0][:, None] == jnp.arange(LANES)[None, :], rts[0][:, None], 0.0)
             + jnp.where(eids[1][:, None] == jnp.arange(LANES)[None, :], rts[1][:, None], 0.0))
    y_sample = _moe_dense(xn2s, hs, gates, wg, wu, wd, gf).reshape(bs, ss, d)

    tm_p = min(INPROJ_ROWS, s)
    tab_p = _rope_table(jnp.arange(s, dtype=jnp.int32))
    (pqdt, p_dk, pkd, p_dv, pvdt, pqmt, p_ckv, pkr, pkm, pvmt) = inproj(
        x_prompt.reshape(b * s, d), tab_p, tm=tm_p, tab_blocks=s // tm_p, batch=b, meta=(mdk32, mdv32, mckv))
    tq = min(ATTN_TILE, s)
    r3 = lambda a: a.reshape(b, s, a.shape[-1])
    od = _diff_attn(pqdt, r3(pkd), pvdt, mdk, _values_t(mdv), lamv, subg.T, tq=tq, lam_init=lam_init)
    om = _mla_attn(pqmt, r3(pkm), pvmt, mkm, _values_t(mvm), tq=tq)
    t = b * s
    tm_t = min(TOKEN_ROWS, t)
    hp, xn2, rt = _merge(od.reshape(t, -1), om.reshape(t, -1), x_prompt.reshape(t, d), wo, norm2_g, wr, br, tm=tm_t)
    pos, sorted_tok, sorted_w, tile_expert, n_used = _sort_by_expert(rt[2:4].astype(jnp.int32), rt[0:2], tm_t)
    rows = lambda a, idx: a.at[idx].get(mode="promise_in_bounds")
    ys = _moe_sorted(tile_expert, n_used, rows(xn2, sorted_tok), sorted_w[:, None], wg, wu, wd, tm=tm_t)
    y_prompt = _combine(hp, rows(ys, pos[0]), rows(ys, pos[1]), gf, tm=tm_t).reshape(b, s, d)

    return (y_prompt, y_sample,
            p_dk[None], p_dv[None], p_ckv[None], pkr.reshape(1, b, s, MLA_ROPE),
            s_dk.reshape(1, bs, ss, DA_HEADS, 2 * DA_D), s_dv.reshape(1, bs, ss, DA_HEADS, DA_V),
            sckv.reshape(1, bs, ss, c_kvl), skr.reshape(1, bs, ss, MLA_ROPE))
```

```python
import functools
import math

import numpy as np
import jax
import jax.numpy as jnp
from jax import lax
from jax.experimental import pallas as pl
from jax.experimental.pallas import tpu as pltpu

CHUNK = 64
N_META = 16
EPS = 1e-6
DA_HEADS = 8
DA_D = 64
DA_V = 2 * DA_D
MLA_HEADS = 8
MLA_NOPE = 128
MLA_ROPE = 64
MLA_V = 128
ROPE_THETA = 10000.0
MLA_SCALE = (MLA_NOPE + MLA_ROPE) ** -0.5
N_GROUPS = 4
EXPERTS_PER_GROUP = 4
N_EXPERTS = N_GROUPS * EXPERTS_PER_GROUP
LOG2E = math.log2(math.e)
LANES = 128
SUBLANES = 8
BF16_ROWS = 16
MXU_DIM = 256
HEAD_W = 128
MLA_W = 256
VT_W = DA_V + BF16_ROWS
NEG_BIG = -1e30
VMEM_LIMIT = 56 * 1024 * 1024
ATTN_TILE = 512
SCORE_LOOKAHEAD = 2
INPROJ_ROWS = 256
TOKEN_ROWS = 512
CAST_ROWS = 1024

BF16 = jnp.bfloat16
F32 = jnp.float32


def _dot(a, b):
    return jnp.dot(a, b, preferred_element_type=F32)


def _dot_nt(a, b):
    return lax.dot_general(a, b, (((1,), (1,)), ((), ())), preferred_element_type=F32)


def _rms(x, g):
    return x * lax.rsqrt(jnp.mean(x * x, axis=-1, keepdims=True) + EPS) * g


def _pack_halves(x):
    w = x.shape[1] // 2
    bits = lax.bitcast_convert_type(x.astype(BF16).astype(F32), jnp.uint32)
    return (bits[:, :w] >> 16) | (bits[:, w:] & jnp.uint32(0xFFFF0000))


def _unpack_halves(u):
    lo = lax.bitcast_convert_type(u << 16, F32)
    hi = lax.bitcast_convert_type(u & jnp.uint32(0xFFFF0000), F32)
    return lo, hi


def _cparams(n_axes):
    return pltpu.CompilerParams(dimension_semantics=("arbitrary",) * n_axes,
                                vmem_limit_bytes=VMEM_LIMIT)


def _const_spec(shape):
    nd = len(shape)
    return pl.BlockSpec(shape, lambda *_: (0,) * nd, pipeline_mode=pl.Buffered(1))


def _alibi_slope(h):
    return 2.0 ** (-8.0 * (h + 1) / DA_HEADS)


def _inproj_kernel(*refs, c_qk, c_v, c_ql, c_kvl, tab_blocks, prompt):
    (x_ref, tab_ref, g1_ref, win_ref, qg_ref, wuq_ref, kvg_ref, wukv_ref), refs = refs[:8], refs[8:]
    if prompt:
        (mk32_ref, mv32_ref, mckv_ref), refs = refs[:3], refs[3:]
    (qd_ref, kd32_ref, kdb_ref, vd32_ref, vdb_ref, qm_ref, ckv_ref, kr_ref, km_ref, vm_ref), refs = refs[:10], refs[10:]
    x = x_ref[...]
    tm = x.shape[0]
    if prompt:
        kbuf, vbuf, cbuf, sem, msem = refs
        i = pl.program_id(0)
        n = pl.num_programs(0)
        slot = i % 2

        def tile_copies(step, s):
            sb = step // tab_blocks
            rows = pl.ds(N_META + (step % tab_blocks) * tm, tm)
            return [pltpu.make_async_copy(kbuf.at[s], kd32_ref.at[sb, rows], sem.at[s, 0]),
                    pltpu.make_async_copy(vbuf.at[s], vd32_ref.at[sb, rows], sem.at[s, 1]),
                    pltpu.make_async_copy(cbuf.at[s], ckv_ref.at[sb, rows], sem.at[s, 2])]

        @pl.when(i >= 2)
        def _slot_free():
            for cp in tile_copies(i - 2, slot):
                cp.wait()

    xn = _rms(x, g1_ref[...]).astype(BF16)
    tab = tab_ref[...]
    o1 = c_qk
    o2 = o1 + c_qk
    o3 = o2 + c_v
    o4 = o3 + c_ql
    o5 = o4 + c_kvl
    ones = jnp.ones((BF16_ROWS, tm), BF16)

    def put_heads32(ref, buf, z):
        for h in range(DA_HEADS):
            if prompt:
                buf[slot, :, h, :] = z[:, h * HEAD_W:(h + 1) * HEAD_W]
            else:
                ref[:, h, :] = z[:, h * HEAD_W:(h + 1) * HEAD_W]

    def put_values_t(ref, z):
        for h in range(DA_HEADS):
            ref[h * VT_W:h * VT_W + DA_V, :] = z[:, h * DA_V:(h + 1) * DA_V].T.astype(BF16)
            ref[h * VT_W + DA_V:(h + 1) * VT_W, :] = ones

    zq = _dot(xn, win_ref[:, 0:o1]) * (DA_D ** -0.5 * LOG2E)
    qd_ref[...] = zq.T.astype(BF16) if prompt else zq.astype(BF16)
    zk = _dot(xn, win_ref[:, o1:o2])
    put_heads32(kd32_ref, kbuf if prompt else None, zk)
    kdb_ref[...] = zk.astype(BF16)
    zv = _dot(xn, win_ref[:, o2:o3])
    put_heads32(vd32_ref, vbuf if prompt else None, zv)
    if prompt:
        put_values_t(vdb_ref, zv)
    else:
        vdb_ref[...] = zv.astype(BF16)

    cq = _rms(_dot(xn, win_ref[:, o3:o4]), qg_ref[...]).astype(BF16)
    nq = MLA_HEADS * MLA_NOPE
    qn = _dot(cq, wuq_ref[:, 0:nq]) * (MLA_SCALE * LOG2E)
    qr = _dot(cq, wuq_ref[:, nq:2 * nq]) * (MLA_SCALE * LOG2E)
    for h in range(MLA_HEADS):
        u = qr[:, h * LANES:(h + 1) * LANES] * tab
        rot = u + pltpu.roll(u, MLA_ROPE, 1)
        nope = qn[:, h * LANES:(h + 1) * LANES]
        if prompt:
            qm_ref[h * MLA_W:h * MLA_W + LANES, :] = nope.T.astype(BF16)
            qm_ref[h * MLA_W + LANES:(h + 1) * MLA_W, :] = rot.T.astype(BF16)
        else:
            qm_ref[:, h * MLA_W:h * MLA_W + LANES] = nope.astype(BF16)
            qm_ref[:, h * MLA_W + LANES:(h + 1) * MLA_W] = rot.astype(BF16)

    ckv = _rms(_dot(xn, win_ref[:, o4:o5]), kvg_ref[...])
    if prompt:
        cbuf[slot] = ckv
    else:
        ckv_ref[...] = ckv
    ckvb = ckv.astype(BF16)
    nk = MLA_HEADS * MLA_NOPE
    kn = _dot(ckvb, wukv_ref[:, 0:nk])
    vm = _dot(ckvb, wukv_ref[:, nk:nk + MLA_HEADS * MLA_V])
    if prompt:
        put_values_t(vm_ref, vm)
    else:
        vm_ref[...] = vm.astype(BF16)

    u = _dot(xn, win_ref[:, o5:o5 + LANES]) * tab
    rot = u + pltpu.roll(u, MLA_ROPE, 1)
    kr_ref[...] = rot[:, 0:MLA_ROPE]
    lane = lax.broadcasted_iota(jnp.int32, rot.shape, 1)
    krp = jnp.where(lane < MLA_ROPE, rot, 0.0).astype(BF16)
    for h in range(MLA_HEADS):
        km_ref[:, h * MLA_W:h * MLA_W + LANES] = kn[:, h * LANES:(h + 1) * LANES].astype(BF16)
        km_ref[:, h * MLA_W + LANES:(h + 1) * MLA_W] = krp

    if prompt:
        for cp in tile_copies(i, slot):
            cp.start()

        @pl.when(i % tab_blocks == 0)
        def _meta_rows():
            head = pl.ds(0, N_META)
            sb = i // tab_blocks
            cps = [pltpu.make_async_copy(mk32_ref, kd32_ref.at[sb, head], msem.at[0]),
                   pltpu.make_async_copy(mv32_ref, vd32_ref.at[sb, head], msem.at[1]),
                   pltpu.make_async_copy(mckv_ref, ckv_ref.at[sb, head], msem.at[2])]
            for cp in cps:
                cp.start()
            for cp in cps:
                cp.wait()

        @pl.when(jnp.logical_and(i == n - 1, i >= 1))
        def _drain_previous():
            for cp in tile_copies(i - 1, 1 - slot):
                cp.wait()

        @pl.when(i == n - 1)
        def _drain_last():
            for cp in tile_copies(i, slot):
                cp.wait()


def _inproj(x, tab, g1, win, qg, wuq, kvg, wukv, *, tm, tab_blocks, dims, batch=None, meta=None):
    m, d = x.shape
    c_qk, c_v, c_ql, c_kvl = dims
    assert m % tm == 0
    prompt = batch is not None
    row = lambda w: pl.BlockSpec((tm, w), lambda i: (i, 0))
    hw = DA_HEADS * VT_W
    extra_in, extra_specs, scratch = [], [], []
    if prompt:
        seq = tab_blocks * tm
        assert m == batch * seq
        col = lambda w: pl.BlockSpec((None, w, tm), lambda i: (i // tab_blocks, 0, i % tab_blocks))
        hbm = pl.BlockSpec(memory_space=pl.ANY)
        tall = (batch, N_META + seq)
        extra_in = list(meta)
        extra_specs = [_const_spec(a.shape) for a in meta]
        scratch = [pltpu.VMEM((2, tm, DA_HEADS, HEAD_W), F32), pltpu.VMEM((2, tm, DA_HEADS, HEAD_W), F32),
                   pltpu.VMEM((2, tm, c_kvl), F32), pltpu.SemaphoreType.DMA((2, 3)), pltpu.SemaphoreType.DMA((3,))]
        out_specs = [col(c_qk), hbm, row(c_qk), hbm, col(hw), col(MLA_HEADS * MLA_W), hbm,
                     row(MLA_ROPE), row(MLA_HEADS * MLA_W), col(hw)]
        out_shape = [
            jax.ShapeDtypeStruct((batch, c_qk, seq), BF16),
            jax.ShapeDtypeStruct(tall + (DA_HEADS, HEAD_W), F32),
            jax.ShapeDtypeStruct((m, c_qk), BF16),
            jax.ShapeDtypeStruct(tall + (DA_HEADS, HEAD_W), F32),
            jax.ShapeDtypeStruct((batch, hw, seq), BF16),
            jax.ShapeDtypeStruct((batch, MLA_HEADS * MLA_W, seq), BF16),
            jax.ShapeDtypeStruct(tall + (c_kvl,), F32),
            jax.ShapeDtypeStruct((m, MLA_ROPE), F32),
            jax.ShapeDtypeStruct((m, MLA_HEADS * MLA_W), BF16),
            jax.ShapeDtypeStruct((batch, hw, seq), BF16),
        ]
    else:
        heads32 = pl.BlockSpec((tm, DA_HEADS, HEAD_W), lambda i: (i, 0, 0))
        out_specs = [row(c_qk), heads32, row(c_qk), heads32, row(c_v), row(MLA_HEADS * MLA_W), row(c_kvl),
                     row(MLA_ROPE), row(MLA_HEADS * MLA_W), row(MLA_HEADS * MLA_V)]
        out_shape = [
            jax.ShapeDtypeStruct((m, c_qk), BF16),
            jax.ShapeDtypeStruct((m, DA_HEADS, HEAD_W), F32),
            jax.ShapeDtypeStruct((m, c_qk), BF16),
            jax.ShapeDtypeStruct((m, DA_HEADS, HEAD_W), F32),
            jax.ShapeDtypeStruct((m, c_v), BF16),
            jax.ShapeDtypeStruct((m, MLA_HEADS * MLA_W), BF16),
            jax.ShapeDtypeStruct((m, c_kvl), F32),
            jax.ShapeDtypeStruct((m, MLA_ROPE), F32),
            jax.ShapeDtypeStruct((m, MLA_HEADS * MLA_W), BF16),
            jax.ShapeDtypeStruct((m, MLA_HEADS * MLA_V), BF16),
        ]
    return pl.pallas_call(
        functools.partial(_inproj_kernel, c_qk=c_qk, c_v=c_v, c_ql=c_ql, c_kvl=c_kvl, tab_blocks=tab_blocks,
                          prompt=prompt),
        grid=(m // tm,),
        in_specs=[
            row(d),
            pl.BlockSpec((tm, LANES), lambda i: (i % tab_blocks, 0)),
            _const_spec(g1.shape), _const_spec(win.shape), _const_spec(qg.shape),
            _const_spec(wuq.shape), _const_spec(kvg.shape), _const_spec(wukv.shape),
        ] + extra_specs,
        out_specs=out_specs,
        out_shape=out_shape,
        scratch_shapes=scratch,
        compiler_params=_cparams(1),
        name="inproj",
    )(x, tab, g1, win, qg, wuq, kvg, wukv, *extra_in)


def _softmax_seed(s, v, m_ref, l_ref, acc_ref, j):
    m = jnp.max(s, axis=1, keepdims=True)
    p = jnp.exp2(s - m)
    m_ref[j] = m
    l_ref[j] = jnp.sum(p, axis=1, keepdims=True)
    acc_ref[j] = _dot(p.astype(BF16), v)


def _softmax_step(s, v, m_ref, l_ref, acc_ref, j):
    m_old = m_ref[j]
    m_new = jnp.maximum(m_old, jnp.max(s, axis=1, keepdims=True))
    alpha = jnp.exp2(m_old - m_new)
    p = jnp.exp2(s - m_new)
    l_ref[j] = alpha * l_ref[j] + jnp.sum(p, axis=1, keepdims=True)
    acc_ref[j] = alpha * acc_ref[j] + _dot(p.astype(BF16), v)
    m_ref[j] = m_new


def _diff_lambda(lamv, lam_init):
    a = jnp.sum(lamv[0:1] * lamv[1:2], axis=1, keepdims=True)
    b = jnp.sum(lamv[2:3] * lamv[3:4], axis=1, keepdims=True)
    return jnp.exp(a) - jnp.exp(b) + lam_init


def _split_maps(q):
    lane = lax.broadcasted_iota(jnp.int32, q.shape, 1)
    zero = jnp.zeros_like(q)
    return jnp.where(lane < DA_D, q, zero), jnp.where(lane >= DA_D, q, zero)


def _init_state(m_ref, acc_ref):
    m_ref[...] = jnp.full(m_ref.shape, NEG_BIG, F32)
    acc_ref[...] = jnp.zeros(acc_ref.shape, F32)


def _step_t(st, shift, vt, m_ref, acc_ref, j):
    m_old = m_ref[j]
    m_new = jnp.maximum(m_old, jnp.max(st, axis=0, keepdims=True) + shift)
    p = jnp.exp2(st - (m_new - shift))
    acc_ref[j] = jnp.exp2(m_old - m_new) * acc_ref[j] + _dot(vt, p.astype(BF16))
    m_ref[j] = m_new


def _diff_attn_kernel(qi_ref, ki_ref, qt_ref, k_ref, vt_ref, mk_ref, mvt_ref, pos_ref, cq_ref, corr_ref,
                      lamv_ref, g_ref, o_ref, m_s, acc_s, *, tq, lam_init):
    t = pl.program_id(1)
    qi = qi_ref[t]
    ki = ki_ref[t]
    qrow = lax.broadcasted_iota(jnp.int32, (HEAD_W, tq), 0) < DA_D
    klane = lax.broadcasted_iota(jnp.int32, (tq, HEAD_W), 1) < DA_D

    @pl.when(ki == 0)
    def _init():
        _init_state(m_s, acc_s)

    def body(diag):
        qpos = (lax.broadcasted_iota(jnp.int32, (1, tq), 1) + (qi - ki) * tq).astype(F32)
        pos = pos_ref[...]
        mlane = lax.broadcasted_iota(jnp.int32, (N_META, HEAD_W), 1) < DA_D

        def scores(h):
            hs = slice(h * HEAD_W, (h + 1) * HEAD_W)
            qt = qt_ref[hs, :]
            cq = cq_ref[h]
            kk = k_ref[:, hs]
            corr = corr_ref[...] * (_alibi_slope(h) * LOG2E) if diag else None
            mk = mk_ref[:, hs]
            mzero = jnp.zeros_like(mk)
            out = []
            for c in range(2):
                qc = jnp.where(qrow, qt, cq) if c == 0 else jnp.where(qrow, cq, qt)
                kc = jnp.where(klane, kk, pos) if c == 0 else jnp.where(klane, pos, kk)
                if diag:
                    mc = jnp.where(mlane, mk, mzero) if c == 0 else jnp.where(mlane, mzero, mk)
                    kc = jnp.concatenate([kc, mc], axis=0)
                st = _dot(kc, qc)
                out.append(st + corr if diag else st)
            return out

        queue = [scores(h) for h in range(SCORE_LOOKAHEAD)]
        for h in range(DA_HEADS):
            cur = queue.pop(0)
            if h + SCORE_LOOKAHEAD < DA_HEADS:
                queue.append(scores(h + SCORE_LOOKAHEAD))
            shift = qpos * (-(_alibi_slope(h) * LOG2E))
            vt = vt_ref[h * VT_W:(h + 1) * VT_W, :]
            if diag:
                vt = jnp.concatenate([vt, mvt_ref[h * VT_W:(h + 1) * VT_W, :]], axis=1)
            for c in range(2):
                _step_t(cur[c], shift, vt, m_s, acc_s, 2 * h + c)

    @pl.when(ki != qi)
    def _off_diagonal():
        body(False)

    @pl.when(ki == qi)
    def _diagonal():
        body(True)
        lam = _diff_lambda(lamv_ref[...], lam_init)
        g = g_ref[...]
        for h in range(DA_HEADS):
            a0 = acc_s[2 * h]
            a1 = acc_s[2 * h + 1]
            ot = a0[0:DA_V] / a0[DA_V:DA_V + 1] - lam * (a1[0:DA_V] / a1[DA_V:DA_V + 1])
            ot = ot * lax.rsqrt(jnp.mean(ot * ot, axis=0, keepdims=True) + EPS) * g * (1.0 - lam_init)
            o_ref[:, h * HEAD_W:(h + 1) * HEAD_W] = ot.T.astype(BF16)


def _mla_attn_kernel(qi_ref, ki_ref, qt_ref, k_ref, vt_ref, mk_ref, mvt_ref, mask_ref, o_ref,
                     m_s, acc_s):
    t = pl.program_id(1)
    qi = qi_ref[t]
    ki = ki_ref[t]

    @pl.when(ki == 0)
    def _init():
        _init_state(m_s, acc_s)

    def body(diag):
        def scores(h):
            kk = k_ref[:, h * MLA_W:(h + 1) * MLA_W]
            if diag:
                kk = jnp.concatenate([kk, mk_ref[:, h * MLA_W:(h + 1) * MLA_W]], axis=0)
            st = _dot(kk, qt_ref[h * MLA_W:(h + 1) * MLA_W, :])
            return st + mask_ref[...] if diag else st

        queue = [scores(h) for h in range(SCORE_LOOKAHEAD)]
        for h in range(MLA_HEADS):
            st = queue.pop(0)
            if h + SCORE_LOOKAHEAD < MLA_HEADS:
                queue.append(scores(h + SCORE_LOOKAHEAD))
            vt = vt_ref[h * VT_W:(h + 1) * VT_W, :]
            if diag:
                vt = jnp.concatenate([vt, mvt_ref[h * VT_W:(h + 1) * VT_W, :]], axis=1)
            _step_t(st, 0.0, vt, m_s, acc_s, h)

    @pl.when(ki != qi)
    def _off_diagonal():
        body(False)

    @pl.when(ki == qi)
    def _diagonal():
        body(True)
        for h in range(MLA_HEADS):
            a = acc_s[h]
            o_ref[:, h * MLA_V:(h + 1) * MLA_V] = (a[0:MLA_V] / a[MLA_V:MLA_V + 1]).T.astype(BF16)


def _pair_tables(nq):
    qi = np.concatenate([np.full((i + 1,), i, np.int32) for i in range(nq)])
    ki = np.concatenate([np.arange(i + 1, dtype=np.int32) for i in range(nq)])
    return jnp.asarray(qi), jnp.asarray(ki)


def _tile_geometry(tq):
    j = np.arange(tq)[:, None]
    i = np.arange(tq)[None, :]
    visible = (j // CHUNK) <= (i // CHUNK)
    return i, j, visible


def _bf16_split3(x):
    parts = []
    for _ in range(3):
        p = float(np.asarray(x, np.float32).astype(BF16).astype(np.float32))
        parts.append(p)
        x = x - p
    return parts


def _alibi_operands(tq):
    assert tq <= 2 * MXU_DIM
    j = np.arange(tq)
    jlo = (j % MXU_DIM).astype(np.float32)
    jhi = (j - j % MXU_DIM).astype(np.float32)
    pos = np.zeros((tq, HEAD_W), np.float32)
    cq = np.zeros((DA_HEADS, HEAD_W, tq), np.float32)
    for base in (0, DA_D):
        for r in range(3):
            pos[:, base + 2 * r] = jlo
            pos[:, base + 2 * r + 1] = jhi
    for h in range(DA_HEADS):
        parts = _bf16_split3(_alibi_slope(h) * LOG2E)
        for base in (0, DA_D):
            for r in range(3):
                cq[h, base + 2 * r, :] = parts[r]
                cq[h, base + 2 * r + 1, :] = parts[r]
    return jnp.asarray(pos, BF16), jnp.asarray(cq, BF16)


def _prompt_attn_specs(tq, wq, wk, wv):
    qt_spec = pl.BlockSpec((None, wq, tq), lambda b, t, qi, ki: (b, 0, qi[t]))
    k_spec = pl.BlockSpec((None, tq, wk), lambda b, t, qi, ki: (b, ki[t], 0))
    vt_spec = pl.BlockSpec((None, wv, tq), lambda b, t, qi, ki: (b, 0, ki[t]))
    mk_spec = pl.BlockSpec((N_META, wk), lambda b, t, qi, ki: (0, 0))
    mvt_spec = pl.BlockSpec((wv, N_META), lambda b, t, qi, ki: (0, 0))
    return qt_spec, k_spec, vt_spec, mk_spec, mvt_spec


def _diff_attn(qt, k, vt, mk, mvt, lamv, g, *, tq, lam_init):
    b, s, w = k.shape
    nq = s // tq
    qi, ki = _pair_tables(nq)
    i, j, visible = _tile_geometry(tq)
    corr = np.where(visible, np.where(j > i, -2.0 * (j - i), 0.0), NEG_BIG)
    corr = jnp.asarray(np.concatenate([corr, np.broadcast_to(i, (N_META, tq))]).astype(np.float32))
    pos, cq = _alibi_operands(tq)
    qt_spec, k_spec, vt_spec, mk_spec, mvt_spec = _prompt_attn_specs(tq, w, w, vt.shape[1])
    full = lambda a: pl.BlockSpec(a.shape, lambda b_, t, qi_, ki_: (0,) * a.ndim)
    return pl.pallas_call(
        functools.partial(_diff_attn_kernel, tq=tq, lam_init=lam_init),
        grid_spec=pltpu.PrefetchScalarGridSpec(
            num_scalar_prefetch=2,
            grid=(b, int(qi.shape[0])),
            in_specs=[qt_spec, k_spec, vt_spec, mk_spec, mvt_spec, full(pos), full(cq), full(corr),
                      full(lamv), full(g)],
            out_specs=pl.BlockSpec((None, tq, w), lambda b_, t, qi_, ki_: (b_, qi_[t], 0)),
            scratch_shapes=[pltpu.VMEM((2 * DA_HEADS, 1, tq), F32), pltpu.VMEM((2 * DA_HEADS, VT_W, tq), F32)],
        ),
        out_shape=jax.ShapeDtypeStruct((b, s, w), BF16),
        compiler_params=_cparams(2),
        name="diff_attn",
    )(qi, ki, qt, k, vt, mk, mvt, pos, cq, corr, lamv, g)


def _mla_attn(qt, k, vt, mk, mvt, *, tq):
    b, s, wq = k.shape
    nq = s // tq
    qi, ki = _pair_tables(nq)
    _, _, visible = _tile_geometry(tq)
    mask = np.concatenate([np.where(visible, 0.0, NEG_BIG), np.zeros((N_META, tq))])
    mask = jnp.asarray(mask.astype(np.float32))
    qt_spec, k_spec, vt_spec, mk_spec, mvt_spec = _prompt_attn_specs(tq, wq, wq, vt.shape[1])
    wo = MLA_HEADS * MLA_V
    return pl.pallas_call(
        _mla_attn_kernel,
        grid_spec=pltpu.PrefetchScalarGridSpec(
            num_scalar_prefetch=2,
            grid=(b, int(qi.shape[0])),
            in_specs=[qt_spec, k_spec, vt_spec, mk_spec, mvt_spec,
                      pl.BlockSpec(mask.shape, lambda b_, t, qi_, ki_: (0, 0))],
            out_specs=pl.BlockSpec((None, tq, wo), lambda b_, t, qi_, ki_: (b_, qi_[t], 0)),
            scratch_shapes=[pltpu.VMEM((MLA_HEADS, 1, tq), F32), pltpu.VMEM((MLA_HEADS, VT_W, tq), F32)],
        ),
        out_shape=jax.ShapeDtypeStruct((b, s, wo), BF16),
        compiler_params=_cparams(2),
        name="mla_attn",
    )(qi, ki, qt, k, vt, mk, mvt, mask)


def _sample_diff_kernel(q_ref, kc_hbm, vc_hbm, kn_ref, vn_ref, dc_ref, dn_ref, lamv_ref, g_ref,
                        o_ref, kbuf, vbuf, sem, m_s, l_s, acc_s, *, tk, lam_init):
    kt = pl.program_id(1)
    nkt = pl.num_programs(1)
    step = pl.program_id(0) * nkt + kt
    n_steps = pl.num_programs(0) * nkt

    def tile_copies(s, slot):
        sb = s // nkt
        rows = pl.ds((s % nkt) * tk, tk)
        cps = []
        for h in range(DA_HEADS):
            cps.append(pltpu.make_async_copy(kc_hbm.at[sb, rows, h, :], kbuf.at[slot, h], sem.at[slot, 0]))
            cps.append(pltpu.make_async_copy(vc_hbm.at[sb, rows, h, :], vbuf.at[slot, h], sem.at[slot, 1]))
        return cps

    @pl.when(step == 0)
    def _prime():
        for cp in tile_copies(0, 0):
            cp.start()

    @pl.when(step + 1 < n_steps)
    def _prefetch():
        for cp in tile_copies(step + 1, (step + 1) % 2):
            cp.start()

    slot = step % 2
    for cp in tile_copies(step, slot):
        cp.wait()

    def heads(get_k, get_v, dist, first):
        for h in range(DA_HEADS):
            q1, q2 = _split_maps(q_ref[:, h * HEAD_W:(h + 1) * HEAD_W])
            qq = jnp.concatenate([q1, q2], axis=0)
            s = _dot_nt(qq, get_k(h)) + dist * (-_alibi_slope(h) * LOG2E)
            if first:
                _softmax_seed(s, get_v(h), m_s, l_s, acc_s, h)
            else:
                _softmax_step(s, get_v(h), m_s, l_s, acc_s, h)

    cache_k = lambda h: kbuf[slot, h].astype(BF16)
    cache_v = lambda h: vbuf[slot, h].astype(BF16)

    @pl.when(kt == 0)
    def _first():
        heads(cache_k, cache_v, dc_ref[...], True)

    @pl.when(kt > 0)
    def _rest():
        heads(cache_k, cache_v, dc_ref[...], False)

    @pl.when(kt == nkt - 1)
    def _finish():
        heads(lambda h: kn_ref[:, h * HEAD_W:(h + 1) * HEAD_W], lambda h: vn_ref[:, h * HEAD_W:(h + 1) * HEAD_W],
              dn_ref[...], False)
        lam = _diff_lambda(lamv_ref[...], lam_init)
        g = g_ref[...]
        nq = q_ref.shape[0]
        for h in range(DA_HEADS):
            a = acc_s[h] / l_s[h]
            o = a[0:nq] - lam * a[nq:2 * nq]
            o_ref[:, h * HEAD_W:(h + 1) * HEAD_W] = (_rms(o, g) * (1.0 - lam_init)).astype(BF16)


def _sample_diff_attn(q, kc, vc, kn, vn, dist_c, dist_n, lamv, g, *, tk, lam_init):
    bs, nq, w = q.shape
    lc = kc.shape[1]
    assert lc % tk == 0
    full = lambda a: pl.BlockSpec(a.shape, lambda b, t: (0,) * a.ndim)
    per_stream = lambda a: pl.BlockSpec((None,) + a.shape[1:], lambda b, t: (b,) + (0,) * (a.ndim - 1))
    cache = pl.BlockSpec(memory_space=pl.ANY)
    return pl.pallas_call(
        functools.partial(_sample_diff_kernel, tk=tk, lam_init=lam_init),
        grid=(bs, lc // tk),
        in_specs=[per_stream(q), cache, cache, per_stream(kn), per_stream(vn),
                  pl.BlockSpec((None, 2 * nq, tk), lambda b, t: (t, 0, 0)),
                  full(dist_n), full(lamv), full(g)],
        out_specs=per_stream(q),
        out_shape=jax.ShapeDtypeStruct((bs, nq, w), BF16),
        scratch_shapes=[pltpu.VMEM((2, DA_HEADS, tk, HEAD_W), F32), pltpu.VMEM((2, DA_HEADS, tk, HEAD_W), F32),
                        pltpu.SemaphoreType.DMA((2, 2)),
                        pltpu.VMEM((DA_HEADS, 2 * nq, 1), F32), pltpu.VMEM((DA_HEADS, 2 * nq, 1), F32),
                        pltpu.VMEM((DA_HEADS, 2 * nq, DA_V), F32)],
        compiler_params=_cparams(2),
        name="sample_diff_attn",
    )(q, kc, vc, kn, vn, dist_c, dist_n, lamv, g)


def _sample_mla_kernel(q_ref, cc_ref, krc_ref, cn_ref, krn_ref, wuk_ref, wuv_ref, o_ref,
                       ql_s, qr_s, m_s, l_s, acc_s):
    kt = pl.program_id(1)
    nkt = pl.num_programs(1)
    nq = q_ref.shape[0]

    @pl.when(kt == 0)
    def _prep():
        for h in range(MLA_HEADS):
            qn = q_ref[:, h * MLA_W:h * MLA_W + MLA_NOPE]
            ql_s[h * nq:(h + 1) * nq, :] = _dot_nt(qn, wuk_ref[:, h * MLA_NOPE:(h + 1) * MLA_NOPE]).astype(BF16)
            qr_s[h * nq:(h + 1) * nq, :] = q_ref[:, h * MLA_W + MLA_NOPE:(h + 1) * MLA_W]

    def scores(c_ref, kr_ref):
        cb = c_ref[...].astype(BF16)
        krb = kr_ref[...].astype(BF16)
        s = _dot_nt(ql_s[...], cb) + _dot_nt(qr_s[:, 0:MLA_ROPE], krb)
        return s, cb

    @pl.when(kt == 0)
    def _first():
        s, cb = scores(cc_ref, krc_ref)
        _softmax_seed(s, cb, m_s, l_s, acc_s, 0)

    @pl.when(kt > 0)
    def _rest():
        s, cb = scores(cc_ref, krc_ref)
        _softmax_step(s, cb, m_s, l_s, acc_s, 0)

    @pl.when(kt == nkt - 1)
    def _finish():
        s, cb = scores(cn_ref, krn_ref)
        _softmax_step(s, cb, m_s, l_s, acc_s, 0)
        ol = (acc_s[0] / l_s[0]).astype(BF16)
        for h in range(MLA_HEADS):
            o_ref[:, h * MLA_V:(h + 1) * MLA_V] = _dot(
                ol[h * nq:(h + 1) * nq, :], wuv_ref[:, h * MLA_V:(h + 1) * MLA_V]).astype(BF16)


def _sample_mla_attn(q, cc, krc, cn, krn, wuk, wuv, *, tk):
    bs, nq, wq = q.shape
    lc, kvl = cc.shape[1], cc.shape[2]
    assert lc % tk == 0
    full = lambda a: pl.BlockSpec(a.shape, lambda b, t: (0,) * a.ndim)
    per_stream = lambda a: pl.BlockSpec((None,) + a.shape[1:], lambda b, t: (b,) + (0,) * (a.ndim - 1))
    rows = MLA_HEADS * nq
    return pl.pallas_call(
        _sample_mla_kernel,
        grid=(bs, lc // tk),
        in_specs=[per_stream(q),
                  pl.BlockSpec((None, tk, kvl), lambda b, t: (b, t, 0)),
                  pl.BlockSpec((None, tk, MLA_ROPE), lambda b, t: (b, t, 0)),
                  per_stream(cn), per_stream(krn), full(wuk), full(wuv)],
        out_specs=pl.BlockSpec((None, nq, MLA_HEADS * MLA_V), lambda b, t: (b, 0, 0)),
        out_shape=jax.ShapeDtypeStruct((bs, nq, MLA_HEADS * MLA_V), BF16),
        scratch_shapes=[pltpu.VMEM((rows, kvl), BF16), pltpu.VMEM((rows, LANES), BF16),
                        pltpu.VMEM((1, rows, 1), F32), pltpu.VMEM((1, rows, 1), F32),
                        pltpu.VMEM((1, rows, kvl), F32)],
        compiler_params=_cparams(2),
        name="sample_mla_attn",
    )(q, cc, krc, cn, krn, wuk, wuv)


ROUTER_ROWS = SUBLANES * (1 + N_GROUPS)


def _route(lt):
    g = [lt[i:i + 1] for i in range(N_GROUPS)]
    gmax = functools.reduce(jnp.maximum, g)
    gidx = jnp.full_like(gmax, float(N_GROUPS - 1))
    for i in range(N_GROUPS - 2, -1, -1):
        gidx = jnp.where(g[i] == gmax, float(i), gidx)
    den = functools.reduce(lambda a, b: a + b, [jnp.exp(gi - gmax) for gi in g])
    p_top = 1.0 / den
    e = []
    for j in range(EXPERTS_PER_GROUP):
        ej = lt[SUBLANES * N_GROUPS + j:SUBLANES * N_GROUPS + j + 1]
        for grp in range(N_GROUPS - 2, -1, -1):
            ej = jnp.where(gidx == float(grp), lt[SUBLANES * (grp + 1) + j:SUBLANES * (grp + 1) + j + 1], ej)
        e.append(ej)

    def first_argmax(vals):
        vmax = functools.reduce(jnp.maximum, vals)
        idx = jnp.full_like(vmax, float(len(vals) - 1))
        for i in range(len(vals) - 2, -1, -1):
            idx = jnp.where(vals[i] == vmax, float(i), idx)
        return vmax, idx

    v1, i1 = first_argmax(e)
    rest = [jnp.where(i1 == float(j), -jnp.inf, e[j]) for j in range(EXPERTS_PER_GROUP)]
    v2, i2 = first_argmax(rest)
    r = jnp.exp(v2 - v1)
    w1 = p_top / (1.0 + r)
    w2 = p_top * r / (1.0 + r)
    base = gidx * float(EXPERTS_PER_GROUP)
    return w1, w2, base + i1, base + i2


def _merge_kernel(od_ref, om_ref, x_ref, wo_ref, g2_ref, wr_ref, br_ref, hp_ref, xn_ref, rt_ref):
    nd = od_ref.shape[1]
    y = _dot(od_ref[...], wo_ref[0:nd, :]) + _dot(om_ref[...], wo_ref[nd:, :])
    hp = x_ref[...] + y
    hp_ref[...] = hp
    xn = _rms(hp, g2_ref[...])
    xn_ref[...] = _pack_halves(xn)
    lt = _dot_nt(wr_ref[...], xn.astype(BF16)) + br_ref[...]
    rows = _route(lt)
    for i, r in enumerate(rows):
        rt_ref[i:i + 1, :] = r
    rt_ref[4:8, :] = jnp.zeros((4, rt_ref.shape[1]), F32)


def _merge(od, om, x, wo, g2, wr, br, *, tm):
    m, d = x.shape
    row = lambda w: pl.BlockSpec((tm, w), lambda i: (i, 0))
    return pl.pallas_call(
        _merge_kernel,
        grid=(m // tm,),
        in_specs=[row(od.shape[1]), row(om.shape[1]), row(d), _const_spec(wo.shape),
                  _const_spec(g2.shape), _const_spec(wr.shape), _const_spec(br.shape)],
        out_specs=[row(d), row(d // 2), pl.BlockSpec((SUBLANES, tm), lambda i: (0, i))],
        out_shape=[jax.ShapeDtypeStruct((m, d), F32), jax.ShapeDtypeStruct((m, d // 2), jnp.uint32),
                   jax.ShapeDtypeStruct((SUBLANES, m), F32)],
        compiler_params=_cparams(1),
        name="merge",
    )(od, om, x, wo, g2, wr, br)


def _cast_kernel(x_ref, o_ref):
    o_ref[...] = x_ref[...].astype(BF16)


def _cast_bf16(w, *, rows):
    e, r, c = w.shape
    assert r % rows == 0
    spec = pl.BlockSpec((None, rows, c), lambda i, j: (i, j, 0))
    return pl.pallas_call(
        _cast_kernel,
        grid=(e, r // rows),
        in_specs=[spec],
        out_specs=spec,
        out_shape=jax.ShapeDtypeStruct(w.shape, BF16),
        compiler_params=_cparams(2),
        name="cast_bf16",
    )(w)


def _swiglu(xp, wg_ref, wu_ref, wd_ref):
    lo, hi = _unpack_halves(xp)
    lo = lo.astype(BF16)
    hi = hi.astype(BF16)
    w = lo.shape[1]
    g = _dot(lo, wg_ref[0:w, :]) + _dot(hi, wg_ref[w:, :])
    u = _dot(lo, wu_ref[0:w, :]) + _dot(hi, wu_ref[w:, :])
    h = (g * jax.nn.sigmoid(g) * u).astype(BF16)
    return _dot(h, wd_ref[...])


def _moe_sorted_kernel(te_ref, nu_ref, x_ref, w_ref, wg_ref, wu_ref, wd_ref, y_ref):
    i = pl.program_id(0)

    @pl.when(i < nu_ref[0])
    def _():
        y_ref[...] = _pack_halves(w_ref[...] * _swiglu(x_ref[...], wg_ref, wu_ref, wd_ref))

    @pl.when(i >= nu_ref[0])
    def _():
        y_ref[...] = jnp.zeros_like(y_ref)


def _moe_sorted(tile_expert, n_used, xs, ws, wg, wu, wd, *, tm):
    n, dh = xs.shape
    d = 2 * dh
    f = wg.shape[2]
    nt = n // tm
    return pl.pallas_call(
        _moe_sorted_kernel,
        grid_spec=pltpu.PrefetchScalarGridSpec(
            num_scalar_prefetch=2,
            grid=(nt,),
            in_specs=[pl.BlockSpec((tm, dh), lambda i, te, nu: (i, 0)),
                      pl.BlockSpec((tm, 1), lambda i, te, nu: (i, 0)),
                      pl.BlockSpec((None, d, f), lambda i, te, nu: (te[i], 0, 0)),
                      pl.BlockSpec((None, d, f), lambda i, te, nu: (te[i], 0, 0)),
                      pl.BlockSpec((None, f, d), lambda i, te, nu: (te[i], 0, 0))],
            out_specs=pl.BlockSpec((tm, dh), lambda i, te, nu: (i, 0)),
        ),
        out_shape=jax.ShapeDtypeStruct((n, dh), jnp.uint32),
        compiler_params=_cparams(1),
        name="moe_sorted",
    )(tile_expert, n_used, xs, ws, wg, wu, wd)


def _moe_dense_kernel(x_ref, hp_ref, gates_ref, wg_ref, wu_ref, wd_ref, gf_ref, o_ref, acc_s):
    e = pl.program_id(0)

    @pl.when(e == 0)
    def _():
        acc_s[...] = jnp.zeros_like(acc_s)

    lane = lax.broadcasted_iota(jnp.int32, gates_ref.shape, 1)
    gate = jnp.sum(jnp.where(lane == e, gates_ref[...], 0.0), axis=1, keepdims=True)
    acc_s[...] += gate * _swiglu(x_ref[...], wg_ref, wu_ref, wd_ref)

    @pl.when(e == pl.num_programs(0) - 1)
    def _():
        o_ref[...] = _rms(hp_ref[...] + acc_s[...], gf_ref[...])


def _moe_dense(xn, hp, gates, wg, wu, wd, gf):
    m, d = hp.shape
    ne, _, f = wg.shape
    full = lambda a: pl.BlockSpec(a.shape, lambda e: (0,) * a.ndim)
    return pl.pallas_call(
        _moe_dense_kernel,
        grid=(ne,),
        in_specs=[full(xn), full(hp), full(gates),
                  pl.BlockSpec((None, d, f), lambda e: (e, 0, 0)),
                  pl.BlockSpec((None, d, f), lambda e: (e, 0, 0)),
                  pl.BlockSpec((None, f, d), lambda e: (e, 0, 0)),
                  full(gf)],
        out_specs=full(hp),
        out_shape=jax.ShapeDtypeStruct((m, d), F32),
        scratch_shapes=[pltpu.VMEM((m, d), F32)],
        compiler_params=_cparams(1),
        name="moe_dense",
    )(xn, hp, gates, wg, wu, wd, gf)


def _combine_kernel(hp_ref, y1_ref, y2_ref, gf_ref, o_ref):
    w = y1_ref.shape[1]
    a_lo, a_hi = _unpack_halves(y1_ref[...])
    b_lo, b_hi = _unpack_halves(y2_ref[...])
    s_lo = hp_ref[:, 0:w] + (a_lo + b_lo)
    s_hi = hp_ref[:, w:] + (a_hi + b_hi)
    ms = (jnp.sum(s_lo * s_lo, axis=-1, keepdims=True) + jnp.sum(s_hi * s_hi, axis=-1, keepdims=True)) / (2 * w)
    inv = lax.rsqrt(ms + EPS)
    o_ref[:, 0:w] = s_lo * inv * gf_ref[:, 0:w]
    o_ref[:, w:] = s_hi * inv * gf_ref[:, w:]


def _combine(hp, y1, y2, gf, *, tm):
    m, d = hp.shape
    row = pl.BlockSpec((tm, d), lambda i: (i, 0))
    half = pl.BlockSpec((tm, d // 2), lambda i: (i, 0))
    return pl.pallas_call(
        _combine_kernel,
        grid=(m // tm,),
        in_specs=[row, half, half, _const_spec(gf.shape)],
        out_specs=row,
        out_shape=jax.ShapeDtypeStruct((m, d), F32),
        compiler_params=_cparams(1),
        name="combine",
    )(hp, y1, y2, gf)


def _rope_table(pos):
    half = MLA_ROPE // 2
    inv_freq = ROPE_THETA ** (-jnp.arange(half, dtype=F32) / half)
    ang = pos.astype(F32)[:, None] * inv_freq[None, :]
    c, s = jnp.cos(ang), jnp.sin(ang)
    return jnp.concatenate([c, c, -s, s], axis=1)


def _swap_halves(w):
    half = MLA_ROPE // 2
    return jnp.concatenate([w[..., half:], w[..., :half]], axis=-1)


def _values_t(v):
    n = v.shape[0]
    vt = v.reshape(n, DA_HEADS, DA_V).transpose(1, 2, 0)
    return jnp.concatenate([vt, jnp.ones((DA_HEADS, BF16_ROWS, n), v.dtype)], axis=1).reshape(DA_HEADS * VT_W, n)


def _sort_by_expert(eid, w, tm):
    t = eid.shape[1]
    flat_e = eid.reshape(-1)
    onehot = (flat_e[:, None] == jnp.arange(N_EXPERTS, dtype=jnp.int32)[None, :]).astype(jnp.int32)
    rank = jnp.sum((jnp.cumsum(onehot, axis=0) - onehot) * onehot, axis=1)
    counts = jnp.sum(onehot, axis=0)
    tiles_per = (counts + tm - 1) // tm
    tiles_end = jnp.cumsum(tiles_per)
    row_start = (tiles_end - tiles_per) * tm
    pos = row_start[flat_e] + rank
    n_tiles = (2 * t) // tm + N_EXPERTS
    slot_a = jnp.full((n_tiles * tm,), -1, jnp.int32).at[pos].set(jnp.arange(2 * t, dtype=jnp.int32),
                                                                   unique_indices=True, mode="promise_in_bounds")
    used = slot_a >= 0
    safe_a = jnp.maximum(slot_a, 0)
    sorted_tok = jnp.where(used, safe_a % t, 0)
    sorted_w = jnp.where(used, w.reshape(-1).at[safe_a].get(mode="promise_in_bounds"), 0.0)
    tile_ids = jnp.arange(n_tiles, dtype=jnp.int32)
    tile_expert = jnp.minimum(jnp.sum((tiles_end[None, :] <= tile_ids[:, None]).astype(jnp.int32), axis=1),
                              N_EXPERTS - 1)
    n_used = tiles_end[-1:].astype(jnp.int32)
    return pos.reshape(2, t), sorted_tok, sorted_w, tile_expert, n_used


def kernel(x_prompt, x_sample, cache_diff_k, cache_diff_v, cache_mla_ckv, cache_mla_kr, meta_tokens, norm1_g, w_in, diff_lam_q1, diff_lam_k1, diff_lam_q2, diff_lam_k2, diff_subln_g, mla_q_norm_g, mla_w_uq, mla_kv_norm_g, mla_w_uk, mla_w_uv, w_o, norm2_g, router_group_w, router_group_b, router_expert_w, router_expert_b, expert_w_gate, expert_w_up, expert_w_down, final_norm_g):
    depth = norm1_g.shape[0]
    assert depth == 1, "single-layer step only"
    assert MLA_HEADS == DA_HEADS and MLA_V == DA_V
    lam_init = 0.8 - 0.6 * math.exp(-0.3 * 0)
    b, s, d = x_prompt.shape
    bs, ss, _ = x_sample.shape
    past = cache_mla_kr.shape[2]
    lc = N_META + past
    c_qk = DA_HEADS * 2 * DA_D
    c_v = DA_HEADS * DA_V
    c_ql = mla_q_norm_g.shape[1]
    c_kvl = mla_kv_norm_g.shape[1]
    o5 = 2 * c_qk + c_v + c_ql + c_kvl

    win = w_in[0]
    win_ext = jnp.concatenate([win, _swap_halves(win[:, o5:])], axis=1).astype(BF16)
    wuq = mla_w_uq[0].reshape(c_ql, MLA_HEADS, MLA_NOPE + MLA_ROPE)
    wuq_n = wuq[:, :, :MLA_NOPE].reshape(c_ql, MLA_HEADS * MLA_NOPE)
    wuq_r = jnp.concatenate([wuq[:, :, MLA_NOPE:], _swap_halves(wuq[:, :, MLA_NOPE:])], axis=2)
    wuq_ext = jnp.concatenate([wuq_n, wuq_r.reshape(c_ql, MLA_HEADS * LANES)], axis=1).astype(BF16)
    wuk = mla_w_uk[0].astype(BF16)
    wuv = mla_w_uv[0].astype(BF16)
    wukv = jnp.concatenate([wuk, wuv], axis=1)
    wo = w_o[0].astype(BF16)
    wr = jnp.zeros((ROUTER_ROWS, d), F32).at[0:N_GROUPS].set(router_group_w[0].T)
    br = jnp.zeros((ROUTER_ROWS, 1), F32).at[0:N_GROUPS, 0].set(router_group_b[0])
    rew = router_expert_w[0].T.reshape(N_GROUPS, EXPERTS_PER_GROUP, d)
    reb = router_expert_b[0].reshape(N_GROUPS, EXPERTS_PER_GROUP)
    for grp in range(N_GROUPS):
        wr = wr.at[SUBLANES * (grp + 1):SUBLANES * (grp + 1) + EXPERTS_PER_GROUP].set(rew[grp])
        br = br.at[SUBLANES * (grp + 1):SUBLANES * (grp + 1) + EXPERTS_PER_GROUP, 0].set(reb[grp])
    wr = wr.astype(BF16)
    wg = _cast_bf16(expert_w_gate[0], rows=min(CAST_ROWS, expert_w_gate.shape[2]))
    wu = _cast_bf16(expert_w_up[0], rows=min(CAST_ROWS, expert_w_up.shape[2]))
    wd = _cast_bf16(expert_w_down[0], rows=min(CAST_ROWS, expert_w_down.shape[2]))
    gf = final_norm_g[None, :]
    lamv = jnp.stack([diff_lam_q1[0], diff_lam_k1[0], diff_lam_q2[0], diff_lam_k2[0]])
    subg = diff_subln_g

    dims = (c_qk, c_v, c_ql, c_kvl)
    inproj = functools.partial(_inproj, g1=norm1_g, win=win_ext, qg=mla_q_norm_g, wuq=wuq_ext,
                               kvg=mla_kv_norm_g, wukv=wukv, dims=dims)

    (_, mdk32, mdk, mdv32, mdv, _, mckv, _, mkm, mvm) = inproj(
        meta_tokens, jnp.zeros((N_META, LANES), F32), tm=N_META, tab_blocks=1)

    ts = bs * ss
    s_pos = past + jnp.arange(ss, dtype=jnp.int32)
    (sqd, s_dk, skd, s_dv, svd, sqm, sckv, skr, _, _) = inproj(
        x_sample.reshape(ts, d), _rope_table(s_pos), tm=ss, tab_blocks=1)
    q3 = lambda a: a.reshape(bs, ss, a.shape[-1])
    kpos_c = np.arange(lc) - N_META
    dist_c = np.where(kpos_c[None, :] >= 0, np.abs(past + np.arange(ss)[:, None] - kpos_c[None, :]), 0)
    dist_n = np.abs(np.arange(ss)[:, None] - np.arange(ss)[None, :])
    tk_s = lc // 2 if (lc // 2) % SUBLANES == 0 and lc % 2 == 0 else lc
    dist_c = np.tile(dist_c, (2, 1)).astype(np.float32).reshape(2 * ss, lc // tk_s, tk_s)
    dist_c = jnp.asarray(np.moveaxis(dist_c, 1, 0))
    dist_n = jnp.asarray(np.tile(dist_n, (2, 1)).astype(np.float32))
    sod = _sample_diff_attn(q3(sqd), cache_diff_k[0], cache_diff_v[0], q3(skd), q3(svd), dist_c, dist_n, lamv, subg,
                            tk=tk_s, lam_init=lam_init)
    krc = jnp.concatenate([jnp.zeros((bs, N_META, MLA_ROPE), F32), cache_mla_kr[0]], axis=1)
    som = _sample_mla_attn(q3(sqm), cache_mla_ckv[0], krc, q3(sckv), q3(skr), wuk, wuv, tk=tk_s)
    hs, xn2s, rts = _merge(sod.reshape(ts, -1), som.reshape(ts, -1), x_sample.reshape(ts, d), wo, norm2_g, wr, br,
                           tm=ts)
    eids = rts[2:4].astype(jnp.int32)
    gates = (jnp.where(eids[0][:, None] == jnp.arange(LANES)[None, :], rts[0][:, None], 0.0)
             + jnp.where(eids[1][:, None] == jnp.arange(LANES)[None, :], rts[1][:, None], 0.0))
    y_sample = _moe_dense(xn2s, hs, gates, wg, wu, wd, gf).reshape(bs, ss, d)

    tm_p = min(INPROJ_ROWS, s)
    tab_p = _rope_table(jnp.arange(s, dtype=jnp.int32))
    (pqdt, p_dk, pkd, p_dv, pvdt, pqmt, p_ckv, pkr, pkm, pvmt) = inproj(
        x_prompt.reshape(b * s, d), tab_p, tm=tm_p, tab_blocks=s // tm_p, batch=b, meta=(mdk32, mdv32, mckv))
    tq = min(ATTN_TILE, s)
    r3 = lambda a: a.reshape(b, s, a.shape[-1])
    od = _diff_attn(pqdt, r3(pkd), pvdt, mdk, _values_t(mdv), lamv, subg.T, tq=tq, lam_init=lam_init)
    om = _mla_attn(pqmt, r3(pkm), pvmt, mkm, _values_t(mvm), tq=tq)
    t = b * s
    tm_t = min(TOKEN_ROWS, t)
    hp, xn2, rt = _merge(od.reshape(t, -1), om.reshape(t, -1), x_prompt.reshape(t, d), wo, norm2_g, wr, br, tm=tm_t)
    pos, sorted_tok, sorted_w, tile_expert, n_used = _sort_by_expert(rt[2:4].astype(jnp.int32), rt[0:2], tm_t)
    rows = lambda a, idx: a.at[idx].get(mode="promise_in_bounds")
    ys = _moe_sorted(tile_expert, n_used, rows(xn2, sorted_tok), sorted_w[:, None], wg, wu, wd, tm=tm_t)
    y_prompt = _combine(hp, rows(ys, pos[0]), rows(ys, pos[1]), gf, tm=tm_t).reshape(b, s, d)

    return (y_prompt, y_sample,
            p_dk[None], p_dv[None], p_ckv[None], pkr.reshape(1, b, s, MLA_ROPE),
            s_dk.reshape(1, bs, ss, DA_HEADS, 2 * DA_D), s_dv.reshape(1, bs, ss, DA_HEADS, DA_V),
            sckv.reshape(1, bs, ss, c_kvl), skr.reshape(1, bs, ss, MLA_ROPE))
```

```python
import functools
import math

import numpy as np
import jax
import jax.numpy as jnp
from jax import lax
from jax.experimental import pallas as pl
from jax.experimental.pallas import tpu as pltpu

CHUNK = 64
N_META = 16
EPS = 1e-6
DA_HEADS = 8
DA_D = 64
DA_V = 2 * DA_D
MLA_HEADS = 8
MLA_NOPE = 128
MLA_ROPE = 64
MLA_V = 128
ROPE_THETA = 10000.0
MLA_SCALE = (MLA_NOPE + MLA_ROPE) ** -0.5
N_GROUPS = 4
EXPERTS_PER_GROUP = 4
N_EXPERTS = N_GROUPS * EXPERTS_PER_GROUP
LOG2E = math.log2(math.e)
LANES = 128
SUBLANES = 8
BF16_ROWS = 16
MXU_DIM = 256
HEAD_W = 128
MLA_W = 256
VT_W = DA_V + BF16_ROWS
NEG_BIG = -1e30
VMEM_LIMIT = 56 * 1024 * 1024
ATTN_TILE = 512
SCORE_LOOKAHEAD = 2
INPROJ_ROWS = 256
TOKEN_ROWS = 512
CAST_ROWS = 1024

BF16 = jnp.bfloat16
F32 = jnp.float32


def _dot(a, b):
    return jnp.dot(a, b, preferred_element_type=F32)


def _dot_nt(a, b):
    return lax.dot_general(a, b, (((1,), (1,)), ((), ())), preferred_element_type=F32)


def _rms(x, g):
    return x * lax.rsqrt(jnp.mean(x * x, axis=-1, keepdims=True) + EPS) * g


def _pack_halves(x):
    w = x.shape[1] // 2
    bits = lax.bitcast_convert_type(x.astype(BF16).astype(F32), jnp.uint32)
    return (bits[:, :w] >> 16) | (bits[:, w:] & jnp.uint32(0xFFFF0000))


def _unpack_halves(u):
    lo = lax.bitcast_convert_type(u << 16, F32)
    hi = lax.bitcast_convert_type(u & jnp.uint32(0xFFFF0000), F32)
    return lo, hi


def _cparams(n_axes):
    return pltpu.CompilerParams(dimension_semantics=("arbitrary",) * n_axes,
                                vmem_limit_bytes=VMEM_LIMIT)


def _const_spec(shape):
    nd = len(shape)
    return pl.BlockSpec(shape, lambda *_: (0,) * nd, pipeline_mode=pl.Buffered(1))


def _alibi_slope(h):
    return 2.0 ** (-8.0 * (h + 1) / DA_HEADS)


def _inproj_kernel(*refs, c_qk, c_v, c_ql, c_kvl, tab_blocks, prompt):
    (x_ref, tab_ref, g1_ref, win_ref, qg_ref, wuq_ref, kvg_ref, wukv_ref), refs = refs[:8], refs[8:]
    if prompt:
        (mk32_ref, mv32_ref, mckv_ref), refs = refs[:3], refs[3:]
    (qd_ref, kd32_ref, kdb_ref, vd32_ref, vdb_ref, qm_ref, ckv_ref, kr_ref, km_ref, vm_ref), refs = refs[:10], refs[10:]
    x = x_ref[...]
    tm = x.shape[0]
    if prompt:
        kbuf, vbuf, cbuf, sem, msem = refs
        i = pl.program_id(0)
        n = pl.num_programs(0)
        slot = i % 2

        def tile_copies(step, s):
            sb = step // tab_blocks
            rows = pl.ds(N_META + (step % tab_blocks) * tm, tm)
            return [pltpu.make_async_copy(kbuf.at[s], kd32_ref.at[sb, rows], sem.at[s, 0]),
                    pltpu.make_async_copy(vbuf.at[s], vd32_ref.at[sb, rows], sem.at[s, 1]),
                    pltpu.make_async_copy(cbuf.at[s], ckv_ref.at[sb, rows], sem.at[s, 2])]

        @pl.when(i >= 2)
        def _slot_free():
            for cp in tile_copies(i - 2, slot):
                cp.wait()

    xn = _rms(x, g1_ref[...]).astype(BF16)
    tab = tab_ref[...]
    o1 = c_qk
    o2 = o1 + c_qk
    o3 = o2 + c_v
    o4 = o3 + c_ql
    o5 = o4 + c_kvl
    ones = jnp.ones((BF16_ROWS, tm), BF16)

    def put_heads32(ref, buf, z):
        for h in range(DA_HEADS):
            if prompt:
                buf[slot, :, h, :] = z[:, h * HEAD_W:(h + 1) * HEAD_W]
            else:
                ref[:, h, :] = z[:, h * HEAD_W:(h + 1) * HEAD_W]

    def put_values_t(ref, z):
        for h in range(DA_HEADS):
            ref[h * VT_W:h * VT_W + DA_V, :] = z[:, h * DA_V:(h + 1) * DA_V].T.astype(BF16)
            ref[h * VT_W + DA_V:(h + 1) * VT_W, :] = ones

    zq = _dot(xn, win_ref[:, 0:o1]) * (DA_D ** -0.5 * LOG2E)
    qd_ref[...] = zq.T.astype(BF16) if prompt else zq.astype(BF16)
    zk = _dot(xn, win_ref[:, o1:o2])
    put_heads32(kd32_ref, kbuf if prompt else None, zk)
    kdb_ref[...] = zk.astype(BF16)
    zv = _dot(xn, win_ref[:, o2:o3])
    put_heads32(vd32_ref, vbuf if prompt else None, zv)
    if prompt:
        put_values_t(vdb_ref, zv)
    else:
        vdb_ref[...] = zv.astype(BF16)

    cq = _rms(_dot(xn, win_ref[:, o3:o4]), qg_ref[...]).astype(BF16)
    nq = MLA_HEADS * MLA_NOPE
    qn = _dot(cq, wuq_ref[:, 0:nq]) * (MLA_SCALE * LOG2E)
    qr = _dot(cq, wuq_ref[:, nq:2 * nq]) * (MLA_SCALE * LOG2E)
    for h in range(MLA_HEADS):
        u = qr[:, h * LANES:(h + 1) * LANES] * tab
        rot = u + pltpu.roll(u, MLA_ROPE, 1)
        nope = qn[:, h * LANES:(h + 1) * LANES]
        if prompt:
            qm_ref[h * MLA_W:h * MLA_W + LANES, :] = nope.T.astype(BF16)
            qm_ref[h * MLA_W + LANES:(h + 1) * MLA_W, :] = rot.T.astype(BF16)
        else:
            qm_ref[:, h * MLA_W:h * MLA_W + LANES] = nope.astype(BF16)
            qm_ref[:, h * MLA_W + LANES:(h + 1) * MLA_W] = rot.astype(BF16)

    ckv = _rms(_dot(xn, win_ref[:, o4:o5]), kvg_ref[...])
    if prompt:
        cbuf[slot] = ckv
    else:
        ckv_ref[...] = ckv
    ckvb = ckv.astype(BF16)
    nk = MLA_HEADS * MLA_NOPE
    kn = _dot(ckvb, wukv_ref[:, 0:nk])
    vm = _dot(ckvb, wukv_ref[:, nk:nk + MLA_HEADS * MLA_V])
    if prompt:
        put_values_t(vm_ref, vm)
    else:
        vm_ref[...] = vm.astype(BF16)

    u = _dot(xn, win_ref[:, o5:o5 + LANES]) * tab
    rot = u + pltpu.roll(u, MLA_ROPE, 1)
    kr_ref[...] = rot[:, 0:MLA_ROPE]
    lane = lax.broadcasted_iota(jnp.int32, rot.shape, 1)
    krp = jnp.where(lane < MLA_ROPE, rot, 0.0).astype(BF16)
    for h in range(MLA_HEADS):
        km_ref[:, h * MLA_W:h * MLA_W + LANES] = kn[:, h * LANES:(h + 1) * LANES].astype(BF16)
        km_ref[:, h * MLA_W + LANES:(h + 1) * MLA_W] = krp

    if prompt:
        for cp in tile_copies(i, slot):
            cp.start()

        @pl.when(i % tab_blocks == 0)
        def _meta_rows():
            head = pl.ds(0, N_META)
            sb = i // tab_blocks
            cps = [pltpu.make_async_copy(mk32_ref, kd32_ref.at[sb, head], msem.at[0]),
                   pltpu.make_async_copy(mv32_ref, vd32_ref.at[sb, head], msem.at[1]),
                   pltpu.make_async_copy(mckv_ref, ckv_ref.at[sb, head], msem.at[2])]
            for cp in cps:
                cp.start()
            for cp in cps:
                cp.wait()

        @pl.when(jnp.logical_and(i == n - 1, i >= 1))
        def _drain_previous():
            for cp in tile_copies(i - 1, 1 - slot):
                cp.wait()

        @pl.when(i == n - 1)
        def _drain_last():
            for cp in tile_copies(i, slot):
                cp.wait()


def _inproj(x, tab, g1, win, qg, wuq, kvg, wukv, *, tm, tab_blocks, dims, batch=None, meta=None):
    m, d = x.shape
    c_qk, c_v, c_ql, c_kvl = dims
    assert m % tm == 0
    prompt = batch is not None
    row = lambda w: pl.BlockSpec((tm, w), lambda i: (i, 0))
    hw = DA_HEADS * VT_W
    extra_in, extra_specs, scratch = [], [], []
    if prompt:
        seq = tab_blocks * tm
        assert m == batch * seq
        col = lambda w: pl.BlockSpec((None, w, tm), lambda i: (i // tab_blocks, 0, i % tab_blocks))
        hbm = pl.BlockSpec(memory_space=pl.ANY)
        tall = (batch, N_META + seq)
        extra_in = list(meta)
        extra_specs = [_const_spec(a.shape) for a in meta]
        scratch = [pltpu.VMEM((2, tm, DA_HEADS, HEAD_W), F32), pltpu.VMEM((2, tm, DA_HEADS, HEAD_W), F32),
                   pltpu.VMEM((2, tm, c_kvl), F32), pltpu.SemaphoreType.DMA((2, 3)), pltpu.SemaphoreType.DMA((3,))]
        out_specs = [col(c_qk), hbm, row(c_qk), hbm, col(hw), col(MLA_HEADS * MLA_W), hbm,
                     row(MLA_ROPE), row(MLA_HEADS * MLA_W), col(hw)]
        out_shape = [
            jax.ShapeDtypeStruct((batch, c_qk, seq), BF16),
            jax.ShapeDtypeStruct(tall + (DA_HEADS, HEAD_W), F32),
            jax.ShapeDtypeStruct((m, c_qk), BF16),
            jax.ShapeDtypeStruct(tall + (DA_HEADS, HEAD_W), F32),
            jax.ShapeDtypeStruct((batch, hw, seq), BF16),
            jax.ShapeDtypeStruct((batch, MLA_HEADS * MLA_W, seq), BF16),
            jax.ShapeDtypeStruct(tall + (c_kvl,), F32),
            jax.ShapeDtypeStruct((m, MLA_ROPE), F32),
            jax.ShapeDtypeStruct((m, MLA_HEADS * MLA_W), BF16),
            jax.ShapeDtypeStruct((batch, hw, seq), BF16),
        ]
    else:
        heads32 = pl.BlockSpec((tm, DA_HEADS, HEAD_W), lambda i: (i, 0, 0))
        out_specs = [row(c_qk), heads32, row(c_qk), heads32, row(c_v), row(MLA_HEADS * MLA_W), row(c_kvl),
                     row(MLA_ROPE), row(MLA_HEADS * MLA_W), row(MLA_HEADS * MLA_V)]
        out_shape = [
            jax.ShapeDtypeStruct((m, c_qk), BF16),
            jax.ShapeDtypeStruct((m, DA_HEADS, HEAD_W), F32),
            jax.ShapeDtypeStruct((m, c_qk), BF16),
            jax.ShapeDtypeStruct((m, DA_HEADS, HEAD_W), F32),
            jax.ShapeDtypeStruct((m, c_v), BF16),
            jax.ShapeDtypeStruct((m, MLA_HEADS * MLA_W), BF16),
            jax.ShapeDtypeStruct((m, c_kvl), F32),
            jax.ShapeDtypeStruct((m, MLA_ROPE), F32),
            jax.ShapeDtypeStruct((m, MLA_HEADS * MLA_W), BF16),
            jax.ShapeDtypeStruct((m, MLA_HEADS * MLA_V), BF16),
        ]
    return pl.pallas_call(
        functools.partial(_inproj_kernel, c_qk=c_qk, c_v=c_v, c_ql=c_ql, c_kvl=c_kvl, tab_blocks=tab_blocks,
                          prompt=prompt),
        grid=(m // tm,),
        in_specs=[
            row(d),
            pl.BlockSpec((tm, LANES), lambda i: (i % tab_blocks, 0)),
            _const_spec(g1.shape), _const_spec(win.shape), _const_spec(qg.shape),
            _const_spec(wuq.shape), _const_spec(kvg.shape), _const_spec(wukv.shape),
        ] + extra_specs,
        out_specs=out_specs,
        out_shape=out_shape,
        scratch_shapes=scratch,
        compiler_params=_cparams(1),
        name="inproj",
    )(x, tab, g1, win, qg, wuq, kvg, wukv, *extra_in)


def _softmax_seed(s, v, m_ref, l_ref, acc_ref, j):
    m = jnp.max(s, axis=1, keepdims=True)
    p = jnp.exp2(s - m)
    m_ref[j] = m
    l_ref[j] = jnp.sum(p, axis=1, keepdims=True)
    acc_ref[j] = _dot(p.astype(BF16), v)


def _softmax_step(s, v, m_ref, l_ref, acc_ref, j):
    m_old = m_ref[j]
    m_new = jnp.maximum(m_old, jnp.max(s, axis=1, keepdims=True))
    alpha = jnp.exp2(m_old - m_new)
    p = jnp.exp2(s - m_new)
    l_ref[j] = alpha * l_ref[j] + jnp.sum(p, axis=1, keepdims=True)
    acc_ref[j] = alpha * acc_ref[j] + _dot(p.astype(BF16), v)
    m_ref[j] = m_new


def _diff_lambda(lamv, lam_init):
    a = jnp.sum(lamv[0:1] * lamv[1:2], axis=1, keepdims=True)
    b = jnp.sum(lamv[2:3] * lamv[3:4], axis=1, keepdims=True)
    return jnp.exp(a) - jnp.exp(b) + lam_init


def _split_maps(q):
    lane = lax.broadcasted_iota(jnp.int32, q.shape, 1)
    zero = jnp.zeros_like(q)
    return jnp.where(lane < DA_D, q, zero), jnp.where(lane >= DA_D, q, zero)


def _init_state(m_ref, acc_ref):
    m_ref[...] = jnp.full(m_ref.shape, NEG_BIG, F32)
    acc_ref[...] = jnp.zeros(acc_ref.shape, F32)


def _step_t(st, shift, vt, m_ref, acc_ref, j):
    m_old = m_ref[j]
    m_new = jnp.maximum(m_old, jnp.max(st, axis=0, keepdims=True) + shift)
    p = jnp.exp2(st - (m_new - shift))
    acc_ref[j] = jnp.exp2(m_old - m_new) * acc_ref[j] + _dot(vt, p.astype(BF16))
    m_ref[j] = m_new


def _diff_attn_kernel(qi_ref, ki_ref, qt_ref, k_ref, vt_ref, mk_ref, mvt_ref, pos_ref, cq_ref, corr_ref,
                      lamv_ref, g_ref, o_ref, m_s, acc_s, *, tq, lam_init):
    t = pl.program_id(1)
    qi = qi_ref[t]
    ki = ki_ref[t]
    qrow = lax.broadcasted_iota(jnp.int32, (HEAD_W, tq), 0) < DA_D
    klane = lax.broadcasted_iota(jnp.int32, (tq, HEAD_W), 1) < DA_D

    @pl.when(ki == 0)
    def _init():
        _init_state(m_s, acc_s)

    def body(diag):
        qpos = (lax.broadcasted_iota(jnp.int32, (1, tq), 1) + (qi - ki) * tq).astype(F32)
        pos = pos_ref[...]
        mlane = lax.broadcasted_iota(jnp.int32, (N_META, HEAD_W), 1) < DA_D

        def scores(h):
            hs = slice(h * HEAD_W, (h + 1) * HEAD_W)
            qt = qt_ref[hs, :]
            cq = cq_ref[h]
            kk = k_ref[:, hs]
            corr = corr_ref[...] * (_alibi_slope(h) * LOG2E) if diag else None
            mk = mk_ref[:, hs]
            mzero = jnp.zeros_like(mk)
            out = []
            for c in range(2):
                qc = jnp.where(qrow, qt, cq) if c == 0 else jnp.where(qrow, cq, qt)
                kc = jnp.where(klane, kk, pos) if c == 0 else jnp.where(klane, pos, kk)
                if diag:
                    mc = jnp.where(mlane, mk, mzero) if c == 0 else jnp.where(mlane, mzero, mk)
                    kc = jnp.concatenate([kc, mc], axis=0)
                st = _dot(kc, qc)
                out.append(st + corr if diag else st)
            return out

        queue = [scores(h) for h in range(SCORE_LOOKAHEAD)]
        for h in range(DA_HEADS):
            cur = queue.pop(0)
            if h + SCORE_LOOKAHEAD < DA_HEADS:
                queue.append(scores(h + SCORE_LOOKAHEAD))
            shift = qpos * (-(_alibi_slope(h) * LOG2E))
            vt = vt_ref[h * VT_W:(h + 1) * VT_W, :]
            if diag:
                vt = jnp.concatenate([vt, mvt_ref[h * VT_W:(h + 1) * VT_W, :]], axis=1)
            for c in range(2):
                _step_t(cur[c], shift, vt, m_s, acc_s, 2 * h + c)

    @pl.when(ki != qi)
    def _off_diagonal():
        body(False)

    @pl.when(ki == qi)
    def _diagonal():
        body(True)
        lam = _diff_lambda(lamv_ref[...], lam_init)
        g = g_ref[...]
        for h in range(DA_HEADS):
            a0 = acc_s[2 * h]
            a1 = acc_s[2 * h + 1]
            ot = a0[0:DA_V] / a0[DA_V:DA_V + 1] - lam * (a1[0:DA_V] / a1[DA_V:DA_V + 1])
            ot = ot * lax.rsqrt(jnp.mean(ot * ot, axis=0, keepdims=True) + EPS) * g * (1.0 - lam_init)
            o_ref[:, h * HEAD_W:(h + 1) * HEAD_W] = ot.T.astype(BF16)


def _mla_attn_kernel(qi_ref, ki_ref, qt_ref, k_ref, vt_ref, mk_ref, mvt_ref, mask_ref, o_ref,
                     m_s, acc_s):
    t = pl.program_id(1)
    qi = qi_ref[t]
    ki = ki_ref[t]

    @pl.when(ki == 0)
    def _init():
        _init_state(m_s, acc_s)

    def body(diag):
        def scores(h):
            kk = k_ref[:, h * MLA_W:(h + 1) * MLA_W]
            if diag:
                kk = jnp.concatenate([kk, mk_ref[:, h * MLA_W:(h + 1) * MLA_W]], axis=0)
            st = _dot(kk, qt_ref[h * MLA_W:(h + 1) * MLA_W, :])
            return st + mask_ref[...] if diag else st

        queue = [scores(h) for h in range(SCORE_LOOKAHEAD)]
        for h in range(MLA_HEADS):
            st = queue.pop(0)
            if h + SCORE_LOOKAHEAD < MLA_HEADS:
                queue.append(scores(h + SCORE_LOOKAHEAD))
            vt = vt_ref[h * VT_W:(h + 1) * VT_W, :]
            if diag:
                vt = jnp.concatenate([vt, mvt_ref[h * VT_W:(h + 1) * VT_W, :]], axis=1)
            _step_t(st, 0.0, vt, m_s, acc_s, h)

    @pl.when(ki != qi)
    def _off_diagonal():
        body(False)

    @pl.when(ki == qi)
    def _diagonal():
        body(True)
        for h in range(MLA_HEADS):
            a = acc_s[h]
            o_ref[:, h * MLA_V:(h + 1) * MLA_V] = (a[0:MLA_V] / a[MLA_V:MLA_V + 1]).T.astype(BF16)


def _pair_tables(nq):
    qi = np.concatenate([np.full((i + 1,), i, np.int32) for i in range(nq)])
    ki = np.concatenate([np.arange(i + 1, dtype=np.int32) for i in range(nq)])
    return jnp.asarray(qi), jnp.asarray(ki)


def _tile_geometry(tq):
    j = np.arange(tq)[:, None]
    i = np.arange(tq)[None, :]
    visible = (j // CHUNK) <= (i // CHUNK)
    return i, j, visible


def _bf16_split3(x):
    parts = []
    for _ in range(3):
        p = float(np.asarray(x, np.float32).astype(BF16).astype(np.float32))
        parts.append(p)
        x = x - p
    return parts


def _alibi_operands(tq):
    assert tq <= 2 * MXU_DIM
    j = np.arange(tq)
    jlo = (j % MXU_DIM).astype(np.float32)
    jhi = (j - j % MXU_DIM).astype(np.float32)
    pos = np.zeros((tq, HEAD_W), np.float32)
    cq = np.zeros((DA_HEADS, HEAD_W, tq), np.float32)
    for base in (0, DA_D):
        for r in range(3):
            pos[:, base + 2 * r] = jlo
            pos[:, base + 2 * r + 1] = jhi
    for h in range(DA_HEADS):
        parts = _bf16_split3(_alibi_slope(h) * LOG2E)
        for base in (0, DA_D):
            for r in range(3):
                cq[h, base + 2 * r, :] = parts[r]
                cq[h, base + 2 * r + 1, :] = parts[r]
    return jnp.asarray(pos, BF16), jnp.asarray(cq, BF16)


def _prompt_attn_specs(tq, wq, wk, wv):
    qt_spec = pl.BlockSpec((None, wq, tq), lambda b, t, qi, ki: (b, 0, qi[t]))
    k_spec = pl.BlockSpec((None, tq, wk), lambda b, t, qi, ki: (b, ki[t], 0))
    vt_spec = pl.BlockSpec((None, wv, tq), lambda b, t, qi, ki: (b, 0, ki[t]))
    mk_spec = pl.BlockSpec((N_META, wk), lambda b, t, qi, ki: (0, 0))
    mvt_spec = pl.BlockSpec((wv, N_META), lambda b, t, qi, ki: (0, 0))
    return qt_spec, k_spec, vt_spec, mk_spec, mvt_spec


def _diff_attn(qt, k, vt, mk, mvt, lamv, g, *, tq, lam_init):
    b, s, w = k.shape
    nq = s // tq
    qi, ki = _pair_tables(nq)
    i, j, visible = _tile_geometry(tq)
    corr = np.where(visible, np.where(j > i, -2.0 * (j - i), 0.0), NEG_BIG)
    corr = jnp.asarray(np.concatenate([corr, np.broadcast_to(i, (N_META, tq))]).astype(np.float32))
    pos, cq = _alibi_operands(tq)
    qt_spec, k_spec, vt_spec, mk_spec, mvt_spec = _prompt_attn_specs(tq, w, w, vt.shape[1])
    full = lambda a: pl.BlockSpec(a.shape, lambda b_, t, qi_, ki_: (0,) * a.ndim)
    return pl.pallas_call(
        functools.partial(_diff_attn_kernel, tq=tq, lam_init=lam_init),
        grid_spec=pltpu.PrefetchScalarGridSpec(
            num_scalar_prefetch=2,
            grid=(b, int(qi.shape[0])),
            in_specs=[qt_spec, k_spec, vt_spec, mk_spec, mvt_spec, full(pos), full(cq), full(corr),
                      full(lamv), full(g)],
            out_specs=pl.BlockSpec((None, tq, w), lambda b_, t, qi_, ki_: (b_, qi_[t], 0)),
            scratch_shapes=[pltpu.VMEM((2 * DA_HEADS, 1, tq), F32), pltpu.VMEM((2 * DA_HEADS, VT_W, tq), F32)],
        ),
        out_shape=jax.ShapeDtypeStruct((b, s, w), BF16),
        compiler_params=_cparams(2),
        name="diff_attn",
    )(qi, ki, qt, k, vt, mk, mvt, pos, cq, corr, lamv, g)


def _mla_attn(qt, k, vt, mk, mvt, *, tq):
    b, s, wq = k.shape
    nq = s // tq
    qi, ki = _pair_tables(nq)
    _, _, visible = _tile_geometry(tq)
    mask = np.concatenate([np.where(visible, 0.0, NEG_BIG), np.zeros((N_META, tq))])
    mask = jnp.asarray(mask.astype(np.float32))
    qt_spec, k_spec, vt_spec, mk_spec, mvt_spec = _prompt_attn_specs(tq, wq, wq, vt.shape[1])
    wo = MLA_HEADS * MLA_V
    return pl.pallas_call(
        _mla_attn_kernel,
        grid_spec=pltpu.PrefetchScalarGridSpec(
            num_scalar_prefetch=2,
            grid=(b, int(qi.shape[0])),
            in_specs=[qt_spec, k_spec, vt_spec, mk_spec, mvt_spec,
                      pl.BlockSpec(mask.shape, lambda b_, t, qi_, ki_: (0, 0))],
            out_specs=pl.BlockSpec((None, tq, wo), lambda b_, t, qi_, ki_: (b_, qi_[t], 0)),
            scratch_shapes=[pltpu.VMEM((MLA_HEADS, 1, tq), F32), pltpu.VMEM((MLA_HEADS, VT_W, tq), F32)],
        ),
        out_shape=jax.ShapeDtypeStruct((b, s, wo), BF16),
        compiler_params=_cparams(2),
        name="mla_attn",
    )(qi, ki, qt, k, vt, mk, mvt, mask)


def _sample_diff_kernel(q_ref, kc_hbm, vc_hbm, kn_ref, vn_ref, dc_ref, dn_ref, lamv_ref, g_ref,
                        o_ref, kbuf, vbuf, sem, m_s, l_s, acc_s, *, tk, lam_init):
    kt = pl.program_id(1)
    nkt = pl.num_programs(1)
    step = pl.program_id(0) * nkt + kt
    n_steps = pl.num_programs(0) * nkt

    def tile_copies(s, slot):
        sb = s // nkt
        rows = pl.ds((s % nkt) * tk, tk)
        cps = []
        for h in range(DA_HEADS):
            cps.append(pltpu.make_async_copy(kc_hbm.at[sb, rows, h, :], kbuf.at[slot, h], sem.at[slot, 0]))
            cps.append(pltpu.make_async_copy(vc_hbm.at[sb, rows, h, :], vbuf.at[slot, h], sem.at[slot, 1]))
        return cps

    @pl.when(step == 0)
    def _prime():
        for cp in tile_copies(0, 0):
            cp.start()

    @pl.when(step + 1 < n_steps)
    def _prefetch():
        for cp in tile_copies(step + 1, (step + 1) % 2):
            cp.start()

    slot = step % 2
    for cp in tile_copies(step, slot):
        cp.wait()

    def heads(get_k, get_v, dist, first):
        for h in range(DA_HEADS):
            q1, q2 = _split_maps(q_ref[:, h * HEAD_W:(h + 1) * HEAD_W])
            qq = jnp.concatenate([q1, q2], axis=0)
            s = _dot_nt(qq, get_k(h)) + dist * (-_alibi_slope(h) * LOG2E)
            if first:
                _softmax_seed(s, get_v(h), m_s, l_s, acc_s, h)
            else:
                _softmax_step(s, get_v(h), m_s, l_s, acc_s, h)

    cache_k = lambda h: kbuf[slot, h].astype(BF16)
    cache_v = lambda h: vbuf[slot, h].astype(BF16)

    @pl.when(kt == 0)
    def _first():
        heads(cache_k, cache_v, dc_ref[...], True)

    @pl.when(kt > 0)
    def _rest():
        heads(cache_k, cache_v, dc_ref[...], False)

    @pl.when(kt == nkt - 1)
    def _finish():
        heads(lambda h: kn_ref[:, h * HEAD_W:(h + 1) * HEAD_W], lambda h: vn_ref[:, h * HEAD_W:(h + 1) * HEAD_W],
              dn_ref[...], False)
        lam = _diff_lambda(lamv_ref[...], lam_init)
        g = g_ref[...]
        nq = q_ref.shape[0]
        for h in range(DA_HEADS):
            a = acc_s[h] / l_s[h]
            o = a[0:nq] - lam * a[nq:2 * nq]
            o_ref[:, h * HEAD_W:(h + 1) * HEAD_W] = (_rms(o, g) * (1.0 - lam_init)).astype(BF16)


def _sample_diff_attn(q, kc, vc, kn, vn, dist_c, dist_n, lamv, g, *, tk, lam_init):
    bs, nq, w = q.shape
    lc = kc.shape[1]
    assert lc % tk == 0
    full = lambda a: pl.BlockSpec(a.shape, lambda b, t: (0,) * a.ndim)
    per_stream = lambda a: pl.BlockSpec((None,) + a.shape[1:], lambda b, t: (b,) + (0,) * (a.ndim - 1))
    cache = pl.BlockSpec(memory_space=pl.ANY)
    return pl.pallas_call(
        functools.partial(_sample_diff_kernel, tk=tk, lam_init=lam_init),
        grid=(bs, lc // tk),
        in_specs=[per_stream(q), cache, cache, per_stream(kn), per_stream(vn),
                  pl.BlockSpec((None, 2 * nq, tk), lambda b, t: (t, 0, 0)),
                  full(dist_n), full(lamv), full(g)],
        out_specs=per_stream(q),
        out_shape=jax.ShapeDtypeStruct((bs, nq, w), BF16),
        scratch_shapes=[pltpu.VMEM((2, DA_HEADS, tk, HEAD_W), F32), pltpu.VMEM((2, DA_HEADS, tk, HEAD_W), F32),
                        pltpu.SemaphoreType.DMA((2, 2)),
                        pltpu.VMEM((DA_HEADS, 2 * nq, 1), F32), pltpu.VMEM((DA_HEADS, 2 * nq, 1), F32),
                        pltpu.VMEM((DA_HEADS, 2 * nq, DA_V), F32)],
        compiler_params=_cparams(2),
        name="sample_diff_attn",
    )(q, kc, vc, kn, vn, dist_c, dist_n, lamv, g)


def _sample_mla_kernel(q_ref, cc_ref, krc_ref, cn_ref, krn_ref, wuk_ref, wuv_ref, o_ref,
                       ql_s, qr_s, m_s, l_s, acc_s):
    kt = pl.program_id(1)
    nkt = pl.num_programs(1)
    nq = q_ref.shape[0]

    @pl.when(kt == 0)
    def _prep():
        for h in range(MLA_HEADS):
            qn = q_ref[:, h * MLA_W:h * MLA_W + MLA_NOPE]
            ql_s[h * nq:(h + 1) * nq, :] = _dot_nt(qn, wuk_ref[:, h * MLA_NOPE:(h + 1) * MLA_NOPE]).astype(BF16)
            qr_s[h * nq:(h + 1) * nq, :] = q_ref[:, h * MLA_W + MLA_NOPE:(h + 1) * MLA_W]

    def scores(c_ref, kr_ref):
        cb = c_ref[...].astype(BF16)
        krb = kr_ref[...].astype(BF16)
        s = _dot_nt(ql_s[...], cb) + _dot_nt(qr_s[:, 0:MLA_ROPE], krb)
        return s, cb

    @pl.when(kt == 0)
    def _first():
        s, cb = scores(cc_ref, krc_ref)
        _softmax_seed(s, cb, m_s, l_s, acc_s, 0)

    @pl.when(kt > 0)
    def _rest():
        s, cb = scores(cc_ref, krc_ref)
        _softmax_step(s, cb, m_s, l_s, acc_s, 0)

    @pl.when(kt == nkt - 1)
    def _finish():
        s, cb = scores(cn_ref, krn_ref)
        _softmax_step(s, cb, m_s, l_s, acc_s, 0)
        ol = (acc_s[0] / l_s[0]).astype(BF16)
        for h in range(MLA_HEADS):
            o_ref[:, h * MLA_V:(h + 1) * MLA_V] = _dot(
                ol[h * nq:(h + 1) * nq, :], wuv_ref[:, h * MLA_V:(h + 1) * MLA_V]).astype(BF16)


def _sample_mla_attn(q, cc, krc, cn, krn, wuk, wuv, *, tk):
    bs, nq, wq = q.shape
    lc, kvl = cc.shape[1], cc.shape[2]
    assert lc % tk == 0
    full = lambda a: pl.BlockSpec(a.shape, lambda b, t: (0,) * a.ndim)
    per_stream = lambda a: pl.BlockSpec((None,) + a.shape[1:], lambda b, t: (b,) + (0,) * (a.ndim - 1))
    rows = MLA_HEADS * nq
    return pl.pallas_call(
        _sample_mla_kernel,
        grid=(bs, lc // tk),
        in_specs=[per_stream(q),
                  pl.BlockSpec((None, tk, kvl), lambda b, t: (b, t, 0)),
                  pl.BlockSpec((None, tk, MLA_ROPE), lambda b, t: (b, t, 0)),
                  per_stream(cn), per_stream(krn), full(wuk), full(wuv)],
        out_specs=pl.BlockSpec((None, nq, MLA_HEADS * MLA_V), lambda b, t: (b, 0, 0)),
        out_shape=jax.ShapeDtypeStruct((bs, nq, MLA_HEADS * MLA_V), BF16),
        scratch_shapes=[pltpu.VMEM((rows, kvl), BF16), pltpu.VMEM((rows, LANES), BF16),
                        pltpu.VMEM((1, rows, 1), F32), pltpu.VMEM((1, rows, 1), F32),
                        pltpu.VMEM((1, rows, kvl), F32)],
        compiler_params=_cparams(2),
        name="sample_mla_attn",
    )(q, cc, krc, cn, krn, wuk, wuv)


ROUTER_ROWS = SUBLANES * (1 + N_GROUPS)


def _route(lt):
    g = [lt[i:i + 1] for i in range(N_GROUPS)]
    gmax = functools.reduce(jnp.maximum, g)
    gidx = jnp.full_like(gmax, float(N_GROUPS - 1))
    for i in range(N_GROUPS - 2, -1, -1):
        gidx = jnp.where(g[i] == gmax, float(i), gidx)
    den = functools.reduce(lambda a, b: a + b, [jnp.exp(gi - gmax) for gi in g])
    p_top = 1.0 / den
    e = []
    for j in range(EXPERTS_PER_GROUP):
        ej = lt[SUBLANES * N_GROUPS + j:SUBLANES * N_GROUPS + j + 1]
        for grp in range(N_GROUPS - 2, -1, -1):
            ej = jnp.where(gidx == float(grp), lt[SUBLANES * (grp + 1) + j:SUBLANES * (grp + 1) + j + 1], ej)
        e.append(ej)

    def first_argmax(vals):
        vmax = functools.reduce(jnp.maximum, vals)
        idx = jnp.full_like(vmax, float(len(vals) - 1))
        for i in range(len(vals) - 2, -1, -1):
            idx = jnp.where(vals[i] == vmax, float(i), idx)
        return vmax, idx

    v1, i1 = first_argmax(e)
    rest = [jnp.where(i1 == float(j), -jnp.inf, e[j]) for j in range(EXPERTS_PER_GROUP)]
    v2, i2 = first_argmax(rest)
    r = jnp.exp(v2 - v1)
    w1 = p_top / (1.0 + r)
    w2 = p_top * r / (1.0 + r)
    base = gidx * float(EXPERTS_PER_GROUP)
    return w1, w2, base + i1, base + i2


def _merge_kernel(od_ref, om_ref, x_ref, wo_ref, g2_ref, wr_ref, br_ref, hp_ref, xn_ref, rt_ref):
    nd = od_ref.shape[1]
    y = _dot(od_ref[...], wo_ref[0:nd, :]) + _dot(om_ref[...], wo_ref[nd:, :])
    hp = x_ref[...] + y
    hp_ref[...] = hp
    xn = _rms(hp, g2_ref[...])
    xn_ref[...] = _pack_halves(xn)
    lt = _dot_nt(wr_ref[...], xn.astype(BF16)) + br_ref[...]
    rows = _route(lt)
    for i, r in enumerate(rows):
        rt_ref[i:i + 1, :] = r
    rt_ref[4:8, :] = jnp.zeros((4, rt_ref.shape[1]), F32)


def _merge(od, om, x, wo, g2, wr, br, *, tm):
    m, d = x.shape
    row = lambda w: pl.BlockSpec((tm, w), lambda i: (i, 0))
    return pl.pallas_call(
        _merge_kernel,
        grid=(m // tm,),
        in_specs=[row(od.shape[1]), row(om.shape[1]), row(d), _const_spec(wo.shape),
                  _const_spec(g2.shape), _const_spec(wr.shape), _const_spec(br.shape)],
        out_specs=[row(d), row(d // 2), pl.BlockSpec((SUBLANES, tm), lambda i: (0, i))],
        out_shape=[jax.ShapeDtypeStruct((m, d), F32), jax.ShapeDtypeStruct((m, d // 2), jnp.uint32),
                   jax.ShapeDtypeStruct((SUBLANES, m), F32)],
        compiler_params=_cparams(1),
        name="merge",
    )(od, om, x, wo, g2, wr, br)


def _cast_kernel(x_ref, o_ref):
    o_ref[...] = x_ref[...].astype(BF16)


def _cast_bf16(w, *, rows):
    e, r, c = w.shape
    assert r % rows == 0
    spec = pl.BlockSpec((None, rows, c), lambda i, j: (i, j, 0))
    return pl.pallas_call(
        _cast_kernel,
        grid=(e, r // rows),
        in_specs=[spec],
        out_specs=spec,
        out_shape=jax.ShapeDtypeStruct(w.shape, BF16),
        compiler_params=_cparams(2),
        name="cast_bf16",
    )(w)


def _swiglu(xp, wg_ref, wu_ref, wd_ref):
    lo, hi = _unpack_halves(xp)
    lo = lo.astype(BF16)
    hi = hi.astype(BF16)
    w = lo.shape[1]
    g = _dot(lo, wg_ref[0:w, :]) + _dot(hi, wg_ref[w:, :])
    u = _dot(lo, wu_ref[0:w, :]) + _dot(hi, wu_ref[w:, :])
    h = (g * jax.nn.sigmoid(g) * u).astype(BF16)
    return _dot(h, wd_ref[...])


def _moe_sorted_kernel(te_ref, nu_ref, x_ref, w_ref, wg_ref, wu_ref, wd_ref, *rest, out_first, x_first, x_tiles):
    y_ref = rest[-1]
    g = pl.program_id(0) + out_first
    live = jnp.logical_and(g < nu_ref[0], jnp.logical_and(g >= x_first, g < x_first + x_tiles))

    @pl.when(live)
    def _():
        y_ref[...] = _pack_halves(w_ref[...] * _swiglu(x_ref[...], wg_ref, wu_ref, wd_ref))

    @pl.when(jnp.logical_not(live))
    def _():
        y_ref[...] = jnp.zeros_like(y_ref)


def _moe_sorted(tile_expert, n_used, xs, ws, wg, wu, wd, *, tm, x_first, y_prev=None):
    n_x, dh = xs.shape
    d = 2 * dh
    f = wg.shape[2]
    x_tiles = n_x // tm
    n_tiles = ws.shape[0] // tm
    out_first = 0 if y_prev is None else x_first
    grid_tiles = n_tiles if y_prev is None else x_tiles
    local = lambda i: jnp.clip(i + out_first - x_first, 0, x_tiles - 1)
    in_specs = [pl.BlockSpec((tm, dh), lambda i, te, nu: (local(i), 0)),
                pl.BlockSpec((tm, 1), lambda i, te, nu: (i + out_first, 0)),
                pl.BlockSpec((None, d, f), lambda i, te, nu: (te[local(i) + x_first], 0, 0)),
                pl.BlockSpec((None, d, f), lambda i, te, nu: (te[local(i) + x_first], 0, 0)),
                pl.BlockSpec((None, f, d), lambda i, te, nu: (te[local(i) + x_first], 0, 0))]
    args = [tile_expert, n_used, xs, ws, wg, wu, wd]
    aliases = {}
    if y_prev is not None:
        in_specs.append(pl.BlockSpec(memory_space=pl.ANY))
        args.append(y_prev)
        aliases = {len(args) - 1: 0}
    return pl.pallas_call(
        functools.partial(_moe_sorted_kernel, out_first=out_first, x_first=x_first, x_tiles=x_tiles),
        grid_spec=pltpu.PrefetchScalarGridSpec(
            num_scalar_prefetch=2,
            grid=(grid_tiles,),
            in_specs=in_specs,
            out_specs=pl.BlockSpec((tm, dh), lambda i, te, nu: (i + out_first, 0)),
        ),
        out_shape=jax.ShapeDtypeStruct((n_tiles * tm, dh), jnp.uint32),
        input_output_aliases=aliases,
        compiler_params=_cparams(1),
        name="moe_sorted",
    )(*args)


def _moe_dense_kernel(x_ref, hp_ref, gates_ref, wg_ref, wu_ref, wd_ref, gf_ref, o_ref, acc_s):
    e = pl.program_id(0)

    @pl.when(e == 0)
    def _():
        acc_s[...] = jnp.zeros_like(acc_s)

    lane = lax.broadcasted_iota(jnp.int32, gates_ref.shape, 1)
    gate = jnp.sum(jnp.where(lane == e, gates_ref[...], 0.0), axis=1, keepdims=True)
    acc_s[...] += gate * _swiglu(x_ref[...], wg_ref, wu_ref, wd_ref)

    @pl.when(e == pl.num_programs(0) - 1)
    def _():
        o_ref[...] = _rms(hp_ref[...] + acc_s[...], gf_ref[...])


def _moe_dense(xn, hp, gates, wg, wu, wd, gf):
    m, d = hp.shape
    ne, _, f = wg.shape
    full = lambda a: pl.BlockSpec(a.shape, lambda e: (0,) * a.ndim)
    return pl.pallas_call(
        _moe_dense_kernel,
        grid=(ne,),
        in_specs=[full(xn), full(hp), full(gates),
                  pl.BlockSpec((None, d, f), lambda e: (e, 0, 0)),
                  pl.BlockSpec((None, d, f), lambda e: (e, 0, 0)),
                  pl.BlockSpec((None, f, d), lambda e: (e, 0, 0)),
                  full(gf)],
        out_specs=full(hp),
        out_shape=jax.ShapeDtypeStruct((m, d), F32),
        scratch_shapes=[pltpu.VMEM((m, d), F32)],
        compiler_params=_cparams(1),
        name="moe_dense",
    )(xn, hp, gates, wg, wu, wd, gf)


def _combine_kernel(hp_ref, y1_ref, y2_ref, gf_ref, o_ref):
    w = y1_ref.shape[1]
    a_lo, a_hi = _unpack_halves(y1_ref[...])
    b_lo, b_hi = _unpack_halves(y2_ref[...])
    s_lo = hp_ref[:, 0:w] + (a_lo + b_lo)
    s_hi = hp_ref[:, w:] + (a_hi + b_hi)
    ms = (jnp.sum(s_lo * s_lo, axis=-1, keepdims=True) + jnp.sum(s_hi * s_hi, axis=-1, keepdims=True)) / (2 * w)
    inv = lax.rsqrt(ms + EPS)
    o_ref[:, 0:w] = s_lo * inv * gf_ref[:, 0:w]
    o_ref[:, w:] = s_hi * inv * gf_ref[:, w:]


def _combine(hp, y1, y2, gf, *, tm):
    m, d = hp.shape
    row = pl.BlockSpec((tm, d), lambda i: (i, 0))
    half = pl.BlockSpec((tm, d // 2), lambda i: (i, 0))
    return pl.pallas_call(
        _combine_kernel,
        grid=(m // tm,),
        in_specs=[row, half, half, _const_spec(gf.shape)],
        out_specs=row,
        out_shape=jax.ShapeDtypeStruct((m, d), F32),
        compiler_params=_cparams(1),
        name="combine",
    )(hp, y1, y2, gf)


def _rope_table(pos):
    half = MLA_ROPE // 2
    inv_freq = ROPE_THETA ** (-jnp.arange(half, dtype=F32) / half)
    ang = pos.astype(F32)[:, None] * inv_freq[None, :]
    c, s = jnp.cos(ang), jnp.sin(ang)
    return jnp.concatenate([c, c, -s, s], axis=1)


def _swap_halves(w):
    half = MLA_ROPE // 2
    return jnp.concatenate([w[..., half:], w[..., :half]], axis=-1)


def _values_t(v):
    n = v.shape[0]
    vt = v.reshape(n, DA_HEADS, DA_V).transpose(1, 2, 0)
    return jnp.concatenate([vt, jnp.ones((DA_HEADS, BF16_ROWS, n), v.dtype)], axis=1).reshape(DA_HEADS * VT_W, n)


def _sort_by_expert(eid, w, tm):
    t = eid.shape[1]
    flat_e = eid.reshape(-1)
    onehot = (flat_e[:, None] == jnp.arange(N_EXPERTS, dtype=jnp.int32)[None, :]).astype(jnp.int32)
    rank = jnp.sum((jnp.cumsum(onehot, axis=0) - onehot) * onehot, axis=1)
    counts = jnp.sum(onehot, axis=0)
    tiles_per = (counts + tm - 1) // tm
    tiles_end = jnp.cumsum(tiles_per)
    row_start = (tiles_end - tiles_per) * tm
    pos = row_start[flat_e] + rank
    n_tiles = (2 * t) // tm + N_EXPERTS
    slot_a = jnp.full((n_tiles * tm,), -1, jnp.int32).at[pos].set(jnp.arange(2 * t, dtype=jnp.int32),
                                                                   unique_indices=True, mode="promise_in_bounds")
    used = slot_a >= 0
    safe_a = jnp.maximum(slot_a, 0)
    sorted_tok = jnp.where(used, safe_a % t, 0)
    sorted_w = jnp.where(used, w.reshape(-1).at[safe_a].get(mode="promise_in_bounds"), 0.0)
    tile_ids = jnp.arange(n_tiles, dtype=jnp.int32)
    tile_expert = jnp.minimum(jnp.sum((tiles_end[None, :] <= tile_ids[:, None]).astype(jnp.int32), axis=1),
                              N_EXPERTS - 1)
    n_used = tiles_end[-1:].astype(jnp.int32)
    return pos.reshape(2, t), sorted_tok, sorted_w, tile_expert, n_used


def kernel(x_prompt, x_sample, cache_diff_k, cache_diff_v, cache_mla_ckv, cache_mla_kr, meta_tokens, norm1_g, w_in, diff_lam_q1, diff_lam_k1, diff_lam_q2, diff_lam_k2, diff_subln_g, mla_q_norm_g, mla_w_uq, mla_kv_norm_g, mla_w_uk, mla_w_uv, w_o, norm2_g, router_group_w, router_group_b, router_expert_w, router_expert_b, expert_w_gate, expert_w_up, expert_w_down, final_norm_g):
    depth = norm1_g.shape[0]
    assert depth == 1, "single-layer step only"
    assert MLA_HEADS == DA_HEADS and MLA_V == DA_V
    lam_init = 0.8 - 0.6 * math.exp(-0.3 * 0)
    b, s, d = x_prompt.shape
    bs, ss, _ = x_sample.shape
    past = cache_mla_kr.shape[2]
    lc = N_META + past
    c_qk = DA_HEADS * 2 * DA_D
    c_v = DA_HEADS * DA_V
    c_ql = mla_q_norm_g.shape[1]
    c_kvl = mla_kv_norm_g.shape[1]
    o5 = 2 * c_qk + c_v + c_ql + c_kvl

    win = w_in[0]
    win_ext = jnp.concatenate([win, _swap_halves(win[:, o5:])], axis=1).astype(BF16)
    wuq = mla_w_uq[0].reshape(c_ql, MLA_HEADS, MLA_NOPE + MLA_ROPE)
    wuq_n = wuq[:, :, :MLA_NOPE].reshape(c_ql, MLA_HEADS * MLA_NOPE)
    wuq_r = jnp.concatenate([wuq[:, :, MLA_NOPE:], _swap_halves(wuq[:, :, MLA_NOPE:])], axis=2)
    wuq_ext = jnp.concatenate([wuq_n, wuq_r.reshape(c_ql, MLA_HEADS * LANES)], axis=1).astype(BF16)
    wuk = mla_w_uk[0].astype(BF16)
    wuv = mla_w_uv[0].astype(BF16)
    wukv = jnp.concatenate([wuk, wuv], axis=1)
    wo = w_o[0].astype(BF16)
    wr = jnp.zeros((ROUTER_ROWS, d), F32).at[0:N_GROUPS].set(router_group_w[0].T)
    br = jnp.zeros((ROUTER_ROWS, 1), F32).at[0:N_GROUPS, 0].set(router_group_b[0])
    rew = router_expert_w[0].T.reshape(N_GROUPS, EXPERTS_PER_GROUP, d)
    reb = router_expert_b[0].reshape(N_GROUPS, EXPERTS_PER_GROUP)
    for grp in range(N_GROUPS):
        wr = wr.at[SUBLANES * (grp + 1):SUBLANES * (grp + 1) + EXPERTS_PER_GROUP].set(rew[grp])
        br = br.at[SUBLANES * (grp + 1):SUBLANES * (grp + 1) + EXPERTS_PER_GROUP, 0].set(reb[grp])
    wr = wr.astype(BF16)
    wg = _cast_bf16(expert_w_gate[0], rows=min(CAST_ROWS, expert_w_gate.shape[2]))
    wu = _cast_bf16(expert_w_up[0], rows=min(CAST_ROWS, expert_w_up.shape[2]))
    wd = _cast_bf16(expert_w_down[0], rows=min(CAST_ROWS, expert_w_down.shape[2]))
    gf = final_norm_g[None, :]
    lamv = jnp.stack([diff_lam_q1[0], diff_lam_k1[0], diff_lam_q2[0], diff_lam_k2[0]])
    subg = diff_subln_g

    dims = (c_qk, c_v, c_ql, c_kvl)
    inproj = functools.partial(_inproj, g1=norm1_g, win=win_ext, qg=mla_q_norm_g, wuq=wuq_ext,
                               kvg=mla_kv_norm_g, wukv=wukv, dims=dims)

    (_, mdk32, mdk, mdv32, mdv, _, mckv, _, mkm, mvm) = inproj(
        meta_tokens, jnp.zeros((N_META, LANES), F32), tm=N_META, tab_blocks=1)

    ts = bs * ss
    s_pos = past + jnp.arange(ss, dtype=jnp.int32)
    (sqd, s_dk, skd, s_dv, svd, sqm, sckv, skr, _, _) = inproj(
        x_sample.reshape(ts, d), _rope_table(s_pos), tm=ss, tab_blocks=1)
    q3 = lambda a: a.reshape(bs, ss, a.shape[-1])
    kpos_c = np.arange(lc) - N_META
    dist_c = np.where(kpos_c[None, :] >= 0, np.abs(past + np.arange(ss)[:, None] - kpos_c[None, :]), 0)
    dist_n = np.abs(np.arange(ss)[:, None] - np.arange(ss)[None, :])
    tk_s = lc // 2 if (lc // 2) % SUBLANES == 0 and lc % 2 == 0 else lc
    dist_c = np.tile(dist_c, (2, 1)).astype(np.float32).reshape(2 * ss, lc // tk_s, tk_s)
    dist_c = jnp.asarray(np.moveaxis(dist_c, 1, 0))
    dist_n = jnp.asarray(np.tile(dist_n, (2, 1)).astype(np.float32))
    sod = _sample_diff_attn(q3(sqd), cache_diff_k[0], cache_diff_v[0], q3(skd), q3(svd), dist_c, dist_n, lamv, subg,
                            tk=tk_s, lam_init=lam_init)
    krc = jnp.concatenate([jnp.zeros((bs, N_META, MLA_ROPE), F32), cache_mla_kr[0]], axis=1)
    som = _sample_mla_attn(q3(sqm), cache_mla_ckv[0], krc, q3(sckv), q3(skr), wuk, wuv, tk=tk_s)
    hs, xn2s, rts = _merge(sod.reshape(ts, -1), som.reshape(ts, -1), x_sample.reshape(ts, d), wo, norm2_g, wr, br,
                           tm=ts)
    eids = rts[2:4].astype(jnp.int32)
    gates = (jnp.where(eids[0][:, None] == jnp.arange(LANES)[None, :], rts[0][:, None], 0.0)
             + jnp.where(eids[1][:, None] == jnp.arange(LANES)[None, :], rts[1][:, None], 0.0))
    y_sample = _moe_dense(xn2s, hs, gates, wg, wu, wd, gf).reshape(bs, ss, d)

    tm_p = min(INPROJ_ROWS, s)
    tab_p = _rope_table(jnp.arange(s, dtype=jnp.int32))
    (pqdt, p_dk, pkd, p_dv, pvdt, pqmt, p_ckv, pkr, pkm, pvmt) = inproj(
        x_prompt.reshape(b * s, d), tab_p, tm=tm_p, tab_blocks=s // tm_p, batch=b, meta=(mdk32, mdv32, mckv))
    tq = min(ATTN_TILE, s)
    r3 = lambda a: a.reshape(b, s, a.shape[-1])
    od = _diff_attn(pqdt, r3(pkd), pvdt, mdk, _values_t(mdv), lamv, subg.T, tq=tq, lam_init=lam_init)
    om = _mla_attn(pqmt, r3(pkm), pvmt, mkm, _values_t(mvm), tq=tq)
    t = b * s
    tm_t = min(TOKEN_ROWS, t)
    hp, xn2, rt = _merge(od.reshape(t, -1), om.reshape(t, -1), x_prompt.reshape(t, d), wo, norm2_g, wr, br, tm=tm_t)
    pos, sorted_tok, sorted_w, tile_expert, n_used = _sort_by_expert(rt[2:4].astype(jnp.int32), rt[0:2], tm_t)
    rows = lambda a, idx: a.at[idx].get(mode="promise_in_bounds")
    n_tiles = sorted_tok.shape[0] // tm_t
    cut = (n_tiles // 2) * tm_t
    sw = sorted_w[:, None]
    ys = _moe_sorted(tile_expert, n_used, rows(xn2, sorted_tok[:cut]), sw, wg, wu, wd, tm=tm_t, x_first=0)
    if cut < n_tiles * tm_t:
        ys = _moe_sorted(tile_expert, n_used, rows(xn2, sorted_tok[cut:]), sw, wg, wu, wd, tm=tm_t,
                         x_first=cut // tm_t, y_prev=ys)
    y_prompt = _combine(hp, rows(ys, pos[0]), rows(ys, pos[1]), gf, tm=tm_t).reshape(b, s, d)

    return (y_prompt, y_sample,
            p_dk[None], p_dv[None], p_ckv[None], pkr.reshape(1, b, s, MLA_ROPE),
            s_dk.reshape(1, bs, ss, DA_HEADS, 2 * DA_D), s_dv.reshape(1, bs, ss, DA_HEADS, DA_V),
            sckv.reshape(1, bs, ss, c_kvl), skr.reshape(1, bs, ss, MLA_ROPE))
```

```python
import functools
import math

import numpy as np
import jax
import jax.numpy as jnp
from jax import lax
from jax.experimental import pallas as pl
from jax.experimental.pallas import tpu as pltpu

CHUNK = 64
N_META = 16
EPS = 1e-6
DA_HEADS = 8
DA_D = 64
DA_V = 2 * DA_D
MLA_HEADS = 8
MLA_NOPE = 128
MLA_ROPE = 64
MLA_V = 128
ROPE_THETA = 10000.0
MLA_SCALE = (MLA_NOPE + MLA_ROPE) ** -0.5
N_GROUPS = 4
EXPERTS_PER_GROUP = 4
N_EXPERTS = N_GROUPS * EXPERTS_PER_GROUP
LOG2E = math.log2(math.e)
LANES = 128
SUBLANES = 8
BF16_ROWS = 16
MXU_DIM = 256
HEAD_W = 128
MLA_W = 256
VT_W = DA_V + BF16_ROWS
NEG_BIG = -1e30
VMEM_LIMIT = 56 * 1024 * 1024
ATTN_TILE = 512
SCORE_LOOKAHEAD = 2
INPROJ_ROWS = 256
TOKEN_ROWS = 512
CAST_ROWS = 1024

BF16 = jnp.bfloat16
F32 = jnp.float32


def _dot(a, b):
    return jnp.dot(a, b, preferred_element_type=F32)


def _dot_nt(a, b):
    return lax.dot_general(a, b, (((1,), (1,)), ((), ())), preferred_element_type=F32)


def _rms(x, g):
    return x * lax.rsqrt(jnp.mean(x * x, axis=-1, keepdims=True) + EPS) * g


def _pack_halves(x):
    w = x.shape[1] // 2
    bits = lax.bitcast_convert_type(x.astype(BF16).astype(F32), jnp.uint32)
    return (bits[:, :w] >> 16) | (bits[:, w:] & jnp.uint32(0xFFFF0000))


def _unpack_halves(u):
    lo = lax.bitcast_convert_type(u << 16, F32)
    hi = lax.bitcast_convert_type(u & jnp.uint32(0xFFFF0000), F32)
    return lo, hi


def _cparams(n_axes):
    return pltpu.CompilerParams(dimension_semantics=("arbitrary",) * n_axes,
                                vmem_limit_bytes=VMEM_LIMIT)


def _const_spec(shape):
    nd = len(shape)
    return pl.BlockSpec(shape, lambda *_: (0,) * nd, pipeline_mode=pl.Buffered(1))


def _alibi_slope(h):
    return 2.0 ** (-8.0 * (h + 1) / DA_HEADS)


def _inproj_kernel(*refs, c_qk, c_v, c_ql, c_kvl, tab_blocks, prompt):
    (x_ref, tab_ref, g1_ref, win_ref, qg_ref, wuq_ref, kvg_ref, wukv_ref), refs = refs[:8], refs[8:]
    if prompt:
        (mk32_ref, mv32_ref, mckv_ref), refs = refs[:3], refs[3:]
    (qd_ref, kd32_ref, kdb_ref, vd32_ref, vdb_ref, qm_ref, ckv_ref, kr_ref, km_ref, vm_ref), refs = refs[:10], refs[10:]
    x = x_ref[...]
    tm = x.shape[0]
    if prompt:
        kbuf, vbuf, cbuf, sem, msem = refs
        i = pl.program_id(0)
        n = pl.num_programs(0)
        slot = i % 2

        def tile_copies(step, s):
            sb = step // tab_blocks
            rows = pl.ds(N_META + (step % tab_blocks) * tm, tm)
            return [pltpu.make_async_copy(kbuf.at[s], kd32_ref.at[sb, rows], sem.at[s, 0]),
                    pltpu.make_async_copy(vbuf.at[s], vd32_ref.at[sb, rows], sem.at[s, 1]),
                    pltpu.make_async_copy(cbuf.at[s], ckv_ref.at[sb, rows], sem.at[s, 2])]

        @pl.when(i >= 2)
        def _slot_free():
            for cp in tile_copies(i - 2, slot):
                cp.wait()

    xn = _rms(x, g1_ref[...]).astype(BF16)
    tab = tab_ref[...]
    o1 = c_qk
    o2 = o1 + c_qk
    o3 = o2 + c_v
    o4 = o3 + c_ql
    o5 = o4 + c_kvl
    ones = jnp.ones((BF16_ROWS, tm), BF16)

    def put_heads32(ref, buf, z):
        for h in range(DA_HEADS):
            if prompt:
                buf[slot, :, h, :] = z[:, h * HEAD_W:(h + 1) * HEAD_W]
            else:
                ref[:, h, :] = z[:, h * HEAD_W:(h + 1) * HEAD_W]

    def put_values_t(ref, z):
        for h in range(DA_HEADS):
            ref[h * VT_W:h * VT_W + DA_V, :] = z[:, h * DA_V:(h + 1) * DA_V].T.astype(BF16)
            ref[h * VT_W + DA_V:(h + 1) * VT_W, :] = ones

    zq = _dot(xn, win_ref[:, 0:o1]) * (DA_D ** -0.5 * LOG2E)
    qd_ref[...] = zq.T.astype(BF16) if prompt else zq.astype(BF16)
    zk = _dot(xn, win_ref[:, o1:o2])
    put_heads32(kd32_ref, kbuf if prompt else None, zk)
    kdb_ref[...] = zk.astype(BF16)
    zv = _dot(xn, win_ref[:, o2:o3])
    put_heads32(vd32_ref, vbuf if prompt else None, zv)
    if prompt:
        put_values_t(vdb_ref, zv)
    else:
        vdb_ref[...] = zv.astype(BF16)

    cq = _rms(_dot(xn, win_ref[:, o3:o4]), qg_ref[...]).astype(BF16)
    nq = MLA_HEADS * MLA_NOPE
    qn = _dot(cq, wuq_ref[:, 0:nq]) * (MLA_SCALE * LOG2E)
    qr = _dot(cq, wuq_ref[:, nq:2 * nq]) * (MLA_SCALE * LOG2E)
    for h in range(MLA_HEADS):
        u = qr[:, h * LANES:(h + 1) * LANES] * tab
        rot = u + pltpu.roll(u, MLA_ROPE, 1)
        nope = qn[:, h * LANES:(h + 1) * LANES]
        if prompt:
            qm_ref[h * MLA_W:h * MLA_W + LANES, :] = nope.T.astype(BF16)
            qm_ref[h * MLA_W + LANES:(h + 1) * MLA_W, :] = rot.T.astype(BF16)
        else:
            qm_ref[:, h * MLA_W:h * MLA_W + LANES] = nope.astype(BF16)
            qm_ref[:, h * MLA_W + LANES:(h + 1) * MLA_W] = rot.astype(BF16)

    ckv = _rms(_dot(xn, win_ref[:, o4:o5]), kvg_ref[...])
    if prompt:
        cbuf[slot] = ckv
    else:
        ckv_ref[...] = ckv
    ckvb = ckv.astype(BF16)
    nk = MLA_HEADS * MLA_NOPE
    kn = _dot(ckvb, wukv_ref[:, 0:nk])
    vm = _dot(ckvb, wukv_ref[:, nk:nk + MLA_HEADS * MLA_V])
    if prompt:
        put_values_t(vm_ref, vm)
    else:
        vm_ref[...] = vm.astype(BF16)

    u = _dot(xn, win_ref[:, o5:o5 + LANES]) * tab
    rot = u + pltpu.roll(u, MLA_ROPE, 1)
    kr_ref[...] = rot[:, 0:MLA_ROPE]
    lane = lax.broadcasted_iota(jnp.int32, rot.shape, 1)
    krp = jnp.where(lane < MLA_ROPE, rot, 0.0).astype(BF16)
    for h in range(MLA_HEADS):
        km_ref[:, h * MLA_W:h * MLA_W + LANES] = kn[:, h * LANES:(h + 1) * LANES].astype(BF16)
        km_ref[:, h * MLA_W + LANES:(h + 1) * MLA_W] = krp

    if prompt:
        for cp in tile_copies(i, slot):
            cp.start()

        @pl.when(i % tab_blocks == 0)
        def _meta_rows():
            head = pl.ds(0, N_META)
            sb = i // tab_blocks
            cps = [pltpu.make_async_copy(mk32_ref, kd32_ref.at[sb, head], msem.at[0]),
                   pltpu.make_async_copy(mv32_ref, vd32_ref.at[sb, head], msem.at[1]),
                   pltpu.make_async_copy(mckv_ref, ckv_ref.at[sb, head], msem.at[2])]
            for cp in cps:
                cp.start()
            for cp in cps:
                cp.wait()

        @pl.when(jnp.logical_and(i == n - 1, i >= 1))
        def _drain_previous():
            for cp in tile_copies(i - 1, 1 - slot):
                cp.wait()

        @pl.when(i == n - 1)
        def _drain_last():
            for cp in tile_copies(i, slot):
                cp.wait()


def _inproj(x, tab, g1, win, qg, wuq, kvg, wukv, *, tm, tab_blocks, dims, batch=None, meta=None):
    m, d = x.shape
    c_qk, c_v, c_ql, c_kvl = dims
    assert m % tm == 0
    prompt = batch is not None
    row = lambda w: pl.BlockSpec((tm, w), lambda i: (i, 0))
    hw = DA_HEADS * VT_W
    extra_in, extra_specs, scratch = [], [], []
    if prompt:
        seq = tab_blocks * tm
        assert m == batch * seq
        col = lambda w: pl.BlockSpec((None, w, tm), lambda i: (i // tab_blocks, 0, i % tab_blocks))
        hbm = pl.BlockSpec(memory_space=pl.ANY)
        tall = (batch, N_META + seq)
        extra_in = list(meta)
        extra_specs = [_const_spec(a.shape) for a in meta]
        scratch = [pltpu.VMEM((2, tm, DA_HEADS, HEAD_W), F32), pltpu.VMEM((2, tm, DA_HEADS, HEAD_W), F32),
                   pltpu.VMEM((2, tm, c_kvl), F32), pltpu.SemaphoreType.DMA((2, 3)), pltpu.SemaphoreType.DMA((3,))]
        out_specs = [col(c_qk), hbm, row(c_qk), hbm, col(hw), col(MLA_HEADS * MLA_W), hbm,
                     row(MLA_ROPE), row(MLA_HEADS * MLA_W), col(hw)]
        out_shape = [
            jax.ShapeDtypeStruct((batch, c_qk, seq), BF16),
            jax.ShapeDtypeStruct(tall + (DA_HEADS, HEAD_W), F32),
            jax.ShapeDtypeStruct((m, c_qk), BF16),
            jax.ShapeDtypeStruct(tall + (DA_HEADS, HEAD_W), F32),
            jax.ShapeDtypeStruct((batch, hw, seq), BF16),
            jax.ShapeDtypeStruct((batch, MLA_HEADS * MLA_W, seq), BF16),
            jax.ShapeDtypeStruct(tall + (c_kvl,), F32),
            jax.ShapeDtypeStruct((m, MLA_ROPE), F32),
            jax.ShapeDtypeStruct((m, MLA_HEADS * MLA_W), BF16),
            jax.ShapeDtypeStruct((batch, hw, seq), BF16),
        ]
    else:
        heads32 = pl.BlockSpec((tm, DA_HEADS, HEAD_W), lambda i: (i, 0, 0))
        out_specs = [row(c_qk), heads32, row(c_qk), heads32, row(c_v), row(MLA_HEADS * MLA_W), row(c_kvl),
                     row(MLA_ROPE), row(MLA_HEADS * MLA_W), row(MLA_HEADS * MLA_V)]
        out_shape = [
            jax.ShapeDtypeStruct((m, c_qk), BF16),
            jax.ShapeDtypeStruct((m, DA_HEADS, HEAD_W), F32),
            jax.ShapeDtypeStruct((m, c_qk), BF16),
            jax.ShapeDtypeStruct((m, DA_HEADS, HEAD_W), F32),
            jax.ShapeDtypeStruct((m, c_v), BF16),
            jax.ShapeDtypeStruct((m, MLA_HEADS * MLA_W), BF16),
            jax.ShapeDtypeStruct((m, c_kvl), F32),
            jax.ShapeDtypeStruct((m, MLA_ROPE), F32),
            jax.ShapeDtypeStruct((m, MLA_HEADS * MLA_W), BF16),
            jax.ShapeDtypeStruct((m, MLA_HEADS * MLA_V), BF16),
        ]
    return pl.pallas_call(
        functools.partial(_inproj_kernel, c_qk=c_qk, c_v=c_v, c_ql=c_ql, c_kvl=c_kvl, tab_blocks=tab_blocks,
                          prompt=prompt),
        grid=(m // tm,),
        in_specs=[
            row(d),
            pl.BlockSpec((tm, LANES), lambda i: (i % tab_blocks, 0)),
            _const_spec(g1.shape), _const_spec(win.shape), _const_spec(qg.shape),
            _const_spec(wuq.shape), _const_spec(kvg.shape), _const_spec(wukv.shape),
        ] + extra_specs,
        out_specs=out_specs,
        out_shape=out_shape,
        scratch_shapes=scratch,
        compiler_params=_cparams(1),
        name="inproj",
    )(x, tab, g1, win, qg, wuq, kvg, wukv, *extra_in)


def _softmax_seed(s, v, m_ref, l_ref, acc_ref, j):
    m = jnp.max(s, axis=1, keepdims=True)
    p = jnp.exp2(s - m)
    m_ref[j] = m
    l_ref[j] = jnp.sum(p, axis=1, keepdims=True)
    acc_ref[j] = _dot(p.astype(BF16), v)


def _softmax_step(s, v, m_ref, l_ref, acc_ref, j):
    m_old = m_ref[j]
    m_new = jnp.maximum(m_old, jnp.max(s, axis=1, keepdims=True))
    alpha = jnp.exp2(m_old - m_new)
    p = jnp.exp2(s - m_new)
    l_ref[j] = alpha * l_ref[j] + jnp.sum(p, axis=1, keepdims=True)
    acc_ref[j] = alpha * acc_ref[j] + _dot(p.astype(BF16), v)
    m_ref[j] = m_new


def _diff_lambda(lamv, lam_init):
    a = jnp.sum(lamv[0:1] * lamv[1:2], axis=1, keepdims=True)
    b = jnp.sum(lamv[2:3] * lamv[3:4], axis=1, keepdims=True)
    return jnp.exp(a) - jnp.exp(b) + lam_init


def _split_maps(q):
    lane = lax.broadcasted_iota(jnp.int32, q.shape, 1)
    zero = jnp.zeros_like(q)
    return jnp.where(lane < DA_D, q, zero), jnp.where(lane >= DA_D, q, zero)


def _init_state(m_ref, acc_ref):
    m_ref[...] = jnp.full(m_ref.shape, NEG_BIG, F32)
    acc_ref[...] = jnp.zeros(acc_ref.shape, F32)


def _step_t(st, shift, vt, m_ref, acc_ref, j):
    m_old = m_ref[j]
    m_new = jnp.maximum(m_old, jnp.max(st, axis=0, keepdims=True) + shift)
    p = jnp.exp2(st - (m_new - shift))
    acc_ref[j] = jnp.exp2(m_old - m_new) * acc_ref[j] + _dot(vt, p.astype(BF16))
    m_ref[j] = m_new


def _diff_attn_parts(qi, ki, qt_ref, k_ref, vt_ref, mk_ref, mvt_ref, pos_ref, cq_ref, corr_ref,
                     lamv_ref, g_ref, o_ref, m_s, acc_s, *, tq, lam_init):
    qrow = lax.broadcasted_iota(jnp.int32, (HEAD_W, tq), 0) < DA_D
    klane = lax.broadcasted_iota(jnp.int32, (tq, HEAD_W), 1) < DA_D

    def init():
        _init_state(m_s, acc_s)

    def body(diag):
        qpos = (lax.broadcasted_iota(jnp.int32, (1, tq), 1) + (qi - ki) * tq).astype(F32)
        pos = pos_ref[...]
        mlane = lax.broadcasted_iota(jnp.int32, (N_META, HEAD_W), 1) < DA_D

        def scores(h):
            hs = slice(h * HEAD_W, (h + 1) * HEAD_W)
            qt = qt_ref[hs, :]
            cq = cq_ref[h]
            kk = k_ref[:, hs]
            corr = corr_ref[...] * (_alibi_slope(h) * LOG2E) if diag else None
            mk = mk_ref[:, hs]
            mzero = jnp.zeros_like(mk)
            out = []
            for c in range(2):
                qc = jnp.where(qrow, qt, cq) if c == 0 else jnp.where(qrow, cq, qt)
                kc = jnp.where(klane, kk, pos) if c == 0 else jnp.where(klane, pos, kk)
                if diag:
                    mc = jnp.where(mlane, mk, mzero) if c == 0 else jnp.where(mlane, mzero, mk)
                    kc = jnp.concatenate([kc, mc], axis=0)
                st = _dot(kc, qc)
                out.append(st + corr if diag else st)
            return out

        queue = [scores(h) for h in range(SCORE_LOOKAHEAD)]
        for h in range(DA_HEADS):
            cur = queue.pop(0)
            if h + SCORE_LOOKAHEAD < DA_HEADS:
                queue.append(scores(h + SCORE_LOOKAHEAD))
            shift = qpos * (-(_alibi_slope(h) * LOG2E))
            vt = vt_ref[h * VT_W:(h + 1) * VT_W, :]
            if diag:
                vt = jnp.concatenate([vt, mvt_ref[h * VT_W:(h + 1) * VT_W, :]], axis=1)
            for c in range(2):
                _step_t(cur[c], shift, vt, m_s, acc_s, 2 * h + c)

    def finish():
        lam = _diff_lambda(lamv_ref[...], lam_init)
        g = g_ref[...]
        for h in range(DA_HEADS):
            a0 = acc_s[2 * h]
            a1 = acc_s[2 * h + 1]
            ot = a0[0:DA_V] / a0[DA_V:DA_V + 1] - lam * (a1[0:DA_V] / a1[DA_V:DA_V + 1])
            ot = ot * lax.rsqrt(jnp.mean(ot * ot, axis=0, keepdims=True) + EPS) * g * (1.0 - lam_init)
            o_ref[:, h * HEAD_W:(h + 1) * HEAD_W] = ot.T.astype(BF16)

    return init, body, finish


def _mla_attn_parts(qt_ref, k_ref, vt_ref, mk_ref, mvt_ref, mask_ref, o_ref, m_s, acc_s):
    def init():
        _init_state(m_s, acc_s)

    def body(diag):
        def scores(h):
            kk = k_ref[:, h * MLA_W:(h + 1) * MLA_W]
            if diag:
                kk = jnp.concatenate([kk, mk_ref[:, h * MLA_W:(h + 1) * MLA_W]], axis=0)
            st = _dot(kk, qt_ref[h * MLA_W:(h + 1) * MLA_W, :])
            return st + mask_ref[...] if diag else st

        queue = [scores(h) for h in range(SCORE_LOOKAHEAD)]
        for h in range(MLA_HEADS):
            st = queue.pop(0)
            if h + SCORE_LOOKAHEAD < MLA_HEADS:
                queue.append(scores(h + SCORE_LOOKAHEAD))
            vt = vt_ref[h * VT_W:(h + 1) * VT_W, :]
            if diag:
                vt = jnp.concatenate([vt, mvt_ref[h * VT_W:(h + 1) * VT_W, :]], axis=1)
            _step_t(st, 0.0, vt, m_s, acc_s, h)

    def finish():
        for h in range(MLA_HEADS):
            a = acc_s[h]
            o_ref[:, h * MLA_V:(h + 1) * MLA_V] = (a[0:MLA_V] / a[MLA_V:MLA_V + 1]).T.astype(BF16)

    return init, body, finish


N_DIFF_IN = 10
N_MLA_IN = 6


def _prompt_attn_kernel(qi_ref, ki_ref, *refs, tq, lam_init):
    t = pl.program_id(1)
    qi = qi_ref[t]
    ki = ki_ref[t]
    d_in, refs = refs[:N_DIFF_IN], refs[N_DIFF_IN:]
    m_in, refs = refs[:N_MLA_IN], refs[N_MLA_IN:]
    d_out, m_out, d_m, d_acc, m_m, m_acc = refs
    d_init, d_body, d_finish = _diff_attn_parts(qi, ki, *d_in, d_out, d_m, d_acc, tq=tq, lam_init=lam_init)
    m_init, m_body, m_finish = _mla_attn_parts(*m_in, m_out, m_m, m_acc)

    @pl.when(ki == 0)
    def _init():
        d_init()
        m_init()

    @pl.when(ki != qi)
    def _off_diagonal():
        d_body(False)
        m_body(False)

    @pl.when(ki == qi)
    def _diagonal():
        d_body(True)
        m_body(True)
        d_finish()
        m_finish()


def _pair_tables(nq):
    qi = np.concatenate([np.full((i + 1,), i, np.int32) for i in range(nq)])
    ki = np.concatenate([np.arange(i + 1, dtype=np.int32) for i in range(nq)])
    return jnp.asarray(qi), jnp.asarray(ki)


def _tile_geometry(tq):
    j = np.arange(tq)[:, None]
    i = np.arange(tq)[None, :]
    visible = (j // CHUNK) <= (i // CHUNK)
    return i, j, visible


def _bf16_split3(x):
    parts = []
    for _ in range(3):
        p = float(np.asarray(x, np.float32).astype(BF16).astype(np.float32))
        parts.append(p)
        x = x - p
    return parts


def _alibi_operands(tq):
    assert tq <= 2 * MXU_DIM
    j = np.arange(tq)
    jlo = (j % MXU_DIM).astype(np.float32)
    jhi = (j - j % MXU_DIM).astype(np.float32)
    pos = np.zeros((tq, HEAD_W), np.float32)
    cq = np.zeros((DA_HEADS, HEAD_W, tq), np.float32)
    for base in (0, DA_D):
        for r in range(3):
            pos[:, base + 2 * r] = jlo
            pos[:, base + 2 * r + 1] = jhi
    for h in range(DA_HEADS):
        parts = _bf16_split3(_alibi_slope(h) * LOG2E)
        for base in (0, DA_D):
            for r in range(3):
                cq[h, base + 2 * r, :] = parts[r]
                cq[h, base + 2 * r + 1, :] = parts[r]
    return jnp.asarray(pos, BF16), jnp.asarray(cq, BF16)


def _prompt_attn_specs(tq, wq, wk, wv):
    qt_spec = pl.BlockSpec((None, wq, tq), lambda b, t, qi, ki: (b, 0, qi[t]))
    k_spec = pl.BlockSpec((None, tq, wk), lambda b, t, qi, ki: (b, ki[t], 0))
    vt_spec = pl.BlockSpec((None, wv, tq), lambda b, t, qi, ki: (b, 0, ki[t]))
    mk_spec = pl.BlockSpec((N_META, wk), lambda b, t, qi, ki: (0, 0))
    mvt_spec = pl.BlockSpec((wv, N_META), lambda b, t, qi, ki: (0, 0))
    return qt_spec, k_spec, vt_spec, mk_spec, mvt_spec


def _prompt_attn(qt, k, vt, mk, mvt, lamv, g, m_qt, m_k, m_vt, m_mk, m_mvt, *, tq, lam_init):
    b, s, w = k.shape
    nq = s // tq
    qi, ki = _pair_tables(nq)
    i, j, visible = _tile_geometry(tq)
    corr = np.where(visible, np.where(j > i, -2.0 * (j - i), 0.0), NEG_BIG)
    corr = jnp.asarray(np.concatenate([corr, np.broadcast_to(i, (N_META, tq))]).astype(np.float32))
    pos, cq = _alibi_operands(tq)
    mask = np.concatenate([np.where(visible, 0.0, NEG_BIG), np.zeros((N_META, tq))])
    mask = jnp.asarray(mask.astype(np.float32))
    full = lambda a: pl.BlockSpec(a.shape, lambda b_, t, qi_, ki_: (0,) * a.ndim)
    d_args = [qt, k, vt, mk, mvt, pos, cq, corr, lamv, g]
    d_specs = list(_prompt_attn_specs(tq, w, w, vt.shape[1])) + [full(a) for a in d_args[5:]]
    m_args = [m_qt, m_k, m_vt, m_mk, m_mvt, mask]
    wq = m_k.shape[2]
    m_specs = list(_prompt_attn_specs(tq, wq, wq, m_vt.shape[1])) + [full(mask)]
    assert len(d_args) == N_DIFF_IN and len(m_args) == N_MLA_IN
    wo = MLA_HEADS * MLA_V
    out_spec = lambda width: pl.BlockSpec((None, tq, width), lambda b_, t, qi_, ki_: (b_, qi_[t], 0))
    return pl.pallas_call(
        functools.partial(_prompt_attn_kernel, tq=tq, lam_init=lam_init),
        grid_spec=pltpu.PrefetchScalarGridSpec(
            num_scalar_prefetch=2,
            grid=(b, int(qi.shape[0])),
            in_specs=d_specs + m_specs,
            out_specs=[out_spec(w), out_spec(wo)],
            scratch_shapes=[pltpu.VMEM((2 * DA_HEADS, 1, tq), F32), pltpu.VMEM((2 * DA_HEADS, VT_W, tq), F32),
                            pltpu.VMEM((MLA_HEADS, 1, tq), F32), pltpu.VMEM((MLA_HEADS, VT_W, tq), F32)],
        ),
        out_shape=[jax.ShapeDtypeStruct((b, s, w), BF16), jax.ShapeDtypeStruct((b, s, wo), BF16)],
        compiler_params=_cparams(2),
        name="prompt_attn",
    )(qi, ki, *d_args, *m_args)


def _sample_diff_kernel(q_ref, kc_hbm, vc_hbm, kn_ref, vn_ref, dc_ref, dn_ref, lamv_ref, g_ref,
                        o_ref, kbuf, vbuf, sem, m_s, l_s, acc_s, *, tk, lam_init):
    kt = pl.program_id(1)
    nkt = pl.num_programs(1)
    step = pl.program_id(0) * nkt + kt
    n_steps = pl.num_programs(0) * nkt

    def tile_copies(s, slot):
        sb = s // nkt
        rows = pl.ds((s % nkt) * tk, tk)
        cps = []
        for h in range(DA_HEADS):
            cps.append(pltpu.make_async_copy(kc_hbm.at[sb, rows, h, :], kbuf.at[slot, h], sem.at[slot, 0]))
            cps.append(pltpu.make_async_copy(vc_hbm.at[sb, rows, h, :], vbuf.at[slot, h], sem.at[slot, 1]))
        return cps

    @pl.when(step == 0)
    def _prime():
        for cp in tile_copies(0, 0):
            cp.start()

    @pl.when(step + 1 < n_steps)
    def _prefetch():
        for cp in tile_copies(step + 1, (step + 1) % 2):
            cp.start()

    slot = step % 2
    for cp in tile_copies(step, slot):
        cp.wait()

    def heads(get_k, get_v, dist, first):
        for h in range(DA_HEADS):
            q1, q2 = _split_maps(q_ref[:, h * HEAD_W:(h + 1) * HEAD_W])
            qq = jnp.concatenate([q1, q2], axis=0)
            s = _dot_nt(qq, get_k(h)) + dist * (-_alibi_slope(h) * LOG2E)
            if first:
                _softmax_seed(s, get_v(h), m_s, l_s, acc_s, h)
            else:
                _softmax_step(s, get_v(h), m_s, l_s, acc_s, h)

    cache_k = lambda h: kbuf[slot, h].astype(BF16)
    cache_v = lambda h: vbuf[slot, h].astype(BF16)

    @pl.when(kt == 0)
    def _first():
        heads(cache_k, cache_v, dc_ref[...], True)

    @pl.when(kt > 0)
    def _rest():
        heads(cache_k, cache_v, dc_ref[...], False)

    @pl.when(kt == nkt - 1)
    def _finish():
        heads(lambda h: kn_ref[:, h * HEAD_W:(h + 1) * HEAD_W], lambda h: vn_ref[:, h * HEAD_W:(h + 1) * HEAD_W],
              dn_ref[...], False)
        lam = _diff_lambda(lamv_ref[...], lam_init)
        g = g_ref[...]
        nq = q_ref.shape[0]
        for h in range(DA_HEADS):
            a = acc_s[h] / l_s[h]
            o = a[0:nq] - lam * a[nq:2 * nq]
            o_ref[:, h * HEAD_W:(h + 1) * HEAD_W] = (_rms(o, g) * (1.0 - lam_init)).astype(BF16)


def _sample_diff_attn(q, kc, vc, kn, vn, dist_c, dist_n, lamv, g, *, tk, lam_init):
    bs, nq, w = q.shape
    lc = kc.shape[1]
    assert lc % tk == 0
    full = lambda a: pl.BlockSpec(a.shape, lambda b, t: (0,) * a.ndim)
    per_stream = lambda a: pl.BlockSpec((None,) + a.shape[1:], lambda b, t: (b,) + (0,) * (a.ndim - 1))
    cache = pl.BlockSpec(memory_space=pl.ANY)
    return pl.pallas_call(
        functools.partial(_sample_diff_kernel, tk=tk, lam_init=lam_init),
        grid=(bs, lc // tk),
        in_specs=[per_stream(q), cache, cache, per_stream(kn), per_stream(vn),
                  pl.BlockSpec((None, 2 * nq, tk), lambda b, t: (t, 0, 0)),
                  full(dist_n), full(lamv), full(g)],
        out_specs=per_stream(q),
        out_shape=jax.ShapeDtypeStruct((bs, nq, w), BF16),
        scratch_shapes=[pltpu.VMEM((2, DA_HEADS, tk, HEAD_W), F32), pltpu.VMEM((2, DA_HEADS, tk, HEAD_W), F32),
                        pltpu.SemaphoreType.DMA((2, 2)),
                        pltpu.VMEM((DA_HEADS, 2 * nq, 1), F32), pltpu.VMEM((DA_HEADS, 2 * nq, 1), F32),
                        pltpu.VMEM((DA_HEADS, 2 * nq, DA_V), F32)],
        compiler_params=_cparams(2),
        name="sample_diff_attn",
    )(q, kc, vc, kn, vn, dist_c, dist_n, lamv, g)


def _sample_mla_kernel(q_ref, cc_ref, krc_ref, cn_ref, krn_ref, wuk_ref, wuv_ref, o_ref,
                       ql_s, qr_s, m_s, l_s, acc_s):
    kt = pl.program_id(1)
    nkt = pl.num_programs(1)
    nq = q_ref.shape[0]

    @pl.when(kt == 0)
    def _prep():
        for h in range(MLA_HEADS):
            qn = q_ref[:, h * MLA_W:h * MLA_W + MLA_NOPE]
            ql_s[h * nq:(h + 1) * nq, :] = _dot_nt(qn, wuk_ref[:, h * MLA_NOPE:(h + 1) * MLA_NOPE]).astype(BF16)
            qr_s[h * nq:(h + 1) * nq, :] = q_ref[:, h * MLA_W + MLA_NOPE:(h + 1) * MLA_W]

    def scores(c_ref, kr_ref):
        cb = c_ref[...].astype(BF16)
        krb = kr_ref[...].astype(BF16)
        s = _dot_nt(ql_s[...], cb) + _dot_nt(qr_s[:, 0:MLA_ROPE], krb)
        return s, cb

    @pl.when(kt == 0)
    def _first():
        s, cb = scores(cc_ref, krc_ref)
        _softmax_seed(s, cb, m_s, l_s, acc_s, 0)

    @pl.when(kt > 0)
    def _rest():
        s, cb = scores(cc_ref, krc_ref)
        _softmax_step(s, cb, m_s, l_s, acc_s, 0)

    @pl.when(kt == nkt - 1)
    def _finish():
        s, cb = scores(cn_ref, krn_ref)
        _softmax_step(s, cb, m_s, l_s, acc_s, 0)
        ol = (acc_s[0] / l_s[0]).astype(BF16)
        for h in range(MLA_HEADS):
            o_ref[:, h * MLA_V:(h + 1) * MLA_V] = _dot(
                ol[h * nq:(h + 1) * nq, :], wuv_ref[:, h * MLA_V:(h + 1) * MLA_V]).astype(BF16)


def _sample_mla_attn(q, cc, krc, cn, krn, wuk, wuv, *, tk):
    bs, nq, wq = q.shape
    lc, kvl = cc.shape[1], cc.shape[2]
    assert lc % tk == 0
    full = lambda a: pl.BlockSpec(a.shape, lambda b, t: (0,) * a.ndim)
    per_stream = lambda a: pl.BlockSpec((None,) + a.shape[1:], lambda b, t: (b,) + (0,) * (a.ndim - 1))
    rows = MLA_HEADS * nq
    return pl.pallas_call(
        _sample_mla_kernel,
        grid=(bs, lc // tk),
        in_specs=[per_stream(q),
                  pl.BlockSpec((None, tk, kvl), lambda b, t: (b, t, 0)),
                  pl.BlockSpec((None, tk, MLA_ROPE), lambda b, t: (b, t, 0)),
                  per_stream(cn), per_stream(krn), full(wuk), full(wuv)],
        out_specs=pl.BlockSpec((None, nq, MLA_HEADS * MLA_V), lambda b, t: (b, 0, 0)),
        out_shape=jax.ShapeDtypeStruct((bs, nq, MLA_HEADS * MLA_V), BF16),
        scratch_shapes=[pltpu.VMEM((rows, kvl), BF16), pltpu.VMEM((rows, LANES), BF16),
                        pltpu.VMEM((1, rows, 1), F32), pltpu.VMEM((1, rows, 1), F32),
                        pltpu.VMEM((1, rows, kvl), F32)],
        compiler_params=_cparams(2),
        name="sample_mla_attn",
    )(q, cc, krc, cn, krn, wuk, wuv)


ROUTER_ROWS = SUBLANES * (1 + N_GROUPS)


def _route(lt):
    g = [lt[i:i + 1] for i in range(N_GROUPS)]
    gmax = functools.reduce(jnp.maximum, g)
    gidx = jnp.full_like(gmax, float(N_GROUPS - 1))
    for i in range(N_GROUPS - 2, -1, -1):
        gidx = jnp.where(g[i] == gmax, float(i), gidx)
    den = functools.reduce(lambda a, b: a + b, [jnp.exp(gi - gmax) for gi in g])
    p_top = 1.0 / den
    e = []
    for j in range(EXPERTS_PER_GROUP):
        ej = lt[SUBLANES * N_GROUPS + j:SUBLANES * N_GROUPS + j + 1]
        for grp in range(N_GROUPS - 2, -1, -1):
            ej = jnp.where(gidx == float(grp), lt[SUBLANES * (grp + 1) + j:SUBLANES * (grp + 1) + j + 1], ej)
        e.append(ej)

    def first_argmax(vals):
        vmax = functools.reduce(jnp.maximum, vals)
        idx = jnp.full_like(vmax, float(len(vals) - 1))
        for i in range(len(vals) - 2, -1, -1):
            idx = jnp.where(vals[i] == vmax, float(i), idx)
        return vmax, idx

    v1, i1 = first_argmax(e)
    rest = [jnp.where(i1 == float(j), -jnp.inf, e[j]) for j in range(EXPERTS_PER_GROUP)]
    v2, i2 = first_argmax(rest)
    r = jnp.exp(v2 - v1)
    w1 = p_top / (1.0 + r)
    w2 = p_top * r / (1.0 + r)
    base = gidx * float(EXPERTS_PER_GROUP)
    return w1, w2, base + i1, base + i2


def _merge_kernel(od_ref, om_ref, x_ref, wo_ref, g2_ref, wr_ref, br_ref, hp_ref, xn_ref, rt_ref):
    nd = od_ref.shape[1]
    y = _dot(od_ref[...], wo_ref[0:nd, :]) + _dot(om_ref[...], wo_ref[nd:, :])
    hp = x_ref[...] + y
    hp_ref[...] = hp
    xn = _rms(hp, g2_ref[...])
    xn_ref[...] = _pack_halves(xn)
    lt = _dot_nt(wr_ref[...], xn.astype(BF16)) + br_ref[...]
    rows = _route(lt)
    for i, r in enumerate(rows):
        rt_ref[i:i + 1, :] = r
    rt_ref[4:8, :] = jnp.zeros((4, rt_ref.shape[1]), F32)


def _merge(od, om, x, wo, g2, wr, br, *, tm):
    m, d = x.shape
    row = lambda w: pl.BlockSpec((tm, w), lambda i: (i, 0))
    return pl.pallas_call(
        _merge_kernel,
        grid=(m // tm,),
        in_specs=[row(od.shape[1]), row(om.shape[1]), row(d), _const_spec(wo.shape),
                  _const_spec(g2.shape), _const_spec(wr.shape), _const_spec(br.shape)],
        out_specs=[row(d), row(d // 2), pl.BlockSpec((SUBLANES, tm), lambda i: (0, i))],
        out_shape=[jax.ShapeDtypeStruct((m, d), F32), jax.ShapeDtypeStruct((m, d // 2), jnp.uint32),
                   jax.ShapeDtypeStruct((SUBLANES, m), F32)],
        compiler_params=_cparams(1),
        name="merge",
    )(od, om, x, wo, g2, wr, br)


def _cast_kernel(x_ref, o_ref):
    o_ref[...] = x_ref[...].astype(BF16)


def _cast_bf16(w, *, rows):
    e, r, c = w.shape
    assert r % rows == 0
    spec = pl.BlockSpec((None, rows, c), lambda i, j: (i, j, 0))
    return pl.pallas_call(
        _cast_kernel,
        grid=(e, r // rows),
        in_specs=[spec],
        out_specs=spec,
        out_shape=jax.ShapeDtypeStruct(w.shape, BF16),
        compiler_params=_cparams(2),
        name="cast_bf16",
    )(w)


def _swiglu(xp, wg_ref, wu_ref, wd_ref):
    lo, hi = _unpack_halves(xp)
    lo = lo.astype(BF16)
    hi = hi.astype(BF16)
    w = lo.shape[1]
    g = _dot(lo, wg_ref[0:w, :]) + _dot(hi, wg_ref[w:, :])
    u = _dot(lo, wu_ref[0:w, :]) + _dot(hi, wu_ref[w:, :])
    h = (g * jax.nn.sigmoid(g) * u).astype(BF16)
    return _dot(h, wd_ref[...])


def _moe_sorted_kernel(te_ref, nu_ref, x_ref, w_ref, wg_ref, wu_ref, wd_ref, *rest, out_first, x_first, x_tiles):
    y_ref = rest[-1]
    g = pl.program_id(0) + out_first
    live = jnp.logical_and(g < nu_ref[0], jnp.logical_and(g >= x_first, g < x_first + x_tiles))

    @pl.when(live)
    def _():
        y_ref[...] = _pack_halves(w_ref[...] * _swiglu(x_ref[...], wg_ref, wu_ref, wd_ref))

    @pl.when(jnp.logical_not(live))
    def _():
        y_ref[...] = jnp.zeros_like(y_ref)


def _moe_sorted(tile_expert, n_used, xs, ws, wg, wu, wd, *, tm, x_first, y_prev=None):
    n_x, dh = xs.shape
    d = 2 * dh
    f = wg.shape[2]
    x_tiles = n_x // tm
    n_tiles = ws.shape[0] // tm
    out_first = 0 if y_prev is None else x_first
    grid_tiles = n_tiles if y_prev is None else x_tiles
    local = lambda i: jnp.clip(i + out_first - x_first, 0, x_tiles - 1)
    in_specs = [pl.BlockSpec((tm, dh), lambda i, te, nu: (local(i), 0)),
                pl.BlockSpec((tm, 1), lambda i, te, nu: (i + out_first, 0)),
                pl.BlockSpec((None, d, f), lambda i, te, nu: (te[local(i) + x_first], 0, 0)),
                pl.BlockSpec((None, d, f), lambda i, te, nu: (te[local(i) + x_first], 0, 0)),
                pl.BlockSpec((None, f, d), lambda i, te, nu: (te[local(i) + x_first], 0, 0))]
    args = [tile_expert, n_used, xs, ws, wg, wu, wd]
    aliases = {}
    if y_prev is not None:
        in_specs.append(pl.BlockSpec(memory_space=pl.ANY))
        args.append(y_prev)
        aliases = {len(args) - 1: 0}
    return pl.pallas_call(
        functools.partial(_moe_sorted_kernel, out_first=out_first, x_first=x_first, x_tiles=x_tiles),
        grid_spec=pltpu.PrefetchScalarGridSpec(
            num_scalar_prefetch=2,
            grid=(grid_tiles,),
            in_specs=in_specs,
            out_specs=pl.BlockSpec((tm, dh), lambda i, te, nu: (i + out_first, 0)),
        ),
        out_shape=jax.ShapeDtypeStruct((n_tiles * tm, dh), jnp.uint32),
        input_output_aliases=aliases,
        compiler_params=_cparams(1),
        name="moe_sorted",
    )(*args)


def _moe_dense_kernel(x_ref, hp_ref, gates_ref, wg_ref, wu_ref, wd_ref, gf_ref, o_ref, acc_s):
    e = pl.program_id(0)

    @pl.when(e == 0)
    def _():
        acc_s[...] = jnp.zeros_like(acc_s)

    lane = lax.broadcasted_iota(jnp.int32, gates_ref.shape, 1)
    gate = jnp.sum(jnp.where(lane == e, gates_ref[...], 0.0), axis=1, keepdims=True)
    acc_s[...] += gate * _swiglu(x_ref[...], wg_ref, wu_ref, wd_ref)

    @pl.when(e == pl.num_programs(0) - 1)
    def _():
        o_ref[...] = _rms(hp_ref[...] + acc_s[...], gf_ref[...])


def _moe_dense(xn, hp, gates, wg, wu, wd, gf):
    m, d = hp.shape
    ne, _, f = wg.shape
    full = lambda a: pl.BlockSpec(a.shape, lambda e: (0,) * a.ndim)
    return pl.pallas_call(
        _moe_dense_kernel,
        grid=(ne,),
        in_specs=[full(xn), full(hp), full(gates),
                  pl.BlockSpec((None, d, f), lambda e: (e, 0, 0)),
                  pl.BlockSpec((None, d, f), lambda e: (e, 0, 0)),
                  pl.BlockSpec((None, f, d), lambda e: (e, 0, 0)),
                  full(gf)],
        out_specs=full(hp),
        out_shape=jax.ShapeDtypeStruct((m, d), F32),
        scratch_shapes=[pltpu.VMEM((m, d), F32)],
        compiler_params=_cparams(1),
        name="moe_dense",
    )(xn, hp, gates, wg, wu, wd, gf)


def _combine_kernel(hp_ref, y1_ref, y2_ref, gf_ref, o_ref):
    w = y1_ref.shape[1]
    a_lo, a_hi = _unpack_halves(y1_ref[...])
    b_lo, b_hi = _unpack_halves(y2_ref[...])
    s_lo = hp_ref[:, 0:w] + (a_lo + b_lo)
    s_hi = hp_ref[:, w:] + (a_hi + b_hi)
    ms = (jnp.sum(s_lo * s_lo, axis=-1, keepdims=True) + jnp.sum(s_hi * s_hi, axis=-1, keepdims=True)) / (2 * w)
    inv = lax.rsqrt(ms + EPS)
    o_ref[:, 0:w] = s_lo * inv * gf_ref[:, 0:w]
    o_ref[:, w:] = s_hi * inv * gf_ref[:, w:]


def _combine(hp, y1, y2, gf, *, tm):
    m, d = hp.shape
    row = pl.BlockSpec((tm, d), lambda i: (i, 0))
    half = pl.BlockSpec((tm, d // 2), lambda i: (i, 0))
    return pl.pallas_call(
        _combine_kernel,
        grid=(m // tm,),
        in_specs=[row, half, half, _const_spec(gf.shape)],
        out_specs=row,
        out_shape=jax.ShapeDtypeStruct((m, d), F32),
        compiler_params=_cparams(1),
        name="combine",
    )(hp, y1, y2, gf)


def _rope_table(pos):
    half = MLA_ROPE // 2
    inv_freq = ROPE_THETA ** (-jnp.arange(half, dtype=F32) / half)
    ang = pos.astype(F32)[:, None] * inv_freq[None, :]
    c, s = jnp.cos(ang), jnp.sin(ang)
    return jnp.concatenate([c, c, -s, s], axis=1)


def _swap_halves(w):
    half = MLA_ROPE // 2
    return jnp.concatenate([w[..., half:], w[..., :half]], axis=-1)


def _values_t(v):
    n = v.shape[0]
    vt = v.reshape(n, DA_HEADS, DA_V).transpose(1, 2, 0)
    return jnp.concatenate([vt, jnp.ones((DA_HEADS, BF16_ROWS, n), v.dtype)], axis=1).reshape(DA_HEADS * VT_W, n)


def _sort_by_expert(eid, w, tm):
    t = eid.shape[1]
    flat_e = eid.reshape(-1)
    onehot = (flat_e[:, None] == jnp.arange(N_EXPERTS, dtype=jnp.int32)[None, :]).astype(jnp.int32)
    rank = jnp.sum((jnp.cumsum(onehot, axis=0) - onehot) * onehot, axis=1)
    counts = jnp.sum(onehot, axis=0)
    tiles_per = (counts + tm - 1) // tm
    tiles_end = jnp.cumsum(tiles_per)
    row_start = (tiles_end - tiles_per) * tm
    pos = row_start[flat_e] + rank
    n_tiles = (2 * t) // tm + N_EXPERTS
    slot_a = jnp.full((n_tiles * tm,), -1, jnp.int32).at[pos].set(jnp.arange(2 * t, dtype=jnp.int32),
                                                                   unique_indices=True, mode="promise_in_bounds")
    used = slot_a >= 0
    safe_a = jnp.maximum(slot_a, 0)
    sorted_tok = jnp.where(used, safe_a % t, 0)
    sorted_w = jnp.where(used, w.reshape(-1).at[safe_a].get(mode="promise_in_bounds"), 0.0)
    tile_ids = jnp.arange(n_tiles, dtype=jnp.int32)
    tile_expert = jnp.minimum(jnp.sum((tiles_end[None, :] <= tile_ids[:, None]).astype(jnp.int32), axis=1),
                              N_EXPERTS - 1)
    n_used = tiles_end[-1:].astype(jnp.int32)
    return pos.reshape(2, t), sorted_tok, sorted_w, tile_expert, n_used


def kernel(x_prompt, x_sample, cache_diff_k, cache_diff_v, cache_mla_ckv, cache_mla_kr, meta_tokens, norm1_g, w_in, diff_lam_q1, diff_lam_k1, diff_lam_q2, diff_lam_k2, diff_subln_g, mla_q_norm_g, mla_w_uq, mla_kv_norm_g, mla_w_uk, mla_w_uv, w_o, norm2_g, router_group_w, router_group_b, router_expert_w, router_expert_b, expert_w_gate, expert_w_up, expert_w_down, final_norm_g):
    depth = norm1_g.shape[0]
    assert depth == 1, "single-layer step only"
    assert MLA_HEADS == DA_HEADS and MLA_V == DA_V
    lam_init = 0.8 - 0.6 * math.exp(-0.3 * 0)
    b, s, d = x_prompt.shape
    bs, ss, _ = x_sample.shape
    past = cache_mla_kr.shape[2]
    lc = N_META + past
    c_qk = DA_HEADS * 2 * DA_D
    c_v = DA_HEADS * DA_V
    c_ql = mla_q_norm_g.shape[1]
    c_kvl = mla_kv_norm_g.shape[1]
    o5 = 2 * c_qk + c_v + c_ql + c_kvl

    win = w_in[0]
    win_ext = jnp.concatenate([win, _swap_halves(win[:, o5:])], axis=1).astype(BF16)
    wuq = mla_w_uq[0].reshape(c_ql, MLA_HEADS, MLA_NOPE + MLA_ROPE)
    wuq_n = wuq[:, :, :MLA_NOPE].reshape(c_ql, MLA_HEADS * MLA_NOPE)
    wuq_r = jnp.concatenate([wuq[:, :, MLA_NOPE:], _swap_halves(wuq[:, :, MLA_NOPE:])], axis=2)
    wuq_ext = jnp.concatenate([wuq_n, wuq_r.reshape(c_ql, MLA_HEADS * LANES)], axis=1).astype(BF16)
    wuk = mla_w_uk[0].astype(BF16)
    wuv = mla_w_uv[0].astype(BF16)
    wukv = jnp.concatenate([wuk, wuv], axis=1)
    wo = w_o[0].astype(BF16)
    wr = jnp.zeros((ROUTER_ROWS, d), F32).at[0:N_GROUPS].set(router_group_w[0].T)
    br = jnp.zeros((ROUTER_ROWS, 1), F32).at[0:N_GROUPS, 0].set(router_group_b[0])
    rew = router_expert_w[0].T.reshape(N_GROUPS, EXPERTS_PER_GROUP, d)
    reb = router_expert_b[0].reshape(N_GROUPS, EXPERTS_PER_GROUP)
    for grp in range(N_GROUPS):
        wr = wr.at[SUBLANES * (grp + 1):SUBLANES * (grp + 1) + EXPERTS_PER_GROUP].set(rew[grp])
        br = br.at[SUBLANES * (grp + 1):SUBLANES * (grp + 1) + EXPERTS_PER_GROUP, 0].set(reb[grp])
    wr = wr.astype(BF16)
    wg = _cast_bf16(expert_w_gate[0], rows=min(CAST_ROWS, expert_w_gate.shape[2]))
    wu = _cast_bf16(expert_w_up[0], rows=min(CAST_ROWS, expert_w_up.shape[2]))
    wd = _cast_bf16(expert_w_down[0], rows=min(CAST_ROWS, expert_w_down.shape[2]))
    gf = final_norm_g[None, :]
    lamv = jnp.stack([diff_lam_q1[0], diff_lam_k1[0], diff_lam_q2[0], diff_lam_k2[0]])
    subg = diff_subln_g

    dims = (c_qk, c_v, c_ql, c_kvl)
    inproj = functools.partial(_inproj, g1=norm1_g, win=win_ext, qg=mla_q_norm_g, wuq=wuq_ext,
                               kvg=mla_kv_norm_g, wukv=wukv, dims=dims)

    (_, mdk32, mdk, mdv32, mdv, _, mckv, _, mkm, mvm) = inproj(
        meta_tokens, jnp.zeros((N_META, LANES), F32), tm=N_META, tab_blocks=1)

    ts = bs * ss
    s_pos = past + jnp.arange(ss, dtype=jnp.int32)
    (sqd, s_dk, skd, s_dv, svd, sqm, sckv, skr, _, _) = inproj(
        x_sample.reshape(ts, d), _rope_table(s_pos), tm=ss, tab_blocks=1)
    q3 = lambda a: a.reshape(bs, ss, a.shape[-1])
    kpos_c = np.arange(lc) - N_META
    dist_c = np.where(kpos_c[None, :] >= 0, np.abs(past + np.arange(ss)[:, None] - kpos_c[None, :]), 0)
    dist_n = np.abs(np.arange(ss)[:, None] - np.arange(ss)[None, :])
    tk_s = lc // 2 if (lc // 2) % SUBLANES == 0 and lc % 2 == 0 else lc
    dist_c = np.tile(dist_c, (2, 1)).astype(np.float32).reshape(2 * ss, lc // tk_s, tk_s)
    dist_c = jnp.asarray(np.moveaxis(dist_c, 1, 0))
    dist_n = jnp.asarray(np.tile(dist_n, (2, 1)).astype(np.float32))
    sod = _sample_diff_attn(q3(sqd), cache_diff_k[0], cache_diff_v[0], q3(skd), q3(svd), dist_c, dist_n, lamv, subg,
                            tk=tk_s, lam_init=lam_init)
    krc = jnp.concatenate([jnp.zeros((bs, N_META, MLA_ROPE), F32), cache_mla_kr[0]], axis=1)
    som = _sample_mla_attn(q3(sqm), cache_mla_ckv[0], krc, q3(sckv), q3(skr), wuk, wuv, tk=tk_s)
    hs, xn2s, rts = _merge(sod.reshape(ts, -1), som.reshape(ts, -1), x_sample.reshape(ts, d), wo, norm2_g, wr, br,
                           tm=ts)
    eids = rts[2:4].astype(jnp.int32)
    gates = (jnp.where(eids[0][:, None] == jnp.arange(LANES)[None, :], rts[0][:, None], 0.0)
             + jnp.where(eids[1][:, None] == jnp.arange(LANES)[None, :], rts[1][:, None], 0.0))
    y_sample = _moe_dense(xn2s, hs, gates, wg, wu, wd, gf).reshape(bs, ss, d)

    tm_p = min(INPROJ_ROWS, s)
    tab_p = _rope_table(jnp.arange(s, dtype=jnp.int32))
    (pqdt, p_dk, pkd, p_dv, pvdt, pqmt, p_ckv, pkr, pkm, pvmt) = inproj(
        x_prompt.reshape(b * s, d), tab_p, tm=tm_p, tab_blocks=s // tm_p, batch=b, meta=(mdk32, mdv32, mckv))
    tq = min(ATTN_TILE, s)
    r3 = lambda a: a.reshape(b, s, a.shape[-1])
    od, om = _prompt_attn(pqdt, r3(pkd), pvdt, mdk, _values_t(mdv), lamv, subg.T,
                          pqmt, r3(pkm), pvmt, mkm, _values_t(mvm), tq=tq, lam_init=lam_init)
    t = b * s
    tm_t = min(TOKEN_ROWS, t)
    hp, xn2, rt = _merge(od.reshape(t, -1), om.reshape(t, -1), x_prompt.reshape(t, d), wo, norm2_g, wr, br, tm=tm_t)
    pos, sorted_tok, sorted_w, tile_expert, n_used = _sort_by_expert(rt[2:4].astype(jnp.int32), rt[0:2], tm_t)
    rows = lambda a, idx: a.at[idx].get(mode="promise_in_bounds")
    n_tiles = sorted_tok.shape[0] // tm_t
    cut = (n_tiles // 2) * tm_t
    sw = sorted_w[:, None]
    ys = _moe_sorted(tile_expert, n_used, rows(xn2, sorted_tok[:cut]), sw, wg, wu, wd, tm=tm_t, x_first=0)
    if cut < n_tiles * tm_t:
        ys = _moe_sorted(tile_expert, n_used, rows(xn2, sorted_tok[cut:]), sw, wg, wu, wd, tm=tm_t,
                         x_first=cut // tm_t, y_prev=ys)
    y_prompt = _combine(hp, rows(ys, pos[0]), rows(ys, pos[1]), gf, tm=tm_t).reshape(b, s, d)

    return (y_prompt, y_sample,
            p_dk[None], p_dv[None], p_ckv[None], pkr.reshape(1, b, s, MLA_ROPE),
            s_dk.reshape(1, bs, ss, DA_HEADS, 2 * DA_D), s_dv.reshape(1, bs, ss, DA_HEADS, DA_V),
            sckv.reshape(1, bs, ss, c_kvl), skr.reshape(1, bs, ss, MLA_ROPE))
```

```python
import functools
import math

import numpy as np
import jax
import jax.numpy as jnp
from jax import lax
from jax.experimental import pallas as pl
from jax.experimental.pallas import tpu as pltpu

CHUNK = 64
N_META = 16
EPS = 1e-6
DA_HEADS = 8
DA_D = 64
DA_V = 2 * DA_D
MLA_HEADS = 8
MLA_NOPE = 128
MLA_ROPE = 64
MLA_V = 128
ROPE_THETA = 10000.0
MLA_SCALE = (MLA_NOPE + MLA_ROPE) ** -0.5
N_GROUPS = 4
EXPERTS_PER_GROUP = 4
N_EXPERTS = N_GROUPS * EXPERTS_PER_GROUP
LOG2E = math.log2(math.e)
LANES = 128
SUBLANES = 8
BF16_ROWS = 16
MXU_DIM = 256
HEAD_W = 128
MLA_W = 256
VT_W = DA_V + BF16_ROWS
NEG_BIG = -1e30
VMEM_LIMIT = 56 * 1024 * 1024
ATTN_TILE = 512
SCORE_LOOKAHEAD = 2
KEY_TILES = 2
INPROJ_ROWS = 256
TOKEN_ROWS = 512
CAST_ROWS = 1024

BF16 = jnp.bfloat16
F32 = jnp.float32


def _dot(a, b):
    return jnp.dot(a, b, preferred_element_type=F32)


def _dot_nt(a, b):
    return lax.dot_general(a, b, (((1,), (1,)), ((), ())), preferred_element_type=F32)


def _rms(x, g):
    return x * lax.rsqrt(jnp.mean(x * x, axis=-1, keepdims=True) + EPS) * g


def _pack_halves(x):
    w = x.shape[1] // 2
    bits = lax.bitcast_convert_type(x.astype(BF16).astype(F32), jnp.uint32)
    return (bits[:, :w] >> 16) | (bits[:, w:] & jnp.uint32(0xFFFF0000))


def _unpack_halves(u):
    lo = lax.bitcast_convert_type(u << 16, F32)
    hi = lax.bitcast_convert_type(u & jnp.uint32(0xFFFF0000), F32)
    return lo, hi


def _cparams(n_axes):
    return pltpu.CompilerParams(dimension_semantics=("arbitrary",) * n_axes,
                                vmem_limit_bytes=VMEM_LIMIT)


def _const_spec(shape):
    nd = len(shape)
    return pl.BlockSpec(shape, lambda *_: (0,) * nd, pipeline_mode=pl.Buffered(1))


def _alibi_slope(h):
    return 2.0 ** (-8.0 * (h + 1) / DA_HEADS)


def _inproj_kernel(*refs, c_qk, c_v, c_ql, c_kvl, tab_blocks, prompt):
    (x_ref, tab_ref, g1_ref, win_ref, qg_ref, wuq_ref, kvg_ref, wukv_ref), refs = refs[:8], refs[8:]
    if prompt:
        (mk32_ref, mv32_ref, mckv_ref), refs = refs[:3], refs[3:]
    (qd_ref, kd32_ref, kdb_ref, vd32_ref, vdb_ref, qm_ref, ckv_ref, kr_ref, km_ref, vm_ref), refs = refs[:10], refs[10:]
    x = x_ref[...]
    tm = x.shape[0]
    if prompt:
        kbuf, vbuf, cbuf, sem, msem = refs
        i = pl.program_id(0)
        n = pl.num_programs(0)
        slot = i % 2

        def tile_copies(step, s):
            sb = step // tab_blocks
            rows = pl.ds(N_META + (step % tab_blocks) * tm, tm)
            return [pltpu.make_async_copy(kbuf.at[s], kd32_ref.at[sb, rows], sem.at[s, 0]),
                    pltpu.make_async_copy(vbuf.at[s], vd32_ref.at[sb, rows], sem.at[s, 1]),
                    pltpu.make_async_copy(cbuf.at[s], ckv_ref.at[sb, rows], sem.at[s, 2])]

        @pl.when(i >= 2)
        def _slot_free():
            for cp in tile_copies(i - 2, slot):
                cp.wait()

    xn = _rms(x, g1_ref[...]).astype(BF16)
    tab = tab_ref[...]
    o1 = c_qk
    o2 = o1 + c_qk
    o3 = o2 + c_v
    o4 = o3 + c_ql
    o5 = o4 + c_kvl
    ones = jnp.ones((BF16_ROWS, tm), BF16)

    def put_heads32(ref, buf, z):
        for h in range(DA_HEADS):
            if prompt:
                buf[slot, :, h, :] = z[:, h * HEAD_W:(h + 1) * HEAD_W]
            else:
                ref[:, h, :] = z[:, h * HEAD_W:(h + 1) * HEAD_W]

    def put_values_t(ref, z):
        for h in range(DA_HEADS):
            ref[h * VT_W:h * VT_W + DA_V, :] = z[:, h * DA_V:(h + 1) * DA_V].T.astype(BF16)
            ref[h * VT_W + DA_V:(h + 1) * VT_W, :] = ones

    zq = _dot(xn, win_ref[:, 0:o1]) * (DA_D ** -0.5 * LOG2E)
    qd_ref[...] = zq.T.astype(BF16) if prompt else zq.astype(BF16)
    zk = _dot(xn, win_ref[:, o1:o2])
    put_heads32(kd32_ref, kbuf if prompt else None, zk)
    kdb_ref[...] = zk.astype(BF16)
    zv = _dot(xn, win_ref[:, o2:o3])
    put_heads32(vd32_ref, vbuf if prompt else None, zv)
    if prompt:
        put_values_t(vdb_ref, zv)
    else:
        vdb_ref[...] = zv.astype(BF16)

    cq = _rms(_dot(xn, win_ref[:, o3:o4]), qg_ref[...]).astype(BF16)
    nq = MLA_HEADS * MLA_NOPE
    qn = _dot(cq, wuq_ref[:, 0:nq]) * (MLA_SCALE * LOG2E)
    qr = _dot(cq, wuq_ref[:, nq:2 * nq]) * (MLA_SCALE * LOG2E)
    for h in range(MLA_HEADS):
        u = qr[:, h * LANES:(h + 1) * LANES] * tab
        rot = u + pltpu.roll(u, MLA_ROPE, 1)
        nope = qn[:, h * LANES:(h + 1) * LANES]
        if prompt:
            qm_ref[h * MLA_W:h * MLA_W + LANES, :] = nope.T.astype(BF16)
            qm_ref[h * MLA_W + LANES:(h + 1) * MLA_W, :] = rot.T.astype(BF16)
        else:
            qm_ref[:, h * MLA_W:h * MLA_W + LANES] = nope.astype(BF16)
            qm_ref[:, h * MLA_W + LANES:(h + 1) * MLA_W] = rot.astype(BF16)

    ckv = _rms(_dot(xn, win_ref[:, o4:o5]), kvg_ref[...])
    if prompt:
        cbuf[slot] = ckv
    else:
        ckv_ref[...] = ckv
    ckvb = ckv.astype(BF16)
    nk = MLA_HEADS * MLA_NOPE
    kn = _dot(ckvb, wukv_ref[:, 0:nk])
    vm = _dot(ckvb, wukv_ref[:, nk:nk + MLA_HEADS * MLA_V])
    if prompt:
        put_values_t(vm_ref, vm)
    else:
        vm_ref[...] = vm.astype(BF16)

    u = _dot(xn, win_ref[:, o5:o5 + LANES]) * tab
    rot = u + pltpu.roll(u, MLA_ROPE, 1)
    kr_ref[...] = rot[:, 0:MLA_ROPE]
    lane = lax.broadcasted_iota(jnp.int32, rot.shape, 1)
    krp = jnp.where(lane < MLA_ROPE, rot, 0.0).astype(BF16)
    for h in range(MLA_HEADS):
        km_ref[:, h * MLA_W:h * MLA_W + LANES] = kn[:, h * LANES:(h + 1) * LANES].astype(BF16)
        km_ref[:, h * MLA_W + LANES:(h + 1) * MLA_W] = krp

    if prompt:
        for cp in tile_copies(i, slot):
            cp.start()

        @pl.when(i % tab_blocks == 0)
        def _meta_rows():
            head = pl.ds(0, N_META)
            sb = i // tab_blocks
            cps = [pltpu.make_async_copy(mk32_ref, kd32_ref.at[sb, head], msem.at[0]),
                   pltpu.make_async_copy(mv32_ref, vd32_ref.at[sb, head], msem.at[1]),
                   pltpu.make_async_copy(mckv_ref, ckv_ref.at[sb, head], msem.at[2])]
            for cp in cps:
                cp.start()
            for cp in cps:
                cp.wait()

        @pl.when(jnp.logical_and(i == n - 1, i >= 1))
        def _drain_previous():
            for cp in tile_copies(i - 1, 1 - slot):
                cp.wait()

        @pl.when(i == n - 1)
        def _drain_last():
            for cp in tile_copies(i, slot):
                cp.wait()


def _inproj(x, tab, g1, win, qg, wuq, kvg, wukv, *, tm, tab_blocks, dims, batch=None, meta=None, attn_tile=None):
    m, d = x.shape
    c_qk, c_v, c_ql, c_kvl = dims
    assert m % tm == 0
    prompt = batch is not None
    row = lambda w: pl.BlockSpec((tm, w), lambda i: (i, 0))
    hw = DA_HEADS * VT_W
    extra_in, extra_specs, scratch = [], [], []
    if prompt:
        seq = tab_blocks * tm
        assert m == batch * seq
        col = lambda w: pl.BlockSpec((None, w, tm), lambda i: (i // tab_blocks, 0, i % tab_blocks))
        hbm = pl.BlockSpec(memory_space=pl.ANY)
        tall = (batch, N_META + seq)
        extra_in = list(meta)
        extra_specs = [_const_spec(a.shape) for a in meta]
        scratch = [pltpu.VMEM((2, tm, DA_HEADS, HEAD_W), F32), pltpu.VMEM((2, tm, DA_HEADS, HEAD_W), F32),
                   pltpu.VMEM((2, tm, c_kvl), F32), pltpu.SemaphoreType.DMA((2, 3)), pltpu.SemaphoreType.DMA((3,))]
        assert attn_tile % tm == 0 and seq % attn_tile == 0
        per = attn_tile // tm
        vcol = pl.BlockSpec((None, None, hw, tm), lambda i: (i // tab_blocks, (i % tab_blocks) // per, 0,
                                                              (i % tab_blocks) % per))
        vshape = jax.ShapeDtypeStruct((batch, seq // attn_tile, hw, attn_tile), BF16)
        out_specs = [col(c_qk), hbm, row(c_qk), hbm, vcol, col(MLA_HEADS * MLA_W), hbm,
                     row(MLA_ROPE), row(MLA_HEADS * MLA_W), vcol]
        out_shape = [
            jax.ShapeDtypeStruct((batch, c_qk, seq), BF16),
            jax.ShapeDtypeStruct(tall + (DA_HEADS, HEAD_W), F32),
            jax.ShapeDtypeStruct((m, c_qk), BF16),
            jax.ShapeDtypeStruct(tall + (DA_HEADS, HEAD_W), F32),
            vshape,
            jax.ShapeDtypeStruct((batch, MLA_HEADS * MLA_W, seq), BF16),
            jax.ShapeDtypeStruct(tall + (c_kvl,), F32),
            jax.ShapeDtypeStruct((m, MLA_ROPE), F32),
            jax.ShapeDtypeStruct((m, MLA_HEADS * MLA_W), BF16),
            vshape,
        ]
    else:
        heads32 = pl.BlockSpec((tm, DA_HEADS, HEAD_W), lambda i: (i, 0, 0))
        out_specs = [row(c_qk), heads32, row(c_qk), heads32, row(c_v), row(MLA_HEADS * MLA_W), row(c_kvl),
                     row(MLA_ROPE), row(MLA_HEADS * MLA_W), row(MLA_HEADS * MLA_V)]
        out_shape = [
            jax.ShapeDtypeStruct((m, c_qk), BF16),
            jax.ShapeDtypeStruct((m, DA_HEADS, HEAD_W), F32),
            jax.ShapeDtypeStruct((m, c_qk), BF16),
            jax.ShapeDtypeStruct((m, DA_HEADS, HEAD_W), F32),
            jax.ShapeDtypeStruct((m, c_v), BF16),
            jax.ShapeDtypeStruct((m, MLA_HEADS * MLA_W), BF16),
            jax.ShapeDtypeStruct((m, c_kvl), F32),
            jax.ShapeDtypeStruct((m, MLA_ROPE), F32),
            jax.ShapeDtypeStruct((m, MLA_HEADS * MLA_W), BF16),
            jax.ShapeDtypeStruct((m, MLA_HEADS * MLA_V), BF16),
        ]
    return pl.pallas_call(
        functools.partial(_inproj_kernel, c_qk=c_qk, c_v=c_v, c_ql=c_ql, c_kvl=c_kvl, tab_blocks=tab_blocks,
                          prompt=prompt),
        grid=(m // tm,),
        in_specs=[
            row(d),
            pl.BlockSpec((tm, LANES), lambda i: (i % tab_blocks, 0)),
            _const_spec(g1.shape), _const_spec(win.shape), _const_spec(qg.shape),
            _const_spec(wuq.shape), _const_spec(kvg.shape), _const_spec(wukv.shape),
        ] + extra_specs,
        out_specs=out_specs,
        out_shape=out_shape,
        scratch_shapes=scratch,
        compiler_params=_cparams(1),
        name="inproj",
    )(x, tab, g1, win, qg, wuq, kvg, wukv, *extra_in)


def _softmax_seed(s, v, m_ref, l_ref, acc_ref, j):
    m = jnp.max(s, axis=1, keepdims=True)
    p = jnp.exp2(s - m)
    m_ref[j] = m
    l_ref[j] = jnp.sum(p, axis=1, keepdims=True)
    acc_ref[j] = _dot(p.astype(BF16), v)


def _softmax_step(s, v, m_ref, l_ref, acc_ref, j):
    m_old = m_ref[j]
    m_new = jnp.maximum(m_old, jnp.max(s, axis=1, keepdims=True))
    alpha = jnp.exp2(m_old - m_new)
    p = jnp.exp2(s - m_new)
    l_ref[j] = alpha * l_ref[j] + jnp.sum(p, axis=1, keepdims=True)
    acc_ref[j] = alpha * acc_ref[j] + _dot(p.astype(BF16), v)
    m_ref[j] = m_new


def _diff_lambda(lamv, lam_init):
    a = jnp.sum(lamv[0:1] * lamv[1:2], axis=1, keepdims=True)
    b = jnp.sum(lamv[2:3] * lamv[3:4], axis=1, keepdims=True)
    return jnp.exp(a) - jnp.exp(b) + lam_init


def _split_maps(q):
    lane = lax.broadcasted_iota(jnp.int32, q.shape, 1)
    zero = jnp.zeros_like(q)
    return jnp.where(lane < DA_D, q, zero), jnp.where(lane >= DA_D, q, zero)


def _init_state(m_ref, acc_ref):
    m_ref[...] = jnp.full(m_ref.shape, NEG_BIG, F32)
    acc_ref[...] = jnp.zeros(acc_ref.shape, F32)


def _step_t(st, shift, vt, m_ref, acc_ref, j):
    m_old = m_ref[j]
    m_new = jnp.maximum(m_old, jnp.max(st, axis=0, keepdims=True) + shift)
    p = jnp.exp2(st - (m_new - shift))
    acc_ref[j] = jnp.exp2(m_old - m_new) * acc_ref[j] + _dot(vt, p.astype(BF16))
    m_ref[j] = m_new


def _diff_attn_parts(qt_ref, k_ref, vt_ref, mk_ref, mvt_ref, pos_ref, cq_ref, corr_ref,
                     lamv_ref, g_ref, o_ref, m_s, acc_s, *, tq, lam_init):
    qrow = lax.broadcasted_iota(jnp.int32, (HEAD_W, tq), 0) < DA_D
    klane = lax.broadcasted_iota(jnp.int32, (tq, HEAD_W), 1) < DA_D

    def init():
        _init_state(m_s, acc_s)

    def body(diag, sub, tiles_back):
        qpos = (lax.broadcasted_iota(jnp.int32, (1, tq), 1) + tiles_back * tq).astype(F32)
        pos = pos_ref[...]
        mlane = lax.broadcasted_iota(jnp.int32, (N_META, HEAD_W), 1) < DA_D
        krows = pl.ds(pl.multiple_of(sub * tq, tq), tq)

        def scores(h):
            hs = slice(h * HEAD_W, (h + 1) * HEAD_W)
            qt = qt_ref[hs, :]
            cq = cq_ref[h]
            kk = k_ref[krows, hs]
            corr = corr_ref[...] * (_alibi_slope(h) * LOG2E) if diag else None
            mk = mk_ref[:, hs]
            mzero = jnp.zeros_like(mk)
            out = []
            for c in range(2):
                qc = jnp.where(qrow, qt, cq) if c == 0 else jnp.where(qrow, cq, qt)
                kc = jnp.where(klane, kk, pos) if c == 0 else jnp.where(klane, pos, kk)
                if diag:
                    mc = jnp.where(mlane, mk, mzero) if c == 0 else jnp.where(mlane, mzero, mk)
                    kc = jnp.concatenate([kc, mc], axis=0)
                st = _dot(kc, qc)
                out.append(st + corr if diag else st)
            return out

        queue = [scores(h) for h in range(SCORE_LOOKAHEAD)]
        for h in range(DA_HEADS):
            cur = queue.pop(0)
            if h + SCORE_LOOKAHEAD < DA_HEADS:
                queue.append(scores(h + SCORE_LOOKAHEAD))
            shift = qpos * (-(_alibi_slope(h) * LOG2E))
            vt = vt_ref[sub, h * VT_W:(h + 1) * VT_W, :]
            if diag:
                vt = jnp.concatenate([vt, mvt_ref[h * VT_W:(h + 1) * VT_W, :]], axis=1)
            for c in range(2):
                _step_t(cur[c], shift, vt, m_s, acc_s, 2 * h + c)

    def finish():
        lam = _diff_lambda(lamv_ref[...], lam_init)
        g = g_ref[...]
        for h in range(DA_HEADS):
            a0 = acc_s[2 * h]
            a1 = acc_s[2 * h + 1]
            ot = a0[0:DA_V] / a0[DA_V:DA_V + 1] - lam * (a1[0:DA_V] / a1[DA_V:DA_V + 1])
            ot = ot * lax.rsqrt(jnp.mean(ot * ot, axis=0, keepdims=True) + EPS) * g * (1.0 - lam_init)
            o_ref[:, h * HEAD_W:(h + 1) * HEAD_W] = ot.T.astype(BF16)

    return init, body, finish


def _mla_attn_parts(qt_ref, k_ref, vt_ref, mk_ref, mvt_ref, mask_ref, o_ref, m_s, acc_s, *, tq):
    def init():
        _init_state(m_s, acc_s)

    def body(diag, sub):
        krows = pl.ds(pl.multiple_of(sub * tq, tq), tq)

        def scores(h):
            kk = k_ref[krows, h * MLA_W:(h + 1) * MLA_W]
            if diag:
                kk = jnp.concatenate([kk, mk_ref[:, h * MLA_W:(h + 1) * MLA_W]], axis=0)
            st = _dot(kk, qt_ref[h * MLA_W:(h + 1) * MLA_W, :])
            return st + mask_ref[...] if diag else st

        queue = [scores(h) for h in range(SCORE_LOOKAHEAD)]
        for h in range(MLA_HEADS):
            st = queue.pop(0)
            if h + SCORE_LOOKAHEAD < MLA_HEADS:
                queue.append(scores(h + SCORE_LOOKAHEAD))
            vt = vt_ref[sub, h * VT_W:(h + 1) * VT_W, :]
            if diag:
                vt = jnp.concatenate([vt, mvt_ref[h * VT_W:(h + 1) * VT_W, :]], axis=1)
            _step_t(st, 0.0, vt, m_s, acc_s, h)

    def finish():
        for h in range(MLA_HEADS):
            a = acc_s[h]
            o_ref[:, h * MLA_V:(h + 1) * MLA_V] = (a[0:MLA_V] / a[MLA_V:MLA_V + 1]).T.astype(BF16)

    return init, body, finish


N_DIFF_IN = 10
N_MLA_IN = 6


def _prompt_attn_kernel(qi_ref, kb_ref, *refs, tq, lam_init):
    t = pl.program_id(1)
    qi = qi_ref[t]
    first = kb_ref[t] * KEY_TILES
    d_in, refs = refs[:N_DIFF_IN], refs[N_DIFF_IN:]
    m_in, refs = refs[:N_MLA_IN], refs[N_MLA_IN:]
    d_out, m_out, d_m, d_acc, m_m, m_acc = refs
    d_init, d_body, d_finish = _diff_attn_parts(*d_in, d_out, d_m, d_acc, tq=tq, lam_init=lam_init)
    m_init, m_body, m_finish = _mla_attn_parts(*m_in, m_out, m_m, m_acc, tq=tq)

    @pl.when(first == 0)
    def _init():
        d_init()
        m_init()

    def off_diagonal(sub, carry):
        d_body(False, sub, qi - first - sub)
        m_body(False, sub)
        return carry

    lax.fori_loop(0, jnp.clip(qi - first, 0, KEY_TILES), off_diagonal, 0)

    @pl.when(qi - first < KEY_TILES)
    def _diagonal():
        d_body(True, qi - first, 0)
        m_body(True, qi - first)
        d_finish()
        m_finish()


def _pair_tables(nq):
    qi = np.concatenate([np.full((i // KEY_TILES + 1,), i, np.int32) for i in range(nq)])
    kb = np.concatenate([np.arange(i // KEY_TILES + 1, dtype=np.int32) for i in range(nq)])
    return jnp.asarray(qi), jnp.asarray(kb)


def _tile_geometry(tq):
    j = np.arange(tq)[:, None]
    i = np.arange(tq)[None, :]
    visible = (j // CHUNK) <= (i // CHUNK)
    return i, j, visible


def _bf16_split3(x):
    parts = []
    for _ in range(3):
        p = float(np.asarray(x, np.float32).astype(BF16).astype(np.float32))
        parts.append(p)
        x = x - p
    return parts


def _alibi_operands(tq):
    assert tq <= 2 * MXU_DIM
    j = np.arange(tq)
    jlo = (j % MXU_DIM).astype(np.float32)
    jhi = (j - j % MXU_DIM).astype(np.float32)
    pos = np.zeros((tq, HEAD_W), np.float32)
    cq = np.zeros((DA_HEADS, HEAD_W, tq), np.float32)
    for base in (0, DA_D):
        for r in range(3):
            pos[:, base + 2 * r] = jlo
            pos[:, base + 2 * r + 1] = jhi
    for h in range(DA_HEADS):
        parts = _bf16_split3(_alibi_slope(h) * LOG2E)
        for base in (0, DA_D):
            for r in range(3):
                cq[h, base + 2 * r, :] = parts[r]
                cq[h, base + 2 * r + 1, :] = parts[r]
    return jnp.asarray(pos, BF16), jnp.asarray(cq, BF16)


def _prompt_attn_specs(tq, wq, wk, wv):
    qt_spec = pl.BlockSpec((None, wq, tq), lambda b, t, qi, ki: (b, 0, qi[t]))
    k_spec = pl.BlockSpec((None, KEY_TILES * tq, wk), lambda b, t, qi, kb: (b, kb[t], 0))
    vt_spec = pl.BlockSpec((None, KEY_TILES, wv, tq), lambda b, t, qi, kb: (b, kb[t], 0, 0))
    mk_spec = pl.BlockSpec((N_META, wk), lambda b, t, qi, ki: (0, 0))
    mvt_spec = pl.BlockSpec((wv, N_META), lambda b, t, qi, ki: (0, 0))
    return qt_spec, k_spec, vt_spec, mk_spec, mvt_spec


def _prompt_attn(qt, k, vt, mk, mvt, lamv, g, m_qt, m_k, m_vt, m_mk, m_mvt, *, tq, lam_init):
    b, s, w = k.shape
    nq = s // tq
    assert nq % KEY_TILES == 0
    qi, ki = _pair_tables(nq)
    i, j, visible = _tile_geometry(tq)
    corr = np.where(visible, np.where(j > i, -2.0 * (j - i), 0.0), NEG_BIG)
    corr = jnp.asarray(np.concatenate([corr, np.broadcast_to(i, (N_META, tq))]).astype(np.float32))
    pos, cq = _alibi_operands(tq)
    mask = np.concatenate([np.where(visible, 0.0, NEG_BIG), np.zeros((N_META, tq))])
    mask = jnp.asarray(mask.astype(np.float32))
    full = lambda a: pl.BlockSpec(a.shape, lambda b_, t, qi_, ki_: (0,) * a.ndim)
    d_args = [qt, k, vt, mk, mvt, pos, cq, corr, lamv, g]
    d_specs = list(_prompt_attn_specs(tq, w, w, vt.shape[2])) + [full(a) for a in d_args[5:]]
    m_args = [m_qt, m_k, m_vt, m_mk, m_mvt, mask]
    wq = m_k.shape[2]
    m_specs = list(_prompt_attn_specs(tq, wq, wq, m_vt.shape[2])) + [full(mask)]
    assert len(d_args) == N_DIFF_IN and len(m_args) == N_MLA_IN
    wo = MLA_HEADS * MLA_V
    out_spec = lambda width: pl.BlockSpec((None, tq, width), lambda b_, t, qi_, ki_: (b_, qi_[t], 0))
    return pl.pallas_call(
        functools.partial(_prompt_attn_kernel, tq=tq, lam_init=lam_init),
        grid_spec=pltpu.PrefetchScalarGridSpec(
            num_scalar_prefetch=2,
            grid=(b, int(qi.shape[0])),
            in_specs=d_specs + m_specs,
            out_specs=[out_spec(w), out_spec(wo)],
            scratch_shapes=[pltpu.VMEM((2 * DA_HEADS, 1, tq), F32), pltpu.VMEM((2 * DA_HEADS, VT_W, tq), F32),
                            pltpu.VMEM((MLA_HEADS, 1, tq), F32), pltpu.VMEM((MLA_HEADS, VT_W, tq), F32)],
        ),
        out_shape=[jax.ShapeDtypeStruct((b, s, w), BF16), jax.ShapeDtypeStruct((b, s, wo), BF16)],
        compiler_params=_cparams(2),
        name="prompt_attn",
    )(qi, ki, *d_args, *m_args)


def _sample_diff_kernel(q_ref, kc_hbm, vc_hbm, kn_ref, vn_ref, dc_ref, dn_ref, lamv_ref, g_ref,
                        o_ref, kbuf, vbuf, sem, m_s, l_s, acc_s, *, tk, lam_init):
    kt = pl.program_id(1)
    nkt = pl.num_programs(1)
    step = pl.program_id(0) * nkt + kt
    n_steps = pl.num_programs(0) * nkt

    def tile_copies(s, slot):
        sb = s // nkt
        rows = pl.ds((s % nkt) * tk, tk)
        cps = []
        for h in range(DA_HEADS):
            cps.append(pltpu.make_async_copy(kc_hbm.at[sb, rows, h, :], kbuf.at[slot, h], sem.at[slot, 0]))
            cps.append(pltpu.make_async_copy(vc_hbm.at[sb, rows, h, :], vbuf.at[slot, h], sem.at[slot, 1]))
        return cps

    @pl.when(step == 0)
    def _prime():
        for cp in tile_copies(0, 0):
            cp.start()

    @pl.when(step + 1 < n_steps)
    def _prefetch():
        for cp in tile_copies(step + 1, (step + 1) % 2):
            cp.start()

    slot = step % 2
    for cp in tile_copies(step, slot):
        cp.wait()

    def heads(get_k, get_v, dist, first):
        for h in range(DA_HEADS):
            q1, q2 = _split_maps(q_ref[:, h * HEAD_W:(h + 1) * HEAD_W])
            qq = jnp.concatenate([q1, q2], axis=0)
            s = _dot_nt(qq, get_k(h)) + dist * (-_alibi_slope(h) * LOG2E)
            if first:
                _softmax_seed(s, get_v(h), m_s, l_s, acc_s, h)
            else:
                _softmax_step(s, get_v(h), m_s, l_s, acc_s, h)

    cache_k = lambda h: kbuf[slot, h].astype(BF16)
    cache_v = lambda h: vbuf[slot, h].astype(BF16)

    @pl.when(kt == 0)
    def _first():
        heads(cache_k, cache_v, dc_ref[...], True)

    @pl.when(kt > 0)
    def _rest():
        heads(cache_k, cache_v, dc_ref[...], False)

    @pl.when(kt == nkt - 1)
    def _finish():
        heads(lambda h: kn_ref[:, h * HEAD_W:(h + 1) * HEAD_W], lambda h: vn_ref[:, h * HEAD_W:(h + 1) * HEAD_W],
              dn_ref[...], False)
        lam = _diff_lambda(lamv_ref[...], lam_init)
        g = g_ref[...]
        nq = q_ref.shape[0]
        for h in range(DA_HEADS):
            a = acc_s[h] / l_s[h]
            o = a[0:nq] - lam * a[nq:2 * nq]
            o_ref[:, h * HEAD_W:(h + 1) * HEAD_W] = (_rms(o, g) * (1.0 - lam_init)).astype(BF16)


def _sample_diff_attn(q, kc, vc, kn, vn, dist_c, dist_n, lamv, g, *, tk, lam_init):
    bs, nq, w = q.shape
    lc = kc.shape[1]
    assert lc % tk == 0
    full = lambda a: pl.BlockSpec(a.shape, lambda b, t: (0,) * a.ndim)
    per_stream = lambda a: pl.BlockSpec((None,) + a.shape[1:], lambda b, t: (b,) + (0,) * (a.ndim - 1))
    cache = pl.BlockSpec(memory_space=pl.ANY)
    return pl.pallas_call(
        functools.partial(_sample_diff_kernel, tk=tk, lam_init=lam_init),
        grid=(bs, lc // tk),
        in_specs=[per_stream(q), cache, cache, per_stream(kn), per_stream(vn),
                  pl.BlockSpec((None, 2 * nq, tk), lambda b, t: (t, 0, 0)),
                  full(dist_n), full(lamv), full(g)],
        out_specs=per_stream(q),
        out_shape=jax.ShapeDtypeStruct((bs, nq, w), BF16),
        scratch_shapes=[pltpu.VMEM((2, DA_HEADS, tk, HEAD_W), F32), pltpu.VMEM((2, DA_HEADS, tk, HEAD_W), F32),
                        pltpu.SemaphoreType.DMA((2, 2)),
                        pltpu.VMEM((DA_HEADS, 2 * nq, 1), F32), pltpu.VMEM((DA_HEADS, 2 * nq, 1), F32),
                        pltpu.VMEM((DA_HEADS, 2 * nq, DA_V), F32)],
        compiler_params=_cparams(2),
        name="sample_diff_attn",
    )(q, kc, vc, kn, vn, dist_c, dist_n, lamv, g)


def _sample_mla_kernel(q_ref, cc_ref, krc_ref, cn_ref, krn_ref, wuk_ref, wuv_ref, o_ref,
                       ql_s, qr_s, m_s, l_s, acc_s):
    kt = pl.program_id(1)
    nkt = pl.num_programs(1)
    nq = q_ref.shape[0]

    @pl.when(kt == 0)
    def _prep():
        for h in range(MLA_HEADS):
            qn = q_ref[:, h * MLA_W:h * MLA_W + MLA_NOPE]
            ql_s[h * nq:(h + 1) * nq, :] = _dot_nt(qn, wuk_ref[:, h * MLA_NOPE:(h + 1) * MLA_NOPE]).astype(BF16)
            qr_s[h * nq:(h + 1) * nq, :] = q_ref[:, h * MLA_W + MLA_NOPE:(h + 1) * MLA_W]

    def scores(c_ref, kr_ref):
        cb = c_ref[...].astype(BF16)
        krb = kr_ref[...].astype(BF16)
        s = _dot_nt(ql_s[...], cb) + _dot_nt(qr_s[:, 0:MLA_ROPE], krb)
        return s, cb

    @pl.when(kt == 0)
    def _first():
        s, cb = scores(cc_ref, krc_ref)
        _softmax_seed(s, cb, m_s, l_s, acc_s, 0)

    @pl.when(kt > 0)
    def _rest():
        s, cb = scores(cc_ref, krc_ref)
        _softmax_step(s, cb, m_s, l_s, acc_s, 0)

    @pl.when(kt == nkt - 1)
    def _finish():
        s, cb = scores(cn_ref, krn_ref)
        _softmax_step(s, cb, m_s, l_s, acc_s, 0)
        ol = (acc_s[0] / l_s[0]).astype(BF16)
        for h in range(MLA_HEADS):
            o_ref[:, h * MLA_V:(h + 1) * MLA_V] = _dot(
                ol[h * nq:(h + 1) * nq, :], wuv_ref[:, h * MLA_V:(h + 1) * MLA_V]).astype(BF16)


def _sample_mla_attn(q, cc, krc, cn, krn, wuk, wuv, *, tk):
    bs, nq, wq = q.shape
    lc, kvl = cc.shape[1], cc.shape[2]
    assert lc % tk == 0
    full = lambda a: pl.BlockSpec(a.shape, lambda b, t: (0,) * a.ndim)
    per_stream = lambda a: pl.BlockSpec((None,) + a.shape[1:], lambda b, t: (b,) + (0,) * (a.ndim - 1))
    rows = MLA_HEADS * nq
    return pl.pallas_call(
        _sample_mla_kernel,
        grid=(bs, lc // tk),
        in_specs=[per_stream(q),
                  pl.BlockSpec((None, tk, kvl), lambda b, t: (b, t, 0)),
                  pl.BlockSpec((None, tk, MLA_ROPE), lambda b, t: (b, t, 0)),
                  per_stream(cn), per_stream(krn), full(wuk), full(wuv)],
        out_specs=pl.BlockSpec((None, nq, MLA_HEADS * MLA_V), lambda b, t: (b, 0, 0)),
        out_shape=jax.ShapeDtypeStruct((bs, nq, MLA_HEADS * MLA_V), BF16),
        scratch_shapes=[pltpu.VMEM((rows, kvl), BF16), pltpu.VMEM((rows, LANES), BF16),
                        pltpu.VMEM((1, rows, 1), F32), pltpu.VMEM((1, rows, 1), F32),
                        pltpu.VMEM((1, rows, kvl), F32)],
        compiler_params=_cparams(2),
        name="sample_mla_attn",
    )(q, cc, krc, cn, krn, wuk, wuv)


ROUTER_ROWS = SUBLANES * (1 + N_GROUPS)


def _route(lt):
    g = [lt[i:i + 1] for i in range(N_GROUPS)]
    gmax = functools.reduce(jnp.maximum, g)
    gidx = jnp.full_like(gmax, float(N_GROUPS - 1))
    for i in range(N_GROUPS - 2, -1, -1):
        gidx = jnp.where(g[i] == gmax, float(i), gidx)
    den = functools.reduce(lambda a, b: a + b, [jnp.exp(gi - gmax) for gi in g])
    p_top = 1.0 / den
    e = []
    for j in range(EXPERTS_PER_GROUP):
        ej = lt[SUBLANES * N_GROUPS + j:SUBLANES * N_GROUPS + j + 1]
        for grp in range(N_GROUPS - 2, -1, -1):
            ej = jnp.where(gidx == float(grp), lt[SUBLANES * (grp + 1) + j:SUBLANES * (grp + 1) + j + 1], ej)
        e.append(ej)

    def first_argmax(vals):
        vmax = functools.reduce(jnp.maximum, vals)
        idx = jnp.full_like(vmax, float(len(vals) - 1))
        for i in range(len(vals) - 2, -1, -1):
            idx = jnp.where(vals[i] == vmax, float(i), idx)
        return vmax, idx

    v1, i1 = first_argmax(e)
    rest = [jnp.where(i1 == float(j), -jnp.inf, e[j]) for j in range(EXPERTS_PER_GROUP)]
    v2, i2 = first_argmax(rest)
    r = jnp.exp(v2 - v1)
    w1 = p_top / (1.0 + r)
    w2 = p_top * r / (1.0 + r)
    base = gidx * float(EXPERTS_PER_GROUP)
    return w1, w2, base + i1, base + i2


def _merge_kernel(od_ref, om_ref, x_ref, wo_ref, g2_ref, wr_ref, br_ref, hp_ref, xn_ref, rt_ref):
    nd = od_ref.shape[1]
    y = _dot(od_ref[...], wo_ref[0:nd, :]) + _dot(om_ref[...], wo_ref[nd:, :])
    hp = x_ref[...] + y
    hp_ref[...] = hp
    xn = _rms(hp, g2_ref[...])
    xn_ref[...] = _pack_halves(xn)
    lt = _dot_nt(wr_ref[...], xn.astype(BF16)) + br_ref[...]
    rows = _route(lt)
    for i, r in enumerate(rows):
        rt_ref[i:i + 1, :] = r
    rt_ref[4:8, :] = jnp.zeros((4, rt_ref.shape[1]), F32)


def _merge(od, om, x, wo, g2, wr, br, *, tm):
    m, d = x.shape
    row = lambda w: pl.BlockSpec((tm, w), lambda i: (i, 0))
    return pl.pallas_call(
        _merge_kernel,
        grid=(m // tm,),
        in_specs=[row(od.shape[1]), row(om.shape[1]), row(d), _const_spec(wo.shape),
                  _const_spec(g2.shape), _const_spec(wr.shape), _const_spec(br.shape)],
        out_specs=[row(d), row(d // 2), pl.BlockSpec((SUBLANES, tm), lambda i: (0, i))],
        out_shape=[jax.ShapeDtypeStruct((m, d), F32), jax.ShapeDtypeStruct((m, d // 2), jnp.uint32),
                   jax.ShapeDtypeStruct((SUBLANES, m), F32)],
        compiler_params=_cparams(1),
        name="merge",
    )(od, om, x, wo, g2, wr, br)


def _cast_kernel(x_ref, o_ref):
    o_ref[...] = x_ref[...].astype(BF16)


def _cast_bf16(w, *, rows):
    e, r, c = w.shape
    assert r % rows == 0
    spec = pl.BlockSpec((None, rows, c), lambda i, j: (i, j, 0))
    return pl.pallas_call(
        _cast_kernel,
        grid=(e, r // rows),
        in_specs=[spec],
        out_specs=spec,
        out_shape=jax.ShapeDtypeStruct(w.shape, BF16),
        compiler_params=_cparams(2),
        name="cast_bf16",
    )(w)


def _swiglu(xp, wg_ref, wu_ref, wd_ref):
    lo, hi = _unpack_halves(xp)
    lo = lo.astype(BF16)
    hi = hi.astype(BF16)
    w = lo.shape[1]
    g = _dot(lo, wg_ref[0:w, :]) + _dot(hi, wg_ref[w:, :])
    u = _dot(lo, wu_ref[0:w, :]) + _dot(hi, wu_ref[w:, :])
    h = (g * jax.nn.sigmoid(g) * u).astype(BF16)
    return _dot(h, wd_ref[...])


def _moe_sorted_kernel(te_ref, nu_ref, x_ref, w_ref, wg_ref, wu_ref, wd_ref, *rest, out_first, x_first, x_tiles):
    y_ref = rest[-1]
    g = pl.program_id(0) + out_first
    live = jnp.logical_and(g < nu_ref[0], jnp.logical_and(g >= x_first, g < x_first + x_tiles))

    @pl.when(live)
    def _():
        y_ref[...] = _pack_halves(w_ref[...] * _swiglu(x_ref[...], wg_ref, wu_ref, wd_ref))

    @pl.when(jnp.logical_not(live))
    def _():
        y_ref[...] = jnp.zeros_like(y_ref)


def _moe_sorted(tile_expert, n_used, xs, ws, wg, wu, wd, *, tm, x_first, y_prev=None):
    n_x, dh = xs.shape
    d = 2 * dh
    f = wg.shape[2]
    x_tiles = n_x // tm
    n_tiles = ws.shape[0] // tm
    out_first = 0 if y_prev is None else x_first
    grid_tiles = n_tiles if y_prev is None else x_tiles
    local = lambda i: jnp.clip(i + out_first - x_first, 0, x_tiles - 1)
    in_specs = [pl.BlockSpec((tm, dh), lambda i, te, nu: (local(i), 0)),
                pl.BlockSpec((tm, 1), lambda i, te, nu: (i + out_first, 0)),
                pl.BlockSpec((None, d, f), lambda i, te, nu: (te[local(i) + x_first], 0, 0)),
                pl.BlockSpec((None, d, f), lambda i, te, nu: (te[local(i) + x_first], 0, 0)),
                pl.BlockSpec((None, f, d), lambda i, te, nu: (te[local(i) + x_first], 0, 0))]
    args = [tile_expert, n_used, xs, ws, wg, wu, wd]
    aliases = {}
    if y_prev is not None:
        in_specs.append(pl.BlockSpec(memory_space=pl.ANY))
        args.append(y_prev)
        aliases = {len(args) - 1: 0}
    return pl.pallas_call(
        functools.partial(_moe_sorted_kernel, out_first=out_first, x_first=x_first, x_tiles=x_tiles),
        grid_spec=pltpu.PrefetchScalarGridSpec(
            num_scalar_prefetch=2,
            grid=(grid_tiles,),
            in_specs=in_specs,
            out_specs=pl.BlockSpec((tm, dh), lambda i, te, nu: (i + out_first, 0)),
        ),
        out_shape=jax.ShapeDtypeStruct((n_tiles * tm, dh), jnp.uint32),
        input_output_aliases=aliases,
        compiler_params=_cparams(1),
        name="moe_sorted",
    )(*args)


def _moe_dense_kernel(x_ref, hp_ref, gates_ref, wg_ref, wu_ref, wd_ref, gf_ref, o_ref, acc_s):
    e = pl.program_id(0)

    @pl.when(e == 0)
    def _():
        acc_s[...] = jnp.zeros_like(acc_s)

    lane = lax.broadcasted_iota(jnp.int32, gates_ref.shape, 1)
    gate = jnp.sum(jnp.where(lane == e, gates_ref[...], 0.0), axis=1, keepdims=True)
    acc_s[...] += gate * _swiglu(x_ref[...], wg_ref, wu_ref, wd_ref)

    @pl.when(e == pl.num_programs(0) - 1)
    def _():
        o_ref[...] = _rms(hp_ref[...] + acc_s[...], gf_ref[...])


def _moe_dense(xn, hp, gates, wg, wu, wd, gf):
    m, d = hp.shape
    ne, _, f = wg.shape
    full = lambda a: pl.BlockSpec(a.shape, lambda e: (0,) * a.ndim)
    return pl.pallas_call(
        _moe_dense_kernel,
        grid=(ne,),
        in_specs=[full(xn), full(hp), full(gates),
                  pl.BlockSpec((None, d, f), lambda e: (e, 0, 0)),
                  pl.BlockSpec((None, d, f), lambda e: (e, 0, 0)),
                  pl.BlockSpec((None, f, d), lambda e: (e, 0, 0)),
                  full(gf)],
        out_specs=full(hp),
        out_shape=jax.ShapeDtypeStruct((m, d), F32),
        scratch_shapes=[pltpu.VMEM((m, d), F32)],
        compiler_params=_cparams(1),
        name="moe_dense",
    )(xn, hp, gates, wg, wu, wd, gf)


def _combine_kernel(hp_ref, y1_ref, y2_ref, gf_ref, o_ref):
    w = y1_ref.shape[1]
    a_lo, a_hi = _unpack_halves(y1_ref[...])
    b_lo, b_hi = _unpack_halves(y2_ref[...])
    s_lo = hp_ref[:, 0:w] + (a_lo + b_lo)
    s_hi = hp_ref[:, w:] + (a_hi + b_hi)
    ms = (jnp.sum(s_lo * s_lo, axis=-1, keepdims=True) + jnp.sum(s_hi * s_hi, axis=-1, keepdims=True)) / (2 * w)
    inv = lax.rsqrt(ms + EPS)
    o_ref[:, 0:w] = s_lo * inv * gf_ref[:, 0:w]
    o_ref[:, w:] = s_hi * inv * gf_ref[:, w:]


def _combine(hp, y1, y2, gf, *, tm):
    m, d = hp.shape
    row = pl.BlockSpec((tm, d), lambda i: (i, 0))
    half = pl.BlockSpec((tm, d // 2), lambda i: (i, 0))
    return pl.pallas_call(
        _combine_kernel,
        grid=(m // tm,),
        in_specs=[row, half, half, _const_spec(gf.shape)],
        out_specs=row,
        out_shape=jax.ShapeDtypeStruct((m, d), F32),
        compiler_params=_cparams(1),
        name="combine",
    )(hp, y1, y2, gf)


def _rope_table(pos):
    half = MLA_ROPE // 2
    inv_freq = ROPE_THETA ** (-jnp.arange(half, dtype=F32) / half)
    ang = pos.astype(F32)[:, None] * inv_freq[None, :]
    c, s = jnp.cos(ang), jnp.sin(ang)
    return jnp.concatenate([c, c, -s, s], axis=1)


def _swap_halves(w):
    half = MLA_ROPE // 2
    return jnp.concatenate([w[..., half:], w[..., :half]], axis=-1)


def _values_t(v):
    n = v.shape[0]
    vt = v.reshape(n, DA_HEADS, DA_V).transpose(1, 2, 0)
    return jnp.concatenate([vt, jnp.ones((DA_HEADS, BF16_ROWS, n), v.dtype)], axis=1).reshape(DA_HEADS * VT_W, n)


def _sort_by_expert(eid, w, tm):
    t = eid.shape[1]
    flat_e = eid.reshape(-1)
    onehot = (flat_e[:, None] == jnp.arange(N_EXPERTS, dtype=jnp.int32)[None, :]).astype(jnp.int32)
    rank = jnp.sum((jnp.cumsum(onehot, axis=0) - onehot) * onehot, axis=1)
    counts = jnp.sum(onehot, axis=0)
    tiles_per = (counts + tm - 1) // tm
    tiles_end = jnp.cumsum(tiles_per)
    row_start = (tiles_end - tiles_per) * tm
    pos = row_start[flat_e] + rank
    n_tiles = (2 * t) // tm + N_EXPERTS
    slot_a = jnp.full((n_tiles * tm,), -1, jnp.int32).at[pos].set(jnp.arange(2 * t, dtype=jnp.int32),
                                                                   unique_indices=True, mode="promise_in_bounds")
    used = slot_a >= 0
    safe_a = jnp.maximum(slot_a, 0)
    sorted_tok = jnp.where(used, safe_a % t, 0)
    sorted_w = jnp.where(used, w.reshape(-1).at[safe_a].get(mode="promise_in_bounds"), 0.0)
    tile_ids = jnp.arange(n_tiles, dtype=jnp.int32)
    tile_expert = jnp.minimum(jnp.sum((tiles_end[None, :] <= tile_ids[:, None]).astype(jnp.int32), axis=1),
                              N_EXPERTS - 1)
    n_used = tiles_end[-1:].astype(jnp.int32)
    return pos.reshape(2, t), sorted_tok, sorted_w, tile_expert, n_used


def kernel(x_prompt, x_sample, cache_diff_k, cache_diff_v, cache_mla_ckv, cache_mla_kr, meta_tokens, norm1_g, w_in, diff_lam_q1, diff_lam_k1, diff_lam_q2, diff_lam_k2, diff_subln_g, mla_q_norm_g, mla_w_uq, mla_kv_norm_g, mla_w_uk, mla_w_uv, w_o, norm2_g, router_group_w, router_group_b, router_expert_w, router_expert_b, expert_w_gate, expert_w_up, expert_w_down, final_norm_g):
    depth = norm1_g.shape[0]
    assert depth == 1, "single-layer step only"
    assert MLA_HEADS == DA_HEADS and MLA_V == DA_V
    lam_init = 0.8 - 0.6 * math.exp(-0.3 * 0)
    b, s, d = x_prompt.shape
    bs, ss, _ = x_sample.shape
    past = cache_mla_kr.shape[2]
    lc = N_META + past
    c_qk = DA_HEADS * 2 * DA_D
    c_v = DA_HEADS * DA_V
    c_ql = mla_q_norm_g.shape[1]
    c_kvl = mla_kv_norm_g.shape[1]
    o5 = 2 * c_qk + c_v + c_ql + c_kvl

    win = w_in[0]
    win_ext = jnp.concatenate([win, _swap_halves(win[:, o5:])], axis=1).astype(BF16)
    wuq = mla_w_uq[0].reshape(c_ql, MLA_HEADS, MLA_NOPE + MLA_ROPE)
    wuq_n = wuq[:, :, :MLA_NOPE].reshape(c_ql, MLA_HEADS * MLA_NOPE)
    wuq_r = jnp.concatenate([wuq[:, :, MLA_NOPE:], _swap_halves(wuq[:, :, MLA_NOPE:])], axis=2)
    wuq_ext = jnp.concatenate([wuq_n, wuq_r.reshape(c_ql, MLA_HEADS * LANES)], axis=1).astype(BF16)
    wuk = mla_w_uk[0].astype(BF16)
    wuv = mla_w_uv[0].astype(BF16)
    wukv = jnp.concatenate([wuk, wuv], axis=1)
    wo = w_o[0].astype(BF16)
    wr = jnp.zeros((ROUTER_ROWS, d), F32).at[0:N_GROUPS].set(router_group_w[0].T)
    br = jnp.zeros((ROUTER_ROWS, 1), F32).at[0:N_GROUPS, 0].set(router_group_b[0])
    rew = router_expert_w[0].T.reshape(N_GROUPS, EXPERTS_PER_GROUP, d)
    reb = router_expert_b[0].reshape(N_GROUPS, EXPERTS_PER_GROUP)
    for grp in range(N_GROUPS):
        wr = wr.at[SUBLANES * (grp + 1):SUBLANES * (grp + 1) + EXPERTS_PER_GROUP].set(rew[grp])
        br = br.at[SUBLANES * (grp + 1):SUBLANES * (grp + 1) + EXPERTS_PER_GROUP, 0].set(reb[grp])
    wr = wr.astype(BF16)
    wg = _cast_bf16(expert_w_gate[0], rows=min(CAST_ROWS, expert_w_gate.shape[2]))
    wu = _cast_bf16(expert_w_up[0], rows=min(CAST_ROWS, expert_w_up.shape[2]))
    wd = _cast_bf16(expert_w_down[0], rows=min(CAST_ROWS, expert_w_down.shape[2]))
    gf = final_norm_g[None, :]
    lamv = jnp.stack([diff_lam_q1[0], diff_lam_k1[0], diff_lam_q2[0], diff_lam_k2[0]])
    subg = diff_subln_g

    dims = (c_qk, c_v, c_ql, c_kvl)
    inproj = functools.partial(_inproj, g1=norm1_g, win=win_ext, qg=mla_q_norm_g, wuq=wuq_ext,
                               kvg=mla_kv_norm_g, wukv=wukv, dims=dims)

    (_, mdk32, mdk, mdv32, mdv, _, mckv, _, mkm, mvm) = inproj(
        meta_tokens, jnp.zeros((N_META, LANES), F32), tm=N_META, tab_blocks=1)

    ts = bs * ss
    s_pos = past + jnp.arange(ss, dtype=jnp.int32)
    (sqd, s_dk, skd, s_dv, svd, sqm, sckv, skr, _, _) = inproj(
        x_sample.reshape(ts, d), _rope_table(s_pos), tm=ss, tab_blocks=1)
    q3 = lambda a: a.reshape(bs, ss, a.shape[-1])
    kpos_c = np.arange(lc) - N_META
    dist_c = np.where(kpos_c[None, :] >= 0, np.abs(past + np.arange(ss)[:, None] - kpos_c[None, :]), 0)
    dist_n = np.abs(np.arange(ss)[:, None] - np.arange(ss)[None, :])
    tk_s = lc // 2 if (lc // 2) % SUBLANES == 0 and lc % 2 == 0 else lc
    dist_c = np.tile(dist_c, (2, 1)).astype(np.float32).reshape(2 * ss, lc // tk_s, tk_s)
    dist_c = jnp.asarray(np.moveaxis(dist_c, 1, 0))
    dist_n = jnp.asarray(np.tile(dist_n, (2, 1)).astype(np.float32))
    sod = _sample_diff_attn(q3(sqd), cache_diff_k[0], cache_diff_v[0], q3(skd), q3(svd), dist_c, dist_n, lamv, subg,
                            tk=tk_s, lam_init=lam_init)
    krc = jnp.concatenate([jnp.zeros((bs, N_META, MLA_ROPE), F32), cache_mla_kr[0]], axis=1)
    som = _sample_mla_attn(q3(sqm), cache_mla_ckv[0], krc, q3(sckv), q3(skr), wuk, wuv, tk=tk_s)
    hs, xn2s, rts = _merge(sod.reshape(ts, -1), som.reshape(ts, -1), x_sample.reshape(ts, d), wo, norm2_g, wr, br,
                           tm=ts)
    eids = rts[2:4].astype(jnp.int32)
    gates = (jnp.where(eids[0][:, None] == jnp.arange(LANES)[None, :], rts[0][:, None], 0.0)
             + jnp.where(eids[1][:, None] == jnp.arange(LANES)[None, :], rts[1][:, None], 0.0))
    y_sample = _moe_dense(xn2s, hs, gates, wg, wu, wd, gf).reshape(bs, ss, d)

    tm_p = min(INPROJ_ROWS, s)
    tq = min(ATTN_TILE, s)
    tab_p = _rope_table(jnp.arange(s, dtype=jnp.int32))
    (pqdt, p_dk, pkd, p_dv, pvdt, pqmt, p_ckv, pkr, pkm, pvmt) = inproj(
        x_prompt.reshape(b * s, d), tab_p, tm=tm_p, tab_blocks=s // tm_p, batch=b, meta=(mdk32, mdv32, mckv),
        attn_tile=tq)
    r3 = lambda a: a.reshape(b, s, a.shape[-1])
    od, om = _prompt_attn(pqdt, r3(pkd), pvdt, mdk, _values_t(mdv), lamv, subg.T,
                          pqmt, r3(pkm), pvmt, mkm, _values_t(mvm), tq=tq, lam_init=lam_init)
    t = b * s
    tm_t = min(TOKEN_ROWS, t)
    hp, xn2, rt = _merge(od.reshape(t, -1), om.reshape(t, -1), x_prompt.reshape(t, d), wo, norm2_g, wr, br, tm=tm_t)
    pos, sorted_tok, sorted_w, tile_expert, n_used = _sort_by_expert(rt[2:4].astype(jnp.int32), rt[0:2], tm_t)
    rows = lambda a, idx: a.at[idx].get(mode="promise_in_bounds")
    n_tiles = sorted_tok.shape[0] // tm_t
    cut = (n_tiles // 2) * tm_t
    sw = sorted_w[:, None]
    ys = _moe_sorted(tile_expert, n_used, rows(xn2, sorted_tok[:cut]), sw, wg, wu, wd, tm=tm_t, x_first=0)
    if cut < n_tiles * tm_t:
        ys = _moe_sorted(tile_expert, n_used, rows(xn2, sorted_tok[cut:]), sw, wg, wu, wd, tm=tm_t,
                         x_first=cut // tm_t, y_prev=ys)
    y_prompt = _combine(hp, rows(ys, pos[0]), rows(ys, pos[1]), gf, tm=tm_t).reshape(b, s, d)

    return (y_prompt, y_sample,
            p_dk[None], p_dv[None], p_ckv[None], pkr.reshape(1, b, s, MLA_ROPE),
            s_dk.reshape(1, bs, ss, DA_HEADS, 2 * DA_D), s_dv.reshape(1, bs, ss, DA_HEADS, DA_V),
            sckv.reshape(1, bs, ss, c_kvl), skr.reshape(1, bs, ss, MLA_ROPE))
```

```python
import functools
import math

import numpy as np
import jax
import jax.numpy as jnp
from jax import lax
from jax.experimental import pallas as pl
from jax.experimental.pallas import tpu as pltpu

CHUNK = 64
N_META = 16
EPS = 1e-6
DA_HEADS = 8
DA_D = 64
DA_V = 2 * DA_D
MLA_HEADS = 8
MLA_NOPE = 128
MLA_ROPE = 64
MLA_V = 128
ROPE_THETA = 10000.0
MLA_SCALE = (MLA_NOPE + MLA_ROPE) ** -0.5
N_GROUPS = 4
EXPERTS_PER_GROUP = 4
N_EXPERTS = N_GROUPS * EXPERTS_PER_GROUP
LOG2E = math.log2(math.e)
LANES = 128
SUBLANES = 8
BF16_ROWS = 16
MXU_DIM = 256
HEAD_W = 128
MLA_W = 256
VT_W = DA_V + BF16_ROWS
NEG_BIG = -1e30
VMEM_LIMIT = 56 * 1024 * 1024
ATTN_TILE = 512
SCORE_LOOKAHEAD = 2
INPROJ_ROWS = 256
TOKEN_ROWS = 512
CAST_ROWS = 1024

BF16 = jnp.bfloat16
F32 = jnp.float32


def _dot(a, b):
    return jnp.dot(a, b, preferred_element_type=F32)


def _dot_nt(a, b):
    return lax.dot_general(a, b, (((1,), (1,)), ((), ())), preferred_element_type=F32)


def _rms(x, g):
    return x * lax.rsqrt(jnp.mean(x * x, axis=-1, keepdims=True) + EPS) * g


def _pack_halves(x):
    w = x.shape[1] // 2
    bits = lax.bitcast_convert_type(x.astype(BF16).astype(F32), jnp.uint32)
    return (bits[:, :w] >> 16) | (bits[:, w:] & jnp.uint32(0xFFFF0000))


def _unpack_halves(u):
    lo = lax.bitcast_convert_type(u << 16, F32)
    hi = lax.bitcast_convert_type(u & jnp.uint32(0xFFFF0000), F32)
    return lo, hi


def _cparams(n_axes):
    return pltpu.CompilerParams(dimension_semantics=("arbitrary",) * n_axes,
                                vmem_limit_bytes=VMEM_LIMIT)


def _const_spec(shape):
    nd = len(shape)
    return pl.BlockSpec(shape, lambda *_: (0,) * nd, pipeline_mode=pl.Buffered(1))


def _alibi_slope(h):
    return 2.0 ** (-8.0 * (h + 1) / DA_HEADS)


def _inproj_kernel(*refs, c_qk, c_v, c_ql, c_kvl, tab_blocks, prompt):
    (x_ref, tab_ref, g1_ref, win_ref, qg_ref, wuq_ref, kvg_ref, wukv_ref), refs = refs[:8], refs[8:]
    if prompt:
        (mk32_ref, mv32_ref, mckv_ref), refs = refs[:3], refs[3:]
    (qd_ref, kd32_ref, kdb_ref, vd32_ref, vdb_ref, qm_ref, ckv_ref, kr_ref, km_ref, vm_ref), refs = refs[:10], refs[10:]
    x = x_ref[...]
    tm = x.shape[0]
    if prompt:
        kbuf, vbuf, cbuf, sem, msem = refs
        i = pl.program_id(0)
        n = pl.num_programs(0)
        slot = i % 2

        def tile_copies(step, s):
            sb = step // tab_blocks
            rows = pl.ds(N_META + (step % tab_blocks) * tm, tm)
            cps = [pltpu.make_async_copy(cbuf.at[s], ckv_ref.at[sb, rows], sem.at[s, 2])]
            for h in range(DA_HEADS):
                cps.append(pltpu.make_async_copy(kbuf.at[s, h], kd32_ref.at[sb, rows, h, :], sem.at[s, 0]))
                cps.append(pltpu.make_async_copy(vbuf.at[s, h], vd32_ref.at[sb, rows, h, :], sem.at[s, 1]))
            return cps

        @pl.when(i >= 2)
        def _slot_free():
            for cp in tile_copies(i - 2, slot):
                cp.wait()

    xn = _rms(x, g1_ref[...]).astype(BF16)
    tab = tab_ref[...]
    o1 = c_qk
    o2 = o1 + c_qk
    o3 = o2 + c_v
    o4 = o3 + c_ql
    o5 = o4 + c_kvl
    ones = jnp.ones((BF16_ROWS, tm), BF16)

    def put_heads32(ref, buf, z):
        for h in range(DA_HEADS):
            if prompt:
                buf[slot, h] = z[:, h * HEAD_W:(h + 1) * HEAD_W]
            else:
                ref[:, h, :] = z[:, h * HEAD_W:(h + 1) * HEAD_W]

    def put_values_t(ref, z):
        for h in range(DA_HEADS):
            ref[h * VT_W:h * VT_W + DA_V, :] = z[:, h * DA_V:(h + 1) * DA_V].T.astype(BF16)
            ref[h * VT_W + DA_V:(h + 1) * VT_W, :] = ones

    zq = _dot(xn, win_ref[:, 0:o1]) * (DA_D ** -0.5 * LOG2E)
    qd_ref[...] = zq.T.astype(BF16) if prompt else zq.astype(BF16)
    zk = _dot(xn, win_ref[:, o1:o2])
    put_heads32(kd32_ref, kbuf if prompt else None, zk)
    kdb_ref[...] = zk.astype(BF16)
    zv = _dot(xn, win_ref[:, o2:o3])
    put_heads32(vd32_ref, vbuf if prompt else None, zv)
    if prompt:
        put_values_t(vdb_ref, zv)
    else:
        vdb_ref[...] = zv.astype(BF16)

    cq = _rms(_dot(xn, win_ref[:, o3:o4]), qg_ref[...]).astype(BF16)
    nq = MLA_HEADS * MLA_NOPE
    qn = _dot(cq, wuq_ref[:, 0:nq]) * (MLA_SCALE * LOG2E)
    qr = _dot(cq, wuq_ref[:, nq:2 * nq]) * (MLA_SCALE * LOG2E)
    for h in range(MLA_HEADS):
        u = qr[:, h * LANES:(h + 1) * LANES] * tab
        rot = u + pltpu.roll(u, MLA_ROPE, 1)
        nope = qn[:, h * LANES:(h + 1) * LANES]
        if prompt:
            qm_ref[h * MLA_W:h * MLA_W + LANES, :] = nope.T.astype(BF16)
            qm_ref[h * MLA_W + LANES:(h + 1) * MLA_W, :] = rot.T.astype(BF16)
        else:
            qm_ref[:, h * MLA_W:h * MLA_W + LANES] = nope.astype(BF16)
            qm_ref[:, h * MLA_W + LANES:(h + 1) * MLA_W] = rot.astype(BF16)

    ckv = _rms(_dot(xn, win_ref[:, o4:o5]), kvg_ref[...])
    if prompt:
        cbuf[slot] = ckv
    else:
        ckv_ref[...] = ckv
    ckvb = ckv.astype(BF16)
    nk = MLA_HEADS * MLA_NOPE
    kn = _dot(ckvb, wukv_ref[:, 0:nk])
    vm = _dot(ckvb, wukv_ref[:, nk:nk + MLA_HEADS * MLA_V])
    if prompt:
        put_values_t(vm_ref, vm)
    else:
        vm_ref[...] = vm.astype(BF16)

    u = _dot(xn, win_ref[:, o5:o5 + LANES]) * tab
    rot = u + pltpu.roll(u, MLA_ROPE, 1)
    kr_ref[...] = rot[:, 0:MLA_ROPE]
    lane = lax.broadcasted_iota(jnp.int32, rot.shape, 1)
    krp = jnp.where(lane < MLA_ROPE, rot, 0.0).astype(BF16)
    for h in range(MLA_HEADS):
        km_ref[:, h * MLA_W:h * MLA_W + LANES] = kn[:, h * LANES:(h + 1) * LANES].astype(BF16)
        km_ref[:, h * MLA_W + LANES:(h + 1) * MLA_W] = krp

    if prompt:
        for cp in tile_copies(i, slot):
            cp.start()

        @pl.when(i % tab_blocks == 0)
        def _meta_rows():
            head = pl.ds(0, N_META)
            sb = i // tab_blocks
            cps = [pltpu.make_async_copy(mk32_ref, kd32_ref.at[sb, head], msem.at[0]),
                   pltpu.make_async_copy(mv32_ref, vd32_ref.at[sb, head], msem.at[1]),
                   pltpu.make_async_copy(mckv_ref, ckv_ref.at[sb, head], msem.at[2])]
            for cp in cps:
                cp.start()
            for cp in cps:
                cp.wait()

        @pl.when(jnp.logical_and(i == n - 1, i >= 1))
        def _drain_previous():
            for cp in tile_copies(i - 1, 1 - slot):
                cp.wait()

        @pl.when(i == n - 1)
        def _drain_last():
            for cp in tile_copies(i, slot):
                cp.wait()


def _inproj(x, tab, g1, win, qg, wuq, kvg, wukv, *, tm, tab_blocks, dims, batch=None, meta=None):
    m, d = x.shape
    c_qk, c_v, c_ql, c_kvl = dims
    assert m % tm == 0
    prompt = batch is not None
    row = lambda w: pl.BlockSpec((tm, w), lambda i: (i, 0))
    hw = DA_HEADS * VT_W
    extra_in, extra_specs, scratch = [], [], []
    if prompt:
        seq = tab_blocks * tm
        assert m == batch * seq
        col = lambda w: pl.BlockSpec((None, w, tm), lambda i: (i // tab_blocks, 0, i % tab_blocks))
        hbm = pl.BlockSpec(memory_space=pl.ANY)
        tall = (batch, N_META + seq)
        extra_in = list(meta)
        extra_specs = [_const_spec(a.shape) for a in meta]
        scratch = [pltpu.VMEM((2, DA_HEADS, tm, HEAD_W), F32), pltpu.VMEM((2, DA_HEADS, tm, HEAD_W), F32),
                   pltpu.VMEM((2, tm, c_kvl), F32), pltpu.SemaphoreType.DMA((2, 3)), pltpu.SemaphoreType.DMA((3,))]
        out_specs = [col(c_qk), hbm, row(c_qk), hbm, col(hw), col(MLA_HEADS * MLA_W), hbm,
                     row(MLA_ROPE), row(MLA_HEADS * MLA_W), col(hw)]
        out_shape = [
            jax.ShapeDtypeStruct((batch, c_qk, seq), BF16),
            jax.ShapeDtypeStruct(tall + (DA_HEADS, HEAD_W), F32),
            jax.ShapeDtypeStruct((m, c_qk), BF16),
            jax.ShapeDtypeStruct(tall + (DA_HEADS, HEAD_W), F32),
            jax.ShapeDtypeStruct((batch, hw, seq), BF16),
            jax.ShapeDtypeStruct((batch, MLA_HEADS * MLA_W, seq), BF16),
            jax.ShapeDtypeStruct(tall + (c_kvl,), F32),
            jax.ShapeDtypeStruct((m, MLA_ROPE), F32),
            jax.ShapeDtypeStruct((m, MLA_HEADS * MLA_W), BF16),
            jax.ShapeDtypeStruct((batch, hw, seq), BF16),
        ]
    else:
        heads32 = pl.BlockSpec((tm, DA_HEADS, HEAD_W), lambda i: (i, 0, 0))
        out_specs = [row(c_qk), heads32, row(c_qk), heads32, row(c_v), row(MLA_HEADS * MLA_W), row(c_kvl),
                     row(MLA_ROPE), row(MLA_HEADS * MLA_W), row(MLA_HEADS * MLA_V)]
        out_shape = [
            jax.ShapeDtypeStruct((m, c_qk), BF16),
            jax.ShapeDtypeStruct((m, DA_HEADS, HEAD_W), F32),
            jax.ShapeDtypeStruct((m, c_qk), BF16),
            jax.ShapeDtypeStruct((m, DA_HEADS, HEAD_W), F32),
            jax.ShapeDtypeStruct((m, c_v), BF16),
            jax.ShapeDtypeStruct((m, MLA_HEADS * MLA_W), BF16),
            jax.ShapeDtypeStruct((m, c_kvl), F32),
            jax.ShapeDtypeStruct((m, MLA_ROPE), F32),
            jax.ShapeDtypeStruct((m, MLA_HEADS * MLA_W), BF16),
            jax.ShapeDtypeStruct((m, MLA_HEADS * MLA_V), BF16),
        ]
    return pl.pallas_call(
        functools.partial(_inproj_kernel, c_qk=c_qk, c_v=c_v, c_ql=c_ql, c_kvl=c_kvl, tab_blocks=tab_blocks,
                          prompt=prompt),
        grid=(m // tm,),
        in_specs=[
            row(d),
            pl.BlockSpec((tm, LANES), lambda i: (i % tab_blocks, 0)),
            _const_spec(g1.shape), _const_spec(win.shape), _const_spec(qg.shape),
            _const_spec(wuq.shape), _const_spec(kvg.shape), _const_spec(wukv.shape),
        ] + extra_specs,
        out_specs=out_specs,
        out_shape=out_shape,
        scratch_shapes=scratch,
        compiler_params=_cparams(1),
        name="inproj",
    )(x, tab, g1, win, qg, wuq, kvg, wukv, *extra_in)


def _softmax_seed(s, v, m_ref, l_ref, acc_ref, j):
    m = jnp.max(s, axis=1, keepdims=True)
    p = jnp.exp2(s - m)
    m_ref[j] = m
    l_ref[j] = jnp.sum(p, axis=1, keepdims=True)
    acc_ref[j] = _dot(p.astype(BF16), v)


def _softmax_step(s, v, m_ref, l_ref, acc_ref, j):
    m_old = m_ref[j]
    m_new = jnp.maximum(m_old, jnp.max(s, axis=1, keepdims=True))
    alpha = jnp.exp2(m_old - m_new)
    p = jnp.exp2(s - m_new)
    l_ref[j] = alpha * l_ref[j] + jnp.sum(p, axis=1, keepdims=True)
    acc_ref[j] = alpha * acc_ref[j] + _dot(p.astype(BF16), v)
    m_ref[j] = m_new


def _diff_lambda(lamv, lam_init):
    a = jnp.sum(lamv[0:1] * lamv[1:2], axis=1, keepdims=True)
    b = jnp.sum(lamv[2:3] * lamv[3:4], axis=1, keepdims=True)
    return jnp.exp(a) - jnp.exp(b) + lam_init


def _split_maps(q):
    lane = lax.broadcasted_iota(jnp.int32, q.shape, 1)
    zero = jnp.zeros_like(q)
    return jnp.where(lane < DA_D, q, zero), jnp.where(lane >= DA_D, q, zero)


def _init_state(m_ref, acc_ref):
    m_ref[...] = jnp.full(m_ref.shape, NEG_BIG, F32)
    acc_ref[...] = jnp.zeros(acc_ref.shape, F32)


def _step_t(st, shift, vt, m_ref, acc_ref, j):
    m_old = m_ref[j]
    m_new = jnp.maximum(m_old, jnp.max(st, axis=0, keepdims=True) + shift)
    p = jnp.exp2(st - (m_new - shift))
    acc_ref[j] = jnp.exp2(m_old - m_new) * acc_ref[j] + _dot(vt, p.astype(BF16))
    m_ref[j] = m_new


def _diff_attn_parts(qi, ki, qt_ref, k_ref, vt_ref, mk_ref, mvt_ref, pos_ref, cq_ref, corr_ref,
                     lamv_ref, g_ref, o_ref, m_s, acc_s, *, tq, lam_init):
    qrow = lax.broadcasted_iota(jnp.int32, (HEAD_W, tq), 0) < DA_D
    klane = lax.broadcasted_iota(jnp.int32, (tq, HEAD_W), 1) < DA_D

    def init():
        _init_state(m_s, acc_s)

    def body(diag):
        qpos = (lax.broadcasted_iota(jnp.int32, (1, tq), 1) + (qi - ki) * tq).astype(F32)
        pos = pos_ref[...]
        mlane = lax.broadcasted_iota(jnp.int32, (N_META, HEAD_W), 1) < DA_D

        def scores(h):
            hs = slice(h * HEAD_W, (h + 1) * HEAD_W)
            qt = qt_ref[hs, :]
            cq = cq_ref[h]
            kk = k_ref[:, hs]
            corr = corr_ref[...] * (_alibi_slope(h) * LOG2E) if diag else None
            mk = mk_ref[:, hs]
            mzero = jnp.zeros_like(mk)
            out = []
            for c in range(2):
                qc = jnp.where(qrow, qt, cq) if c == 0 else jnp.where(qrow, cq, qt)
                kc = jnp.where(klane, kk, pos) if c == 0 else jnp.where(klane, pos, kk)
                if diag:
                    mc = jnp.where(mlane, mk, mzero) if c == 0 else jnp.where(mlane, mzero, mk)
                    kc = jnp.concatenate([kc, mc], axis=0)
                st = _dot(kc, qc)
                out.append(st + corr if diag else st)
            return out

        queue = [scores(h) for h in range(SCORE_LOOKAHEAD)]
        for h in range(DA_HEADS):
            cur = queue.pop(0)
            if h + SCORE_LOOKAHEAD < DA_HEADS:
                queue.append(scores(h + SCORE_LOOKAHEAD))
            shift = qpos * (-(_alibi_slope(h) * LOG2E))
            vt = vt_ref[h * VT_W:(h + 1) * VT_W, :]
            if diag:
                vt = jnp.concatenate([vt, mvt_ref[h * VT_W:(h + 1) * VT_W, :]], axis=1)
            for c in range(2):
                _step_t(cur[c], shift, vt, m_s, acc_s, 2 * h + c)

    def finish():
        lam = _diff_lambda(lamv_ref[...], lam_init)
        g = g_ref[...]
        for h in range(DA_HEADS):
            a0 = acc_s[2 * h]
            a1 = acc_s[2 * h + 1]
            ot = a0[0:DA_V] / a0[DA_V:DA_V + 1] - lam * (a1[0:DA_V] / a1[DA_V:DA_V + 1])
            ot = ot * lax.rsqrt(jnp.mean(ot * ot, axis=0, keepdims=True) + EPS) * g * (1.0 - lam_init)
            o_ref[:, h * HEAD_W:(h + 1) * HEAD_W] = ot.T.astype(BF16)

    return init, body, finish


def _mla_attn_parts(qt_ref, k_ref, vt_ref, mk_ref, mvt_ref, mask_ref, o_ref, m_s, acc_s):
    def init():
        _init_state(m_s, acc_s)

    def body(diag):
        def scores(h):
            kk = k_ref[:, h * MLA_W:(h + 1) * MLA_W]
            if diag:
                kk = jnp.concatenate([kk, mk_ref[:, h * MLA_W:(h + 1) * MLA_W]], axis=0)
            st = _dot(kk, qt_ref[h * MLA_W:(h + 1) * MLA_W, :])
            return st + mask_ref[...] if diag else st

        queue = [scores(h) for h in range(SCORE_LOOKAHEAD)]
        for h in range(MLA_HEADS):
            st = queue.pop(0)
            if h + SCORE_LOOKAHEAD < MLA_HEADS:
                queue.append(scores(h + SCORE_LOOKAHEAD))
            vt = vt_ref[h * VT_W:(h + 1) * VT_W, :]
            if diag:
                vt = jnp.concatenate([vt, mvt_ref[h * VT_W:(h + 1) * VT_W, :]], axis=1)
            _step_t(st, 0.0, vt, m_s, acc_s, h)

    def finish():
        for h in range(MLA_HEADS):
            a = acc_s[h]
            o_ref[:, h * MLA_V:(h + 1) * MLA_V] = (a[0:MLA_V] / a[MLA_V:MLA_V + 1]).T.astype(BF16)

    return init, body, finish


N_DIFF_IN = 10
N_MLA_IN = 6


def _prompt_attn_kernel(qi_ref, ki_ref, *refs, tq, lam_init):
    t = pl.program_id(1)
    qi = qi_ref[t]
    ki = ki_ref[t]
    d_in, refs = refs[:N_DIFF_IN], refs[N_DIFF_IN:]
    m_in, refs = refs[:N_MLA_IN], refs[N_MLA_IN:]
    d_out, m_out, d_m, d_acc, m_m, m_acc = refs
    d_init, d_body, d_finish = _diff_attn_parts(qi, ki, *d_in, d_out, d_m, d_acc, tq=tq, lam_init=lam_init)
    m_init, m_body, m_finish = _mla_attn_parts(*m_in, m_out, m_m, m_acc)

    @pl.when(ki == 0)
    def _init():
        d_init()
        m_init()

    @pl.when(ki != qi)
    def _off_diagonal():
        d_body(False)
        m_body(False)

    @pl.when(ki == qi)
    def _diagonal():
        d_body(True)
        m_body(True)
        d_finish()
        m_finish()


def _pair_tables(nq):
    qi = np.concatenate([np.full((i + 1,), i, np.int32) for i in range(nq)])
    ki = np.concatenate([np.arange(i + 1, dtype=np.int32) for i in range(nq)])
    return jnp.asarray(qi), jnp.asarray(ki)


def _tile_geometry(tq):
    j = np.arange(tq)[:, None]
    i = np.arange(tq)[None, :]
    visible = (j // CHUNK) <= (i // CHUNK)
    return i, j, visible


def _bf16_split3(x):
    parts = []
    for _ in range(3):
        p = float(np.asarray(x, np.float32).astype(BF16).astype(np.float32))
        parts.append(p)
        x = x - p
    return parts


def _alibi_operands(tq):
    assert tq <= 2 * MXU_DIM
    j = np.arange(tq)
    jlo = (j % MXU_DIM).astype(np.float32)
    jhi = (j - j % MXU_DIM).astype(np.float32)
    pos = np.zeros((tq, HEAD_W), np.float32)
    cq = np.zeros((DA_HEADS, HEAD_W, tq), np.float32)
    for base in (0, DA_D):
        for r in range(3):
            pos[:, base + 2 * r] = jlo
            pos[:, base + 2 * r + 1] = jhi
    for h in range(DA_HEADS):
        parts = _bf16_split3(_alibi_slope(h) * LOG2E)
        for base in (0, DA_D):
            for r in range(3):
                cq[h, base + 2 * r, :] = parts[r]
                cq[h, base + 2 * r + 1, :] = parts[r]
    return jnp.asarray(pos, BF16), jnp.asarray(cq, BF16)


def _prompt_attn_specs(tq, wq, wk, wv):
    qt_spec = pl.BlockSpec((None, wq, tq), lambda b, t, qi, ki: (b, 0, qi[t]))
    k_spec = pl.BlockSpec((None, tq, wk), lambda b, t, qi, ki: (b, ki[t], 0))
    vt_spec = pl.BlockSpec((None, wv, tq), lambda b, t, qi, ki: (b, 0, ki[t]))
    mk_spec = pl.BlockSpec((N_META, wk), lambda b, t, qi, ki: (0, 0))
    mvt_spec = pl.BlockSpec((wv, N_META), lambda b, t, qi, ki: (0, 0))
    return qt_spec, k_spec, vt_spec, mk_spec, mvt_spec


def _prompt_attn(qt, k, vt, mk, mvt, lamv, g, m_qt, m_k, m_vt, m_mk, m_mvt, *, tq, lam_init):
    b, s, w = k.shape
    nq = s // tq
    qi, ki = _pair_tables(nq)
    i, j, visible = _tile_geometry(tq)
    corr = np.where(visible, np.where(j > i, -2.0 * (j - i), 0.0), NEG_BIG)
    corr = jnp.asarray(np.concatenate([corr, np.broadcast_to(i, (N_META, tq))]).astype(np.float32))
    pos, cq = _alibi_operands(tq)
    mask = np.concatenate([np.where(visible, 0.0, NEG_BIG), np.zeros((N_META, tq))])
    mask = jnp.asarray(mask.astype(np.float32))
    full = lambda a: pl.BlockSpec(a.shape, lambda b_, t, qi_, ki_: (0,) * a.ndim)
    d_args = [qt, k, vt, mk, mvt, pos, cq, corr, lamv, g]
    d_specs = list(_prompt_attn_specs(tq, w, w, vt.shape[1])) + [full(a) for a in d_args[5:]]
    m_args = [m_qt, m_k, m_vt, m_mk, m_mvt, mask]
    wq = m_k.shape[2]
    m_specs = list(_prompt_attn_specs(tq, wq, wq, m_vt.shape[1])) + [full(mask)]
    assert len(d_args) == N_DIFF_IN and len(m_args) == N_MLA_IN
    wo = MLA_HEADS * MLA_V
    out_spec = lambda width: pl.BlockSpec((None, tq, width), lambda b_, t, qi_, ki_: (b_, qi_[t], 0))
    return pl.pallas_call(
        functools.partial(_prompt_attn_kernel, tq=tq, lam_init=lam_init),
        grid_spec=pltpu.PrefetchScalarGridSpec(
            num_scalar_prefetch=2,
            grid=(b, int(qi.shape[0])),
            in_specs=d_specs + m_specs,
            out_specs=[out_spec(w), out_spec(wo)],
            scratch_shapes=[pltpu.VMEM((2 * DA_HEADS, 1, tq), F32), pltpu.VMEM((2 * DA_HEADS, VT_W, tq), F32),
                            pltpu.VMEM((MLA_HEADS, 1, tq), F32), pltpu.VMEM((MLA_HEADS, VT_W, tq), F32)],
        ),
        out_shape=[jax.ShapeDtypeStruct((b, s, w), BF16), jax.ShapeDtypeStruct((b, s, wo), BF16)],
        compiler_params=_cparams(2),
        name="prompt_attn",
    )(qi, ki, *d_args, *m_args)


def _sample_diff_kernel(q_ref, kc_hbm, vc_hbm, kn_ref, vn_ref, dc_ref, dn_ref, lamv_ref, g_ref,
                        o_ref, kbuf, vbuf, sem, m_s, l_s, acc_s, *, tk, lam_init):
    kt = pl.program_id(1)
    nkt = pl.num_programs(1)
    step = pl.program_id(0) * nkt + kt
    n_steps = pl.num_programs(0) * nkt

    def tile_copies(s, slot):
        sb = s // nkt
        rows = pl.ds((s % nkt) * tk, tk)
        cps = []
        for h in range(DA_HEADS):
            cps.append(pltpu.make_async_copy(kc_hbm.at[sb, rows, h, :], kbuf.at[slot, h], sem.at[slot, 0]))
            cps.append(pltpu.make_async_copy(vc_hbm.at[sb, rows, h, :], vbuf.at[slot, h], sem.at[slot, 1]))
        return cps

    @pl.when(step == 0)
    def _prime():
        for cp in tile_copies(0, 0):
            cp.start()

    @pl.when(step + 1 < n_steps)
    def _prefetch():
        for cp in tile_copies(step + 1, (step + 1) % 2):
            cp.start()

    slot = step % 2
    for cp in tile_copies(step, slot):
        cp.wait()

    def heads(get_k, get_v, dist, first):
        for h in range(DA_HEADS):
            q1, q2 = _split_maps(q_ref[:, h * HEAD_W:(h + 1) * HEAD_W])
            qq = jnp.concatenate([q1, q2], axis=0)
            s = _dot_nt(qq, get_k(h)) + dist * (-_alibi_slope(h) * LOG2E)
            if first:
                _softmax_seed(s, get_v(h), m_s, l_s, acc_s, h)
            else:
                _softmax_step(s, get_v(h), m_s, l_s, acc_s, h)

    cache_k = lambda h: kbuf[slot, h].astype(BF16)
    cache_v = lambda h: vbuf[slot, h].astype(BF16)

    @pl.when(kt == 0)
    def _first():
        heads(cache_k, cache_v, dc_ref[...], True)

    @pl.when(kt > 0)
    def _rest():
        heads(cache_k, cache_v, dc_ref[...], False)

    @pl.when(kt == nkt - 1)
    def _finish():
        heads(lambda h: kn_ref[:, h * HEAD_W:(h + 1) * HEAD_W], lambda h: vn_ref[:, h * HEAD_W:(h + 1) * HEAD_W],
              dn_ref[...], False)
        lam = _diff_lambda(lamv_ref[...], lam_init)
        g = g_ref[...]
        nq = q_ref.shape[0]
        for h in range(DA_HEADS):
            a = acc_s[h] / l_s[h]
            o = a[0:nq] - lam * a[nq:2 * nq]
            o_ref[:, h * HEAD_W:(h + 1) * HEAD_W] = (_rms(o, g) * (1.0 - lam_init)).astype(BF16)


def _sample_diff_attn(q, kc, vc, kn, vn, dist_c, dist_n, lamv, g, *, tk, lam_init):
    bs, nq, w = q.shape
    lc = kc.shape[1]
    assert lc % tk == 0
    full = lambda a: pl.BlockSpec(a.shape, lambda b, t: (0,) * a.ndim)
    per_stream = lambda a: pl.BlockSpec((None,) + a.shape[1:], lambda b, t: (b,) + (0,) * (a.ndim - 1))
    cache = pl.BlockSpec(memory_space=pl.ANY)
    return pl.pallas_call(
        functools.partial(_sample_diff_kernel, tk=tk, lam_init=lam_init),
        grid=(bs, lc // tk),
        in_specs=[per_stream(q), cache, cache, per_stream(kn), per_stream(vn),
                  pl.BlockSpec((None, 2 * nq, tk), lambda b, t: (t, 0, 0)),
                  full(dist_n), full(lamv), full(g)],
        out_specs=per_stream(q),
        out_shape=jax.ShapeDtypeStruct((bs, nq, w), BF16),
        scratch_shapes=[pltpu.VMEM((2, DA_HEADS, tk, HEAD_W), F32), pltpu.VMEM((2, DA_HEADS, tk, HEAD_W), F32),
                        pltpu.SemaphoreType.DMA((2, 2)),
                        pltpu.VMEM((DA_HEADS, 2 * nq, 1), F32), pltpu.VMEM((DA_HEADS, 2 * nq, 1), F32),
                        pltpu.VMEM((DA_HEADS, 2 * nq, DA_V), F32)],
        compiler_params=_cparams(2),
        name="sample_diff_attn",
    )(q, kc, vc, kn, vn, dist_c, dist_n, lamv, g)


def _sample_mla_kernel(q_ref, cc_ref, krc_ref, cn_ref, krn_ref, wuk_ref, wuv_ref, o_ref,
                       ql_s, qr_s, m_s, l_s, acc_s):
    kt = pl.program_id(1)
    nkt = pl.num_programs(1)
    nq = q_ref.shape[0]

    @pl.when(kt == 0)
    def _prep():
        for h in range(MLA_HEADS):
            qn = q_ref[:, h * MLA_W:h * MLA_W + MLA_NOPE]
            ql_s[h * nq:(h + 1) * nq, :] = _dot_nt(qn, wuk_ref[:, h * MLA_NOPE:(h + 1) * MLA_NOPE]).astype(BF16)
            qr_s[h * nq:(h + 1) * nq, :] = q_ref[:, h * MLA_W + MLA_NOPE:(h + 1) * MLA_W]

    def scores(c_ref, kr_ref):
        cb = c_ref[...].astype(BF16)
        krb = kr_ref[...].astype(BF16)
        s = _dot_nt(ql_s[...], cb) + _dot_nt(qr_s[:, 0:MLA_ROPE], krb)
        return s, cb

    @pl.when(kt == 0)
    def _first():
        s, cb = scores(cc_ref, krc_ref)
        _softmax_seed(s, cb, m_s, l_s, acc_s, 0)

    @pl.when(kt > 0)
    def _rest():
        s, cb = scores(cc_ref, krc_ref)
        _softmax_step(s, cb, m_s, l_s, acc_s, 0)

    @pl.when(kt == nkt - 1)
    def _finish():
        s, cb = scores(cn_ref, krn_ref)
        _softmax_step(s, cb, m_s, l_s, acc_s, 0)
        ol = (acc_s[0] / l_s[0]).astype(BF16)
        for h in range(MLA_HEADS):
            o_ref[:, h * MLA_V:(h + 1) * MLA_V] = _dot(
                ol[h * nq:(h + 1) * nq, :], wuv_ref[:, h * MLA_V:(h + 1) * MLA_V]).astype(BF16)


def _sample_mla_attn(q, cc, krc, cn, krn, wuk, wuv, *, tk):
    bs, nq, wq = q.shape
    lc, kvl = cc.shape[1], cc.shape[2]
    assert lc % tk == 0
    full = lambda a: pl.BlockSpec(a.shape, lambda b, t: (0,) * a.ndim)
    per_stream = lambda a: pl.BlockSpec((None,) + a.shape[1:], lambda b, t: (b,) + (0,) * (a.ndim - 1))
    rows = MLA_HEADS * nq
    return pl.pallas_call(
        _sample_mla_kernel,
        grid=(bs, lc // tk),
        in_specs=[per_stream(q),
                  pl.BlockSpec((None, tk, kvl), lambda b, t: (b, t, 0)),
                  pl.BlockSpec((None, tk, MLA_ROPE), lambda b, t: (b, t, 0)),
                  per_stream(cn), per_stream(krn), full(wuk), full(wuv)],
        out_specs=pl.BlockSpec((None, nq, MLA_HEADS * MLA_V), lambda b, t: (b, 0, 0)),
        out_shape=jax.ShapeDtypeStruct((bs, nq, MLA_HEADS * MLA_V), BF16),
        scratch_shapes=[pltpu.VMEM((rows, kvl), BF16), pltpu.VMEM((rows, LANES), BF16),
                        pltpu.VMEM((1, rows, 1), F32), pltpu.VMEM((1, rows, 1), F32),
                        pltpu.VMEM((1, rows, kvl), F32)],
        compiler_params=_cparams(2),
        name="sample_mla_attn",
    )(q, cc, krc, cn, krn, wuk, wuv)


ROUTER_ROWS = SUBLANES * (1 + N_GROUPS)


def _route(lt):
    g = [lt[i:i + 1] for i in range(N_GROUPS)]
    gmax = functools.reduce(jnp.maximum, g)
    gidx = jnp.full_like(gmax, float(N_GROUPS - 1))
    for i in range(N_GROUPS - 2, -1, -1):
        gidx = jnp.where(g[i] == gmax, float(i), gidx)
    den = functools.reduce(lambda a, b: a + b, [jnp.exp(gi - gmax) for gi in g])
    p_top = 1.0 / den
    e = []
    for j in range(EXPERTS_PER_GROUP):
        ej = lt[SUBLANES * N_GROUPS + j:SUBLANES * N_GROUPS + j + 1]
        for grp in range(N_GROUPS - 2, -1, -1):
            ej = jnp.where(gidx == float(grp), lt[SUBLANES * (grp + 1) + j:SUBLANES * (grp + 1) + j + 1], ej)
        e.append(ej)

    def first_argmax(vals):
        vmax = functools.reduce(jnp.maximum, vals)
        idx = jnp.full_like(vmax, float(len(vals) - 1))
        for i in range(len(vals) - 2, -1, -1):
            idx = jnp.where(vals[i] == vmax, float(i), idx)
        return vmax, idx

    v1, i1 = first_argmax(e)
    rest = [jnp.where(i1 == float(j), -jnp.inf, e[j]) for j in range(EXPERTS_PER_GROUP)]
    v2, i2 = first_argmax(rest)
    r = jnp.exp(v2 - v1)
    w1 = p_top / (1.0 + r)
    w2 = p_top * r / (1.0 + r)
    base = gidx * float(EXPERTS_PER_GROUP)
    return w1, w2, base + i1, base + i2


def _merge_kernel(od_ref, om_ref, x_ref, wo_ref, g2_ref, wr_ref, br_ref, hp_ref, xn_ref, rt_ref):
    nd = od_ref.shape[1]
    y = _dot(od_ref[...], wo_ref[0:nd, :]) + _dot(om_ref[...], wo_ref[nd:, :])
    hp = x_ref[...] + y
    hp_ref[...] = hp
    xn = _rms(hp, g2_ref[...])
    xn_ref[...] = _pack_halves(xn)
    lt = _dot_nt(wr_ref[...], xn.astype(BF16)) + br_ref[...]
    rows = _route(lt)
    for i, r in enumerate(rows):
        rt_ref[i:i + 1, :] = r
    rt_ref[4:8, :] = jnp.zeros((4, rt_ref.shape[1]), F32)


def _merge(od, om, x, wo, g2, wr, br, *, tm):
    m, d = x.shape
    row = lambda w: pl.BlockSpec((tm, w), lambda i: (i, 0))
    return pl.pallas_call(
        _merge_kernel,
        grid=(m // tm,),
        in_specs=[row(od.shape[1]), row(om.shape[1]), row(d), _const_spec(wo.shape),
                  _const_spec(g2.shape), _const_spec(wr.shape), _const_spec(br.shape)],
        out_specs=[row(d), row(d // 2), pl.BlockSpec((SUBLANES, tm), lambda i: (0, i))],
        out_shape=[jax.ShapeDtypeStruct((m, d), F32), jax.ShapeDtypeStruct((m, d // 2), jnp.uint32),
                   jax.ShapeDtypeStruct((SUBLANES, m), F32)],
        compiler_params=_cparams(1),
        name="merge",
    )(od, om, x, wo, g2, wr, br)


def _cast_kernel(x_ref, o_ref):
    o_ref[...] = x_ref[...].astype(BF16)


def _cast_bf16(w, *, rows):
    e, r, c = w.shape
    assert r % rows == 0
    spec = pl.BlockSpec((None, rows, c), lambda i, j: (i, j, 0))
    return pl.pallas_call(
        _cast_kernel,
        grid=(e, r // rows),
        in_specs=[spec],
        out_specs=spec,
        out_shape=jax.ShapeDtypeStruct(w.shape, BF16),
        compiler_params=_cparams(2),
        name="cast_bf16",
    )(w)


def _swiglu(xp, wg_ref, wu_ref, wd_ref):
    lo, hi = _unpack_halves(xp)
    lo = lo.astype(BF16)
    hi = hi.astype(BF16)
    w = lo.shape[1]
    g = _dot(lo, wg_ref[0:w, :]) + _dot(hi, wg_ref[w:, :])
    u = _dot(lo, wu_ref[0:w, :]) + _dot(hi, wu_ref[w:, :])
    h = (g * jax.nn.sigmoid(g) * u).astype(BF16)
    return _dot(h, wd_ref[...])


def _moe_sorted_kernel(te_ref, nu_ref, x_ref, w_ref, wg_ref, wu_ref, wd_ref, *rest, out_first, x_first, x_tiles):
    y_ref = rest[-1]
    g = pl.program_id(0) + out_first
    live = jnp.logical_and(g < nu_ref[0], jnp.logical_and(g >= x_first, g < x_first + x_tiles))

    @pl.when(live)
    def _():
        y_ref[...] = _pack_halves(w_ref[...] * _swiglu(x_ref[...], wg_ref, wu_ref, wd_ref))

    @pl.when(jnp.logical_not(live))
    def _():
        y_ref[...] = jnp.zeros_like(y_ref)


def _moe_sorted(tile_expert, n_used, xs, ws, wg, wu, wd, *, tm, x_first, y_prev=None):
    n_x, dh = xs.shape
    d = 2 * dh
    f = wg.shape[2]
    x_tiles = n_x // tm
    n_tiles = ws.shape[0] // tm
    out_first = 0 if y_prev is None else x_first
    grid_tiles = n_tiles if y_prev is None else x_tiles
    local = lambda i: jnp.clip(i + out_first - x_first, 0, x_tiles - 1)
    in_specs = [pl.BlockSpec((tm, dh), lambda i, te, nu: (local(i), 0)),
                pl.BlockSpec((tm, 1), lambda i, te, nu: (i + out_first, 0)),
                pl.BlockSpec((None, d, f), lambda i, te, nu: (te[local(i) + x_first], 0, 0)),
                pl.BlockSpec((None, d, f), lambda i, te, nu: (te[local(i) + x_first], 0, 0)),
                pl.BlockSpec((None, f, d), lambda i, te, nu: (te[local(i) + x_first], 0, 0))]
    args = [tile_expert, n_used, xs, ws, wg, wu, wd]
    aliases = {}
    if y_prev is not None:
        in_specs.append(pl.BlockSpec(memory_space=pl.ANY))
        args.append(y_prev)
        aliases = {len(args) - 1: 0}
    return pl.pallas_call(
        functools.partial(_moe_sorted_kernel, out_first=out_first, x_first=x_first, x_tiles=x_tiles),
        grid_spec=pltpu.PrefetchScalarGridSpec(
            num_scalar_prefetch=2,
            grid=(grid_tiles,),
            in_specs=in_specs,
            out_specs=pl.BlockSpec((tm, dh), lambda i, te, nu: (i + out_first, 0)),
        ),
        out_shape=jax.ShapeDtypeStruct((n_tiles * tm, dh), jnp.uint32),
        input_output_aliases=aliases,
        compiler_params=_cparams(1),
        name="moe_sorted",
    )(*args)


def _moe_dense_kernel(x_ref, hp_ref, gates_ref, wg_ref, wu_ref, wd_ref, gf_ref, o_ref, acc_s):
    e = pl.program_id(0)

    @pl.when(e == 0)
    def _():
        acc_s[...] = jnp.zeros_like(acc_s)

    lane = lax.broadcasted_iota(jnp.int32, gates_ref.shape, 1)
    gate = jnp.sum(jnp.where(lane == e, gates_ref[...], 0.0), axis=1, keepdims=True)
    acc_s[...] += gate * _swiglu(x_ref[...], wg_ref, wu_ref, wd_ref)

    @pl.when(e == pl.num_programs(0) - 1)
    def _():
        o_ref[...] = _rms(hp_ref[...] + acc_s[...], gf_ref[...])


def _moe_dense(xn, hp, gates, wg, wu, wd, gf):
    m, d = hp.shape
    ne, _, f = wg.shape
    full = lambda a: pl.BlockSpec(a.shape, lambda e: (0,) * a.ndim)
    return pl.pallas_call(
        _moe_dense_kernel,
        grid=(ne,),
        in_specs=[full(xn), full(hp), full(gates),
                  pl.BlockSpec((None, d, f), lambda e: (e, 0, 0)),
                  pl.BlockSpec((None, d, f), lambda e: (e, 0, 0)),
                  pl.BlockSpec((None, f, d), lambda e: (e, 0, 0)),
                  full(gf)],
        out_specs=full(hp),
        out_shape=jax.ShapeDtypeStruct((m, d), F32),
        scratch_shapes=[pltpu.VMEM((m, d), F32)],
        compiler_params=_cparams(1),
        name="moe_dense",
    )(xn, hp, gates, wg, wu, wd, gf)


def _combine_kernel(hp_ref, y1_ref, y2_ref, gf_ref, o_ref):
    w = y1_ref.shape[1]
    a_lo, a_hi = _unpack_halves(y1_ref[...])
    b_lo, b_hi = _unpack_halves(y2_ref[...])
    s_lo = hp_ref[:, 0:w] + (a_lo + b_lo)
    s_hi = hp_ref[:, w:] + (a_hi + b_hi)
    ms = (jnp.sum(s_lo * s_lo, axis=-1, keepdims=True) + jnp.sum(s_hi * s_hi, axis=-1, keepdims=True)) / (2 * w)
    inv = lax.rsqrt(ms + EPS)
    o_ref[:, 0:w] = s_lo * inv * gf_ref[:, 0:w]
    o_ref[:, w:] = s_hi * inv * gf_ref[:, w:]


def _combine(hp, y1, y2, gf, *, tm):
    m, d = hp.shape
    row = pl.BlockSpec((tm, d), lambda i: (i, 0))
    half = pl.BlockSpec((tm, d // 2), lambda i: (i, 0))
    return pl.pallas_call(
        _combine_kernel,
        grid=(m // tm,),
        in_specs=[row, half, half, _const_spec(gf.shape)],
        out_specs=row,
        out_shape=jax.ShapeDtypeStruct((m, d), F32),
        compiler_params=_cparams(1),
        name="combine",
    )(hp, y1, y2, gf)


def _rope_table(pos):
    half = MLA_ROPE // 2
    inv_freq = ROPE_THETA ** (-jnp.arange(half, dtype=F32) / half)
    ang = pos.astype(F32)[:, None] * inv_freq[None, :]
    c, s = jnp.cos(ang), jnp.sin(ang)
    return jnp.concatenate([c, c, -s, s], axis=1)


def _swap_halves(w):
    half = MLA_ROPE // 2
    return jnp.concatenate([w[..., half:], w[..., :half]], axis=-1)


def _values_t(v):
    n = v.shape[0]
    vt = v.reshape(n, DA_HEADS, DA_V).transpose(1, 2, 0)
    return jnp.concatenate([vt, jnp.ones((DA_HEADS, BF16_ROWS, n), v.dtype)], axis=1).reshape(DA_HEADS * VT_W, n)


def _sort_by_expert(eid, w, tm):
    t = eid.shape[1]
    flat_e = eid.reshape(-1)
    onehot = (flat_e[:, None] == jnp.arange(N_EXPERTS, dtype=jnp.int32)[None, :]).astype(jnp.int32)
    rank = jnp.sum((jnp.cumsum(onehot, axis=0) - onehot) * onehot, axis=1)
    counts = jnp.sum(onehot, axis=0)
    tiles_per = (counts + tm - 1) // tm
    tiles_end = jnp.cumsum(tiles_per)
    row_start = (tiles_end - tiles_per) * tm
    pos = row_start[flat_e] + rank
    n_tiles = (2 * t) // tm + N_EXPERTS
    slot_a = jnp.full((n_tiles * tm,), -1, jnp.int32).at[pos].set(jnp.arange(2 * t, dtype=jnp.int32),
                                                                   unique_indices=True, mode="promise_in_bounds")
    used = slot_a >= 0
    safe_a = jnp.maximum(slot_a, 0)
    sorted_tok = jnp.where(used, safe_a % t, 0)
    sorted_w = jnp.where(used, w.reshape(-1).at[safe_a].get(mode="promise_in_bounds"), 0.0)
    tile_ids = jnp.arange(n_tiles, dtype=jnp.int32)
    tile_expert = jnp.minimum(jnp.sum((tiles_end[None, :] <= tile_ids[:, None]).astype(jnp.int32), axis=1),
                              N_EXPERTS - 1)
    n_used = tiles_end[-1:].astype(jnp.int32)
    return pos.reshape(2, t), sorted_tok, sorted_w, tile_expert, n_used


def kernel(x_prompt, x_sample, cache_diff_k, cache_diff_v, cache_mla_ckv, cache_mla_kr, meta_tokens, norm1_g, w_in, diff_lam_q1, diff_lam_k1, diff_lam_q2, diff_lam_k2, diff_subln_g, mla_q_norm_g, mla_w_uq, mla_kv_norm_g, mla_w_uk, mla_w_uv, w_o, norm2_g, router_group_w, router_group_b, router_expert_w, router_expert_b, expert_w_gate, expert_w_up, expert_w_down, final_norm_g):
    depth = norm1_g.shape[0]
    assert depth == 1, "single-layer step only"
    assert MLA_HEADS == DA_HEADS and MLA_V == DA_V
    lam_init = 0.8 - 0.6 * math.exp(-0.3 * 0)
    b, s, d = x_prompt.shape
    bs, ss, _ = x_sample.shape
    past = cache_mla_kr.shape[2]
    lc = N_META + past
    c_qk = DA_HEADS * 2 * DA_D
    c_v = DA_HEADS * DA_V
    c_ql = mla_q_norm_g.shape[1]
    c_kvl = mla_kv_norm_g.shape[1]
    o5 = 2 * c_qk + c_v + c_ql + c_kvl

    win = w_in[0]
    win_ext = jnp.concatenate([win, _swap_halves(win[:, o5:])], axis=1).astype(BF16)
    wuq = mla_w_uq[0].reshape(c_ql, MLA_HEADS, MLA_NOPE + MLA_ROPE)
    wuq_n = wuq[:, :, :MLA_NOPE].reshape(c_ql, MLA_HEADS * MLA_NOPE)
    wuq_r = jnp.concatenate([wuq[:, :, MLA_NOPE:], _swap_halves(wuq[:, :, MLA_NOPE:])], axis=2)
    wuq_ext = jnp.concatenate([wuq_n, wuq_r.reshape(c_ql, MLA_HEADS * LANES)], axis=1).astype(BF16)
    wuk = mla_w_uk[0].astype(BF16)
    wuv = mla_w_uv[0].astype(BF16)
    wukv = jnp.concatenate([wuk, wuv], axis=1)
    wo = w_o[0].astype(BF16)
    wr = jnp.zeros((ROUTER_ROWS, d), F32).at[0:N_GROUPS].set(router_group_w[0].T)
    br = jnp.zeros((ROUTER_ROWS, 1), F32).at[0:N_GROUPS, 0].set(router_group_b[0])
    rew = router_expert_w[0].T.reshape(N_GROUPS, EXPERTS_PER_GROUP, d)
    reb = router_expert_b[0].reshape(N_GROUPS, EXPERTS_PER_GROUP)
    for grp in range(N_GROUPS):
        wr = wr.at[SUBLANES * (grp + 1):SUBLANES * (grp + 1) + EXPERTS_PER_GROUP].set(rew[grp])
        br = br.at[SUBLANES * (grp + 1):SUBLANES * (grp + 1) + EXPERTS_PER_GROUP, 0].set(reb[grp])
    wr = wr.astype(BF16)
    wg = _cast_bf16(expert_w_gate[0], rows=min(CAST_ROWS, expert_w_gate.shape[2]))
    wu = _cast_bf16(expert_w_up[0], rows=min(CAST_ROWS, expert_w_up.shape[2]))
    wd = _cast_bf16(expert_w_down[0], rows=min(CAST_ROWS, expert_w_down.shape[2]))
    gf = final_norm_g[None, :]
    lamv = jnp.stack([diff_lam_q1[0], diff_lam_k1[0], diff_lam_q2[0], diff_lam_k2[0]])
    subg = diff_subln_g

    dims = (c_qk, c_v, c_ql, c_kvl)
    inproj = functools.partial(_inproj, g1=norm1_g, win=win_ext, qg=mla_q_norm_g, wuq=wuq_ext,
                               kvg=mla_kv_norm_g, wukv=wukv, dims=dims)

    (_, mdk32, mdk, mdv32, mdv, _, mckv, _, mkm, mvm) = inproj(
        meta_tokens, jnp.zeros((N_META, LANES), F32), tm=N_META, tab_blocks=1)

    ts = bs * ss
    s_pos = past + jnp.arange(ss, dtype=jnp.int32)
    (sqd, s_dk, skd, s_dv, svd, sqm, sckv, skr, _, _) = inproj(
        x_sample.reshape(ts, d), _rope_table(s_pos), tm=ss, tab_blocks=1)
    q3 = lambda a: a.reshape(bs, ss, a.shape[-1])
    kpos_c = np.arange(lc) - N_META
    dist_c = np.where(kpos_c[None, :] >= 0, np.abs(past + np.arange(ss)[:, None] - kpos_c[None, :]), 0)
    dist_n = np.abs(np.arange(ss)[:, None] - np.arange(ss)[None, :])
    tk_s = lc // 2 if (lc // 2) % SUBLANES == 0 and lc % 2 == 0 else lc
    dist_c = np.tile(dist_c, (2, 1)).astype(np.float32).reshape(2 * ss, lc // tk_s, tk_s)
    dist_c = jnp.asarray(np.moveaxis(dist_c, 1, 0))
    dist_n = jnp.asarray(np.tile(dist_n, (2, 1)).astype(np.float32))
    sod = _sample_diff_attn(q3(sqd), cache_diff_k[0], cache_diff_v[0], q3(skd), q3(svd), dist_c, dist_n, lamv, subg,
                            tk=tk_s, lam_init=lam_init)
    krc = jnp.concatenate([jnp.zeros((bs, N_META, MLA_ROPE), F32), cache_mla_kr[0]], axis=1)
    som = _sample_mla_attn(q3(sqm), cache_mla_ckv[0], krc, q3(sckv), q3(skr), wuk, wuv, tk=tk_s)
    hs, xn2s, rts = _merge(sod.reshape(ts, -1), som.reshape(ts, -1), x_sample.reshape(ts, d), wo, norm2_g, wr, br,
                           tm=ts)
    eids = rts[2:4].astype(jnp.int32)
    gates = (jnp.where(eids[0][:, None] == jnp.arange(LANES)[None, :], rts[0][:, None], 0.0)
             + jnp.where(eids[1][:, None] == jnp.arange(LANES)[None, :], rts[1][:, None], 0.0))
    y_sample = _moe_dense(xn2s, hs, gates, wg, wu, wd, gf).reshape(bs, ss, d)

    tm_p = min(INPROJ_ROWS, s)
    tab_p = _rope_table(jnp.arange(s, dtype=jnp.int32))
    (pqdt, p_dk, pkd, p_dv, pvdt, pqmt, p_ckv, pkr, pkm, pvmt) = inproj(
        x_prompt.reshape(b * s, d), tab_p, tm=tm_p, tab_blocks=s // tm_p, batch=b, meta=(mdk32, mdv32, mckv))
    tq = min(ATTN_TILE, s)
    r3 = lambda a: a.reshape(b, s, a.shape[-1])
    od, om = _prompt_attn(pqdt, r3(pkd), pvdt, mdk, _values_t(mdv), lamv, subg.T,
                          pqmt, r3(pkm), pvmt, mkm, _values_t(mvm), tq=tq, lam_init=lam_init)
    t = b * s
    tm_t = min(TOKEN_ROWS, t)
    hp, xn2, rt = _merge(od.reshape(t, -1), om.reshape(t, -1), x_prompt.reshape(t, d), wo, norm2_g, wr, br, tm=tm_t)
    pos, sorted_tok, sorted_w, tile_expert, n_used = _sort_by_expert(rt[2:4].astype(jnp.int32), rt[0:2], tm_t)
    rows = lambda a, idx: a.at[idx].get(mode="promise_in_bounds")
    n_tiles = sorted_tok.shape[0] // tm_t
    cut = (n_tiles // 2) * tm_t
    sw = sorted_w[:, None]
    ys = _moe_sorted(tile_expert, n_used, rows(xn2, sorted_tok[:cut]), sw, wg, wu, wd, tm=tm_t, x_first=0)
    if cut < n_tiles * tm_t:
        ys = _moe_sorted(tile_expert, n_used, rows(xn2, sorted_tok[cut:]), sw, wg, wu, wd, tm=tm_t,
                         x_first=cut // tm_t, y_prev=ys)
    y_prompt = _combine(hp, rows(ys, pos[0]), rows(ys, pos[1]), gf, tm=tm_t).reshape(b, s, d)

    return (y_prompt, y_sample,
            p_dk[None], p_dv[None], p_ckv[None], pkr.reshape(1, b, s, MLA_ROPE),
            s_dk.reshape(1, bs, ss, DA_HEADS, 2 * DA_D), s_dv.reshape(1, bs, ss, DA_HEADS, DA_V),
            sckv.reshape(1, bs, ss, c_kvl), skr.reshape(1, bs, ss, MLA_ROPE))
```

```python
import functools
import math

import numpy as np
import jax
import jax.numpy as jnp
from jax import lax
from jax.experimental import pallas as pl
from jax.experimental.pallas import tpu as pltpu

CHUNK = 64
N_META = 16
EPS = 1e-6
DA_HEADS = 8
DA_D = 64
DA_V = 2 * DA_D
MLA_HEADS = 8
MLA_NOPE = 128
MLA_ROPE = 64
MLA_V = 128
ROPE_THETA = 10000.0
MLA_SCALE = (MLA_NOPE + MLA_ROPE) ** -0.5
N_GROUPS = 4
EXPERTS_PER_GROUP = 4
N_EXPERTS = N_GROUPS * EXPERTS_PER_GROUP
LOG2E = math.log2(math.e)
LANES = 128
SUBLANES = 8
BF16_ROWS = 16
MXU_DIM = 256
HEAD_W = 128
MLA_W = 256
VT_W = DA_V + BF16_ROWS
NEG_BIG = -1e30
VMEM_LIMIT = 56 * 1024 * 1024
ATTN_TILE = 512
SCORE_LOOKAHEAD = 2
INPROJ_ROWS = 256
TOKEN_ROWS = 512
CAST_ROWS = 1024

BF16 = jnp.bfloat16
F32 = jnp.float32


def _dot(a, b):
    return jnp.dot(a, b, preferred_element_type=F32)


def _dot_nt(a, b):
    return lax.dot_general(a, b, (((1,), (1,)), ((), ())), preferred_element_type=F32)


def _rms(x, g):
    return x * lax.rsqrt(jnp.mean(x * x, axis=-1, keepdims=True) + EPS) * g


def _cparams(n_axes):
    return pltpu.CompilerParams(dimension_semantics=("arbitrary",) * n_axes,
                                vmem_limit_bytes=VMEM_LIMIT)


def _const_spec(shape):
    nd = len(shape)
    return pl.BlockSpec(shape, lambda *_: (0,) * nd, pipeline_mode=pl.Buffered(1))


def _alibi_slope(h):
    return 2.0 ** (-8.0 * (h + 1) / DA_HEADS)


def _inproj_kernel(*refs, c_qk, c_v, c_ql, c_kvl, tab_blocks, prompt):
    (x_ref, tab_ref, g1_ref, win_ref, qg_ref, wuq_ref, kvg_ref, wukv_ref), refs = refs[:8], refs[8:]
    if prompt:
        (mk32_ref, mv32_ref, mckv_ref), refs = refs[:3], refs[3:]
    (qd_ref, kd32_ref, kdb_ref, vd32_ref, vdb_ref, qm_ref, ckv_ref, kr_ref, km_ref, vm_ref), refs = refs[:10], refs[10:]
    x = x_ref[...]
    tm = x.shape[0]
    if prompt:
        kbuf, vbuf, cbuf, sem, msem = refs
        i = pl.program_id(0)
        n = pl.num_programs(0)
        slot = i % 2

        def tile_copies(step, s):
            sb = step // tab_blocks
            rows = pl.ds(N_META + (step % tab_blocks) * tm, tm)
            cps = [pltpu.make_async_copy(cbuf.at[s], ckv_ref.at[sb, rows], sem.at[s, 2])]
            for h in range(DA_HEADS):
                cps.append(pltpu.make_async_copy(kbuf.at[s, h], kd32_ref.at[sb, rows, h, :], sem.at[s, 0]))
                cps.append(pltpu.make_async_copy(vbuf.at[s, h], vd32_ref.at[sb, rows, h, :], sem.at[s, 1]))
            return cps

        @pl.when(i >= 2)
        def _slot_free():
            for cp in tile_copies(i - 2, slot):
                cp.wait()

    xn = _rms(x, g1_ref[...]).astype(BF16)
    tab = tab_ref[...]
    o1 = c_qk
    o2 = o1 + c_qk
    o3 = o2 + c_v
    o4 = o3 + c_ql
    o5 = o4 + c_kvl
    ones = jnp.ones((BF16_ROWS, tm), BF16)

    def put_heads32(ref, buf, z):
        for h in range(DA_HEADS):
            if prompt:
                buf[slot, h] = z[:, h * HEAD_W:(h + 1) * HEAD_W]
            else:
                ref[:, h, :] = z[:, h * HEAD_W:(h + 1) * HEAD_W]

    def put_values_t(ref, z):
        for h in range(DA_HEADS):
            ref[h * VT_W:h * VT_W + DA_V, :] = z[:, h * DA_V:(h + 1) * DA_V].T.astype(BF16)
            ref[h * VT_W + DA_V:(h + 1) * VT_W, :] = ones

    zq = _dot(xn, win_ref[:, 0:o1]) * (DA_D ** -0.5 * LOG2E)
    qd_ref[...] = zq.T.astype(BF16) if prompt else zq.astype(BF16)
    zk = _dot(xn, win_ref[:, o1:o2])
    put_heads32(kd32_ref, kbuf if prompt else None, zk)
    kdb_ref[...] = zk.astype(BF16)
    zv = _dot(xn, win_ref[:, o2:o3])
    put_heads32(vd32_ref, vbuf if prompt else None, zv)
    if prompt:
        put_values_t(vdb_ref, zv)
    else:
        vdb_ref[...] = zv.astype(BF16)

    cq = _rms(_dot(xn, win_ref[:, o3:o4]), qg_ref[...]).astype(BF16)
    nq = MLA_HEADS * MLA_NOPE
    qn = _dot(cq, wuq_ref[:, 0:nq]) * (MLA_SCALE * LOG2E)
    qr = _dot(cq, wuq_ref[:, nq:2 * nq]) * (MLA_SCALE * LOG2E)
    for h in range(MLA_HEADS):
        u = qr[:, h * LANES:(h + 1) * LANES] * tab
        rot = u + pltpu.roll(u, MLA_ROPE, 1)
        nope = qn[:, h * LANES:(h + 1) * LANES]
        if prompt:
            qm_ref[h * MLA_W:h * MLA_W + LANES, :] = nope.T.astype(BF16)
            qm_ref[h * MLA_W + LANES:(h + 1) * MLA_W, :] = rot.T.astype(BF16)
        else:
            qm_ref[:, h * MLA_W:h * MLA_W + LANES] = nope.astype(BF16)
            qm_ref[:, h * MLA_W + LANES:(h + 1) * MLA_W] = rot.astype(BF16)

    ckv = _rms(_dot(xn, win_ref[:, o4:o5]), kvg_ref[...])
    if prompt:
        cbuf[slot] = ckv
    else:
        ckv_ref[...] = ckv
    ckvb = ckv.astype(BF16)
    nk = MLA_HEADS * MLA_NOPE
    kn = _dot(ckvb, wukv_ref[:, 0:nk])
    vm = _dot(ckvb, wukv_ref[:, nk:nk + MLA_HEADS * MLA_V])
    if prompt:
        put_values_t(vm_ref, vm)
    else:
        vm_ref[...] = vm.astype(BF16)

    u = _dot(xn, win_ref[:, o5:o5 + LANES]) * tab
    rot = u + pltpu.roll(u, MLA_ROPE, 1)
    kr_ref[...] = rot[:, 0:MLA_ROPE]
    lane = lax.broadcasted_iota(jnp.int32, rot.shape, 1)
    krp = jnp.where(lane < MLA_ROPE, rot, 0.0).astype(BF16)
    for h in range(MLA_HEADS):
        km_ref[:, h * MLA_W:h * MLA_W + LANES] = kn[:, h * LANES:(h + 1) * LANES].astype(BF16)
        km_ref[:, h * MLA_W + LANES:(h + 1) * MLA_W] = krp

    if prompt:
        for cp in tile_copies(i, slot):
            cp.start()

        @pl.when(i % tab_blocks == 0)
        def _meta_rows():
            head = pl.ds(0, N_META)
            sb = i // tab_blocks
            cps = [pltpu.make_async_copy(mk32_ref, kd32_ref.at[sb, head], msem.at[0]),
                   pltpu.make_async_copy(mv32_ref, vd32_ref.at[sb, head], msem.at[1]),
                   pltpu.make_async_copy(mckv_ref, ckv_ref.at[sb, head], msem.at[2])]
            for cp in cps:
                cp.start()
            for cp in cps:
                cp.wait()

        @pl.when(jnp.logical_and(i == n - 1, i >= 1))
        def _drain_previous():
            for cp in tile_copies(i - 1, 1 - slot):
                cp.wait()

        @pl.when(i == n - 1)
        def _drain_last():
            for cp in tile_copies(i, slot):
                cp.wait()


def _inproj(x, tab, g1, win, qg, wuq, kvg, wukv, *, tm, tab_blocks, dims, batch=None, meta=None):
    m, d = x.shape
    c_qk, c_v, c_ql, c_kvl = dims
    assert m % tm == 0
    prompt = batch is not None
    row = lambda w: pl.BlockSpec((tm, w), lambda i: (i, 0))
    hw = DA_HEADS * VT_W
    extra_in, extra_specs, scratch = [], [], []
    if prompt:
        seq = tab_blocks * tm
        assert m == batch * seq
        col = lambda w: pl.BlockSpec((None, w, tm), lambda i: (i // tab_blocks, 0, i % tab_blocks))
        hbm = pl.BlockSpec(memory_space=pl.ANY)
        tall = (batch, N_META + seq)
        extra_in = list(meta)
        extra_specs = [_const_spec(a.shape) for a in meta]
        scratch = [pltpu.VMEM((2, DA_HEADS, tm, HEAD_W), F32), pltpu.VMEM((2, DA_HEADS, tm, HEAD_W), F32),
                   pltpu.VMEM((2, tm, c_kvl), F32), pltpu.SemaphoreType.DMA((2, 3)), pltpu.SemaphoreType.DMA((3,))]
        out_specs = [col(c_qk), hbm, row(c_qk), hbm, col(hw), col(MLA_HEADS * MLA_W), hbm,
                     row(MLA_ROPE), row(MLA_HEADS * MLA_W), col(hw)]
        out_shape = [
            jax.ShapeDtypeStruct((batch, c_qk, seq), BF16),
            jax.ShapeDtypeStruct(tall + (DA_HEADS, HEAD_W), F32),
            jax.ShapeDtypeStruct((m, c_qk), BF16),
            jax.ShapeDtypeStruct(tall + (DA_HEADS, HEAD_W), F32),
            jax.ShapeDtypeStruct((batch, hw, seq), BF16),
            jax.ShapeDtypeStruct((batch, MLA_HEADS * MLA_W, seq), BF16),
            jax.ShapeDtypeStruct(tall + (c_kvl,), F32),
            jax.ShapeDtypeStruct((m, MLA_ROPE), F32),
            jax.ShapeDtypeStruct((m, MLA_HEADS * MLA_W), BF16),
            jax.ShapeDtypeStruct((batch, hw, seq), BF16),
        ]
    else:
        heads32 = pl.BlockSpec((tm, DA_HEADS, HEAD_W), lambda i: (i, 0, 0))
        out_specs = [row(c_qk), heads32, row(c_qk), heads32, row(c_v), row(MLA_HEADS * MLA_W), row(c_kvl),
                     row(MLA_ROPE), row(MLA_HEADS * MLA_W), row(MLA_HEADS * MLA_V)]
        out_shape = [
            jax.ShapeDtypeStruct((m, c_qk), BF16),
            jax.ShapeDtypeStruct((m, DA_HEADS, HEAD_W), F32),
            jax.ShapeDtypeStruct((m, c_qk), BF16),
            jax.ShapeDtypeStruct((m, DA_HEADS, HEAD_W), F32),
            jax.ShapeDtypeStruct((m, c_v), BF16),
            jax.ShapeDtypeStruct((m, MLA_HEADS * MLA_W), BF16),
            jax.ShapeDtypeStruct((m, c_kvl), F32),
            jax.ShapeDtypeStruct((m, MLA_ROPE), F32),
            jax.ShapeDtypeStruct((m, MLA_HEADS * MLA_W), BF16),
            jax.ShapeDtypeStruct((m, MLA_HEADS * MLA_V), BF16),
        ]
    return pl.pallas_call(
        functools.partial(_inproj_kernel, c_qk=c_qk, c_v=c_v, c_ql=c_ql, c_kvl=c_kvl, tab_blocks=tab_blocks,
                          prompt=prompt),
        grid=(m // tm,),
        in_specs=[
            row(d),
            pl.BlockSpec((tm, LANES), lambda i: (i % tab_blocks, 0)),
            _const_spec(g1.shape), _const_spec(win.shape), _const_spec(qg.shape),
            _const_spec(wuq.shape), _const_spec(kvg.shape), _const_spec(wukv.shape),
        ] + extra_specs,
        out_specs=out_specs,
        out_shape=out_shape,
        scratch_shapes=scratch,
        compiler_params=_cparams(1),
        name="inproj",
    )(x, tab, g1, win, qg, wuq, kvg, wukv, *extra_in)


def _softmax_seed(s, v, m_ref, l_ref, acc_ref, j):
    m = jnp.max(s, axis=1, keepdims=True)
    p = jnp.exp2(s - m)
    m_ref[j] = m
    l_ref[j] = jnp.sum(p, axis=1, keepdims=True)
    acc_ref[j] = _dot(p.astype(BF16), v)


def _softmax_step(s, v, m_ref, l_ref, acc_ref, j):
    m_old = m_ref[j]
    m_new = jnp.maximum(m_old, jnp.max(s, axis=1, keepdims=True))
    alpha = jnp.exp2(m_old - m_new)
    p = jnp.exp2(s - m_new)
    l_ref[j] = alpha * l_ref[j] + jnp.sum(p, axis=1, keepdims=True)
    acc_ref[j] = alpha * acc_ref[j] + _dot(p.astype(BF16), v)
    m_ref[j] = m_new


def _diff_lambda(lamv, lam_init):
    a = jnp.sum(lamv[0:1] * lamv[1:2], axis=1, keepdims=True)
    b = jnp.sum(lamv[2:3] * lamv[3:4], axis=1, keepdims=True)
    return jnp.exp(a) - jnp.exp(b) + lam_init


def _split_maps(q):
    lane = lax.broadcasted_iota(jnp.int32, q.shape, 1)
    zero = jnp.zeros_like(q)
    return jnp.where(lane < DA_D, q, zero), jnp.where(lane >= DA_D, q, zero)


def _init_state(m_ref, acc_ref):
    m_ref[...] = jnp.full(m_ref.shape, NEG_BIG, F32)
    acc_ref[...] = jnp.zeros(acc_ref.shape, F32)


def _step_t(st, shift, vt, m_ref, acc_ref, j):
    m_old = m_ref[j]
    m_new = jnp.maximum(m_old, jnp.max(st, axis=0, keepdims=True) + shift)
    p = jnp.exp2(st - (m_new - shift))
    acc_ref[j] = jnp.exp2(m_old - m_new) * acc_ref[j] + _dot(vt, p.astype(BF16))
    m_ref[j] = m_new


def _diff_attn_parts(qi, ki, qt_ref, k_ref, vt_ref, mk_ref, mvt_ref, pos_ref, cq_ref, corr_ref,
                     lamv_ref, g_ref, o_ref, m_s, acc_s, *, tq, lam_init):
    qrow = lax.broadcasted_iota(jnp.int32, (HEAD_W, tq), 0) < DA_D
    klane = lax.broadcasted_iota(jnp.int32, (tq, HEAD_W), 1) < DA_D

    def init():
        _init_state(m_s, acc_s)

    def body(diag):
        qpos = (lax.broadcasted_iota(jnp.int32, (1, tq), 1) + (qi - ki) * tq).astype(F32)
        pos = pos_ref[...]
        mlane = lax.broadcasted_iota(jnp.int32, (N_META, HEAD_W), 1) < DA_D

        def scores(h):
            hs = slice(h * HEAD_W, (h + 1) * HEAD_W)
            qt = qt_ref[hs, :]
            cq = cq_ref[h]
            kk = k_ref[:, hs]
            corr = corr_ref[...] * (_alibi_slope(h) * LOG2E) if diag else None
            mk = mk_ref[:, hs]
            mzero = jnp.zeros_like(mk)
            out = []
            for c in range(2):
                qc = jnp.where(qrow, qt, cq) if c == 0 else jnp.where(qrow, cq, qt)
                kc = jnp.where(klane, kk, pos) if c == 0 else jnp.where(klane, pos, kk)
                if diag:
                    mc = jnp.where(mlane, mk, mzero) if c == 0 else jnp.where(mlane, mzero, mk)
                    kc = jnp.concatenate([kc, mc], axis=0)
                st = _dot(kc, qc)
                out.append(st + corr if diag else st)
            return out

        queue = [scores(h) for h in range(SCORE_LOOKAHEAD)]
        for h in range(DA_HEADS):
            cur = queue.pop(0)
            if h + SCORE_LOOKAHEAD < DA_HEADS:
                queue.append(scores(h + SCORE_LOOKAHEAD))
            shift = qpos * (-(_alibi_slope(h) * LOG2E))
            vt = vt_ref[h * VT_W:(h + 1) * VT_W, :]
            if diag:
                vt = jnp.concatenate([vt, mvt_ref[h * VT_W:(h + 1) * VT_W, :]], axis=1)
            for c in range(2):
                _step_t(cur[c], shift, vt, m_s, acc_s, 2 * h + c)

    def finish():
        lam = _diff_lambda(lamv_ref[...], lam_init)
        g = g_ref[...]
        for h in range(DA_HEADS):
            a0 = acc_s[2 * h]
            a1 = acc_s[2 * h + 1]
            ot = a0[0:DA_V] / a0[DA_V:DA_V + 1] - lam * (a1[0:DA_V] / a1[DA_V:DA_V + 1])
            ot = ot * lax.rsqrt(jnp.mean(ot * ot, axis=0, keepdims=True) + EPS) * g * (1.0 - lam_init)
            o_ref[:, h * HEAD_W:(h + 1) * HEAD_W] = ot.T.astype(BF16)

    return init, body, finish


def _mla_attn_parts(qt_ref, k_ref, vt_ref, mk_ref, mvt_ref, mask_ref, o_ref, m_s, acc_s):
    def init():
        _init_state(m_s, acc_s)

    def body(diag):
        def scores(h):
            kk = k_ref[:, h * MLA_W:(h + 1) * MLA_W]
            if diag:
                kk = jnp.concatenate([kk, mk_ref[:, h * MLA_W:(h + 1) * MLA_W]], axis=0)
            st = _dot(kk, qt_ref[h * MLA_W:(h + 1) * MLA_W, :])
            return st + mask_ref[...] if diag else st

        queue = [scores(h) for h in range(SCORE_LOOKAHEAD)]
        for h in range(MLA_HEADS):
            st = queue.pop(0)
            if h + SCORE_LOOKAHEAD < MLA_HEADS:
                queue.append(scores(h + SCORE_LOOKAHEAD))
            vt = vt_ref[h * VT_W:(h + 1) * VT_W, :]
            if diag:
                vt = jnp.concatenate([vt, mvt_ref[h * VT_W:(h + 1) * VT_W, :]], axis=1)
            _step_t(st, 0.0, vt, m_s, acc_s, h)

    def finish():
        for h in range(MLA_HEADS):
            a = acc_s[h]
            o_ref[:, h * MLA_V:(h + 1) * MLA_V] = (a[0:MLA_V] / a[MLA_V:MLA_V + 1]).T.astype(BF16)

    return init, body, finish


N_DIFF_IN = 10
N_MLA_IN = 6


def _prompt_attn_kernel(qi_ref, ki_ref, *refs, tq, lam_init):
    t = pl.program_id(1)
    qi = qi_ref[t]
    ki = ki_ref[t]
    d_in, refs = refs[:N_DIFF_IN], refs[N_DIFF_IN:]
    m_in, refs = refs[:N_MLA_IN], refs[N_MLA_IN:]
    d_out, m_out, d_m, d_acc, m_m, m_acc = refs
    d_init, d_body, d_finish = _diff_attn_parts(qi, ki, *d_in, d_out, d_m, d_acc, tq=tq, lam_init=lam_init)
    m_init, m_body, m_finish = _mla_attn_parts(*m_in, m_out, m_m, m_acc)

    @pl.when(ki == 0)
    def _init():
        d_init()
        m_init()

    @pl.when(ki != qi)
    def _off_diagonal():
        d_body(False)
        m_body(False)

    @pl.when(ki == qi)
    def _diagonal():
        d_body(True)
        m_body(True)
        d_finish()
        m_finish()


def _pair_tables(nq):
    qi = np.concatenate([np.full((i + 1,), i, np.int32) for i in range(nq)])
    ki = np.concatenate([np.arange(i + 1, dtype=np.int32) for i in range(nq)])
    return jnp.asarray(qi), jnp.asarray(ki)


def _tile_geometry(tq):
    j = np.arange(tq)[:, None]
    i = np.arange(tq)[None, :]
    visible = (j // CHUNK) <= (i // CHUNK)
    return i, j, visible


def _bf16_split3(x):
    parts = []
    for _ in range(3):
        p = float(np.asarray(x, np.float32).astype(BF16).astype(np.float32))
        parts.append(p)
        x = x - p
    return parts


def _alibi_operands(tq):
    assert tq <= 2 * MXU_DIM
    j = np.arange(tq)
    jlo = (j % MXU_DIM).astype(np.float32)
    jhi = (j - j % MXU_DIM).astype(np.float32)
    pos = np.zeros((tq, HEAD_W), np.float32)
    cq = np.zeros((DA_HEADS, HEAD_W, tq), np.float32)
    for base in (0, DA_D):
        for r in range(3):
            pos[:, base + 2 * r] = jlo
            pos[:, base + 2 * r + 1] = jhi
    for h in range(DA_HEADS):
        parts = _bf16_split3(_alibi_slope(h) * LOG2E)
        for base in (0, DA_D):
            for r in range(3):
                cq[h, base + 2 * r, :] = parts[r]
                cq[h, base + 2 * r + 1, :] = parts[r]
    return jnp.asarray(pos, BF16), jnp.asarray(cq, BF16)


def _prompt_attn_specs(tq, wq, wk, wv):
    qt_spec = pl.BlockSpec((None, wq, tq), lambda b, t, qi, ki: (b, 0, qi[t]))
    k_spec = pl.BlockSpec((None, tq, wk), lambda b, t, qi, ki: (b, ki[t], 0))
    vt_spec = pl.BlockSpec((None, wv, tq), lambda b, t, qi, ki: (b, 0, ki[t]))
    mk_spec = pl.BlockSpec((N_META, wk), lambda b, t, qi, ki: (0, 0))
    mvt_spec = pl.BlockSpec((wv, N_META), lambda b, t, qi, ki: (0, 0))
    return qt_spec, k_spec, vt_spec, mk_spec, mvt_spec


def _prompt_attn(qt, k, vt, mk, mvt, lamv, g, m_qt, m_k, m_vt, m_mk, m_mvt, *, tq, lam_init):
    b, s, w = k.shape
    nq = s // tq
    qi, ki = _pair_tables(nq)
    i, j, visible = _tile_geometry(tq)
    corr = np.where(visible, np.where(j > i, -2.0 * (j - i), 0.0), NEG_BIG)
    corr = jnp.asarray(np.concatenate([corr, np.broadcast_to(i, (N_META, tq))]).astype(np.float32))
    pos, cq = _alibi_operands(tq)
    mask = np.concatenate([np.where(visible, 0.0, NEG_BIG), np.zeros((N_META, tq))])
    mask = jnp.asarray(mask.astype(np.float32))
    full = lambda a: pl.BlockSpec(a.shape, lambda b_, t, qi_, ki_: (0,) * a.ndim)
    d_args = [qt, k, vt, mk, mvt, pos, cq, corr, lamv, g]
    d_specs = list(_prompt_attn_specs(tq, w, w, vt.shape[1])) + [full(a) for a in d_args[5:]]
    m_args = [m_qt, m_k, m_vt, m_mk, m_mvt, mask]
    wq = m_k.shape[2]
    m_specs = list(_prompt_attn_specs(tq, wq, wq, m_vt.shape[1])) + [full(mask)]
    assert len(d_args) == N_DIFF_IN and len(m_args) == N_MLA_IN
    wo = MLA_HEADS * MLA_V
    out_spec = lambda width: pl.BlockSpec((None, tq, width), lambda b_, t, qi_, ki_: (b_, qi_[t], 0))
    return pl.pallas_call(
        functools.partial(_prompt_attn_kernel, tq=tq, lam_init=lam_init),
        grid_spec=pltpu.PrefetchScalarGridSpec(
            num_scalar_prefetch=2,
            grid=(b, int(qi.shape[0])),
            in_specs=d_specs + m_specs,
            out_specs=[out_spec(w), out_spec(wo)],
            scratch_shapes=[pltpu.VMEM((2 * DA_HEADS, 1, tq), F32), pltpu.VMEM((2 * DA_HEADS, VT_W, tq), F32),
                            pltpu.VMEM((MLA_HEADS, 1, tq), F32), pltpu.VMEM((MLA_HEADS, VT_W, tq), F32)],
        ),
        out_shape=[jax.ShapeDtypeStruct((b, s, w), BF16), jax.ShapeDtypeStruct((b, s, wo), BF16)],
        compiler_params=_cparams(2),
        name="prompt_attn",
    )(qi, ki, *d_args, *m_args)


def _sample_diff_kernel(q_ref, kc_hbm, vc_hbm, kn_ref, vn_ref, dc_ref, dn_ref, lamv_ref, g_ref,
                        o_ref, kbuf, vbuf, sem, m_s, l_s, acc_s, *, tk, lam_init):
    kt = pl.program_id(1)
    nkt = pl.num_programs(1)
    step = pl.program_id(0) * nkt + kt
    n_steps = pl.num_programs(0) * nkt

    def tile_copies(s, slot):
        sb = s // nkt
        rows = pl.ds((s % nkt) * tk, tk)
        cps = []
        for h in range(DA_HEADS):
            cps.append(pltpu.make_async_copy(kc_hbm.at[sb, rows, h, :], kbuf.at[slot, h], sem.at[slot, 0]))
            cps.append(pltpu.make_async_copy(vc_hbm.at[sb, rows, h, :], vbuf.at[slot, h], sem.at[slot, 1]))
        return cps

    @pl.when(step == 0)
    def _prime():
        for cp in tile_copies(0, 0):
            cp.start()

    @pl.when(step + 1 < n_steps)
    def _prefetch():
        for cp in tile_copies(step + 1, (step + 1) % 2):
            cp.start()

    slot = step % 2
    for cp in tile_copies(step, slot):
        cp.wait()

    def heads(get_k, get_v, dist, first):
        for h in range(DA_HEADS):
            q1, q2 = _split_maps(q_ref[:, h * HEAD_W:(h + 1) * HEAD_W])
            qq = jnp.concatenate([q1, q2], axis=0)
            s = _dot_nt(qq, get_k(h)) + dist * (-_alibi_slope(h) * LOG2E)
            if first:
                _softmax_seed(s, get_v(h), m_s, l_s, acc_s, h)
            else:
                _softmax_step(s, get_v(h), m_s, l_s, acc_s, h)

    cache_k = lambda h: kbuf[slot, h].astype(BF16)
    cache_v = lambda h: vbuf[slot, h].astype(BF16)

    @pl.when(kt == 0)
    def _first():
        heads(cache_k, cache_v, dc_ref[...], True)

    @pl.when(kt > 0)
    def _rest():
        heads(cache_k, cache_v, dc_ref[...], False)

    @pl.when(kt == nkt - 1)
    def _finish():
        heads(lambda h: kn_ref[:, h * HEAD_W:(h + 1) * HEAD_W], lambda h: vn_ref[:, h * HEAD_W:(h + 1) * HEAD_W],
              dn_ref[...], False)
        lam = _diff_lambda(lamv_ref[...], lam_init)
        g = g_ref[...]
        nq = q_ref.shape[0]
        for h in range(DA_HEADS):
            a = acc_s[h] / l_s[h]
            o = a[0:nq] - lam * a[nq:2 * nq]
            o_ref[:, h * HEAD_W:(h + 1) * HEAD_W] = (_rms(o, g) * (1.0 - lam_init)).astype(BF16)


def _sample_diff_attn(q, kc, vc, kn, vn, dist_c, dist_n, lamv, g, *, tk, lam_init):
    bs, nq, w = q.shape
    lc = kc.shape[1]
    assert lc % tk == 0
    full = lambda a: pl.BlockSpec(a.shape, lambda b, t: (0,) * a.ndim)
    per_stream = lambda a: pl.BlockSpec((None,) + a.shape[1:], lambda b, t: (b,) + (0,) * (a.ndim - 1))
    cache = pl.BlockSpec(memory_space=pl.ANY)
    return pl.pallas_call(
        functools.partial(_sample_diff_kernel, tk=tk, lam_init=lam_init),
        grid=(bs, lc // tk),
        in_specs=[per_stream(q), cache, cache, per_stream(kn), per_stream(vn),
                  pl.BlockSpec((None, 2 * nq, tk), lambda b, t: (t, 0, 0)),
                  full(dist_n), full(lamv), full(g)],
        out_specs=per_stream(q),
        out_shape=jax.ShapeDtypeStruct((bs, nq, w), BF16),
        scratch_shapes=[pltpu.VMEM((2, DA_HEADS, tk, HEAD_W), F32), pltpu.VMEM((2, DA_HEADS, tk, HEAD_W), F32),
                        pltpu.SemaphoreType.DMA((2, 2)),
                        pltpu.VMEM((DA_HEADS, 2 * nq, 1), F32), pltpu.VMEM((DA_HEADS, 2 * nq, 1), F32),
                        pltpu.VMEM((DA_HEADS, 2 * nq, DA_V), F32)],
        compiler_params=_cparams(2),
        name="sample_diff_attn",
    )(q, kc, vc, kn, vn, dist_c, dist_n, lamv, g)


def _sample_mla_kernel(q_ref, cc_ref, krc_ref, cn_ref, krn_ref, wuk_ref, wuv_ref, o_ref,
                       ql_s, qr_s, m_s, l_s, acc_s):
    kt = pl.program_id(1)
    nkt = pl.num_programs(1)
    nq = q_ref.shape[0]

    @pl.when(kt == 0)
    def _prep():
        for h in range(MLA_HEADS):
            qn = q_ref[:, h * MLA_W:h * MLA_W + MLA_NOPE]
            ql_s[h * nq:(h + 1) * nq, :] = _dot_nt(qn, wuk_ref[:, h * MLA_NOPE:(h + 1) * MLA_NOPE]).astype(BF16)
            qr_s[h * nq:(h + 1) * nq, :] = q_ref[:, h * MLA_W + MLA_NOPE:(h + 1) * MLA_W]

    def scores(c_ref, kr_ref):
        cb = c_ref[...].astype(BF16)
        krb = kr_ref[...].astype(BF16)
        s = _dot_nt(ql_s[...], cb) + _dot_nt(qr_s[:, 0:MLA_ROPE], krb)
        return s, cb

    @pl.when(kt == 0)
    def _first():
        s, cb = scores(cc_ref, krc_ref)
        _softmax_seed(s, cb, m_s, l_s, acc_s, 0)

    @pl.when(kt > 0)
    def _rest():
        s, cb = scores(cc_ref, krc_ref)
        _softmax_step(s, cb, m_s, l_s, acc_s, 0)

    @pl.when(kt == nkt - 1)
    def _finish():
        s, cb = scores(cn_ref, krn_ref)
        _softmax_step(s, cb, m_s, l_s, acc_s, 0)
        ol = (acc_s[0] / l_s[0]).astype(BF16)
        for h in range(MLA_HEADS):
            o_ref[:, h * MLA_V:(h + 1) * MLA_V] = _dot(
                ol[h * nq:(h + 1) * nq, :], wuv_ref[:, h * MLA_V:(h + 1) * MLA_V]).astype(BF16)


def _sample_mla_attn(q, cc, krc, cn, krn, wuk, wuv, *, tk):
    bs, nq, wq = q.shape
    lc, kvl = cc.shape[1], cc.shape[2]
    assert lc % tk == 0
    full = lambda a: pl.BlockSpec(a.shape, lambda b, t: (0,) * a.ndim)
    per_stream = lambda a: pl.BlockSpec((None,) + a.shape[1:], lambda b, t: (b,) + (0,) * (a.ndim - 1))
    rows = MLA_HEADS * nq
    return pl.pallas_call(
        _sample_mla_kernel,
        grid=(bs, lc // tk),
        in_specs=[per_stream(q),
                  pl.BlockSpec((None, tk, kvl), lambda b, t: (b, t, 0)),
                  pl.BlockSpec((None, tk, MLA_ROPE), lambda b, t: (b, t, 0)),
                  per_stream(cn), per_stream(krn), full(wuk), full(wuv)],
        out_specs=pl.BlockSpec((None, nq, MLA_HEADS * MLA_V), lambda b, t: (b, 0, 0)),
        out_shape=jax.ShapeDtypeStruct((bs, nq, MLA_HEADS * MLA_V), BF16),
        scratch_shapes=[pltpu.VMEM((rows, kvl), BF16), pltpu.VMEM((rows, LANES), BF16),
                        pltpu.VMEM((1, rows, 1), F32), pltpu.VMEM((1, rows, 1), F32),
                        pltpu.VMEM((1, rows, kvl), F32)],
        compiler_params=_cparams(2),
        name="sample_mla_attn",
    )(q, cc, krc, cn, krn, wuk, wuv)


ROUTER_ROWS = SUBLANES * (1 + N_GROUPS)


def _route(lt):
    g = [lt[i:i + 1] for i in range(N_GROUPS)]
    gmax = functools.reduce(jnp.maximum, g)
    gidx = jnp.full_like(gmax, float(N_GROUPS - 1))
    for i in range(N_GROUPS - 2, -1, -1):
        gidx = jnp.where(g[i] == gmax, float(i), gidx)
    den = functools.reduce(lambda a, b: a + b, [jnp.exp(gi - gmax) for gi in g])
    p_top = 1.0 / den
    e = []
    for j in range(EXPERTS_PER_GROUP):
        ej = lt[SUBLANES * N_GROUPS + j:SUBLANES * N_GROUPS + j + 1]
        for grp in range(N_GROUPS - 2, -1, -1):
            ej = jnp.where(gidx == float(grp), lt[SUBLANES * (grp + 1) + j:SUBLANES * (grp + 1) + j + 1], ej)
        e.append(ej)

    def first_argmax(vals):
        vmax = functools.reduce(jnp.maximum, vals)
        idx = jnp.full_like(vmax, float(len(vals) - 1))
        for i in range(len(vals) - 2, -1, -1):
            idx = jnp.where(vals[i] == vmax, float(i), idx)
        return vmax, idx

    v1, i1 = first_argmax(e)
    rest = [jnp.where(i1 == float(j), -jnp.inf, e[j]) for j in range(EXPERTS_PER_GROUP)]
    v2, i2 = first_argmax(rest)
    r = jnp.exp(v2 - v1)
    w1 = p_top / (1.0 + r)
    w2 = p_top * r / (1.0 + r)
    base = gidx * float(EXPERTS_PER_GROUP)
    return w1, w2, base + i1, base + i2


def _merge_kernel(od_ref, om_ref, x_ref, wo_ref, g2_ref, wr_ref, br_ref, hp_ref, xn_ref, rt_ref):
    nd = od_ref.shape[1]
    y = _dot(od_ref[...], wo_ref[0:nd, :]) + _dot(om_ref[...], wo_ref[nd:, :])
    hp = x_ref[...] + y
    hp_ref[...] = hp
    xn = _rms(hp, g2_ref[...]).astype(BF16)
    xn_ref[...] = xn
    lt = _dot_nt(wr_ref[...], xn) + br_ref[...]
    rows = _route(lt)
    for i, r in enumerate(rows):
        rt_ref[i:i + 1, :] = r
    rt_ref[4:8, :] = jnp.zeros((4, rt_ref.shape[1]), F32)


def _merge(od, om, x, wo, g2, wr, br, *, tm):
    m, d = x.shape
    row = lambda w: pl.BlockSpec((tm, w), lambda i: (i, 0))
    return pl.pallas_call(
        _merge_kernel,
        grid=(m // tm,),
        in_specs=[row(od.shape[1]), row(om.shape[1]), row(d), _const_spec(wo.shape),
                  _const_spec(g2.shape), _const_spec(wr.shape), _const_spec(br.shape)],
        out_specs=[row(d), row(d), pl.BlockSpec((SUBLANES, tm), lambda i: (0, i))],
        out_shape=[jax.ShapeDtypeStruct((m, d), F32), jax.ShapeDtypeStruct((m, d), BF16),
                   jax.ShapeDtypeStruct((SUBLANES, m), F32)],
        compiler_params=_cparams(1),
        name="merge",
    )(od, om, x, wo, g2, wr, br)


def _cast_kernel(x_ref, o_ref):
    o_ref[...] = x_ref[...].astype(BF16)


def _cast_bf16(w, *, rows):
    e, r, c = w.shape
    assert r % rows == 0
    spec = pl.BlockSpec((None, rows, c), lambda i, j: (i, j, 0))
    return pl.pallas_call(
        _cast_kernel,
        grid=(e, r // rows),
        in_specs=[spec],
        out_specs=spec,
        out_shape=jax.ShapeDtypeStruct(w.shape, BF16),
        compiler_params=_cparams(2),
        name="cast_bf16",
    )(w)


def _swiglu(x, wg_ref, wu_ref, wd_ref):
    g = _dot(x, wg_ref[...])
    u = _dot(x, wu_ref[...])
    h = (g * jax.nn.sigmoid(g) * u).astype(BF16)
    return _dot(h, wd_ref[...])


def _moe_sorted_kernel(te_ref, nu_ref, x_ref, w_ref, wg_ref, wu_ref, wd_ref, *rest, out_first, x_first, x_tiles):
    y_ref = rest[-1]
    g = pl.program_id(0) + out_first
    live = jnp.logical_and(g < nu_ref[0], jnp.logical_and(g >= x_first, g < x_first + x_tiles))

    @pl.when(live)
    def _():
        y_ref[...] = (w_ref[...] * _swiglu(x_ref[...], wg_ref, wu_ref, wd_ref)).astype(BF16)

    @pl.when(jnp.logical_not(live))
    def _():
        y_ref[...] = jnp.zeros_like(y_ref)


def _moe_sorted(tile_expert, n_used, xs, ws, wg, wu, wd, *, tm, x_first, y_prev=None):
    n_x, d = xs.shape
    dh = d
    f = wg.shape[2]
    x_tiles = n_x // tm
    n_tiles = ws.shape[0] // tm
    out_first = 0 if y_prev is None else x_first
    grid_tiles = n_tiles if y_prev is None else x_tiles
    local = lambda i: jnp.clip(i + out_first - x_first, 0, x_tiles - 1)
    in_specs = [pl.BlockSpec((tm, dh), lambda i, te, nu: (local(i), 0)),
                pl.BlockSpec((tm, 1), lambda i, te, nu: (i + out_first, 0)),
                pl.BlockSpec((None, d, f), lambda i, te, nu: (te[local(i) + x_first], 0, 0)),
                pl.BlockSpec((None, d, f), lambda i, te, nu: (te[local(i) + x_first], 0, 0)),
                pl.BlockSpec((None, f, d), lambda i, te, nu: (te[local(i) + x_first], 0, 0))]
    args = [tile_expert, n_used, xs, ws, wg, wu, wd]
    aliases = {}
    if y_prev is not None:
        in_specs.append(pl.BlockSpec(memory_space=pl.ANY))
        args.append(y_prev)
        aliases = {len(args) - 1: 0}
    return pl.pallas_call(
        functools.partial(_moe_sorted_kernel, out_first=out_first, x_first=x_first, x_tiles=x_tiles),
        grid_spec=pltpu.PrefetchScalarGridSpec(
            num_scalar_prefetch=2,
            grid=(grid_tiles,),
            in_specs=in_specs,
            out_specs=pl.BlockSpec((tm, dh), lambda i, te, nu: (i + out_first, 0)),
        ),
        out_shape=jax.ShapeDtypeStruct((n_tiles * tm, dh), BF16),
        input_output_aliases=aliases,
        compiler_params=_cparams(1),
        name="moe_sorted",
    )(*args)


def _moe_dense_kernel(x_ref, hp_ref, gates_ref, wg_ref, wu_ref, wd_ref, gf_ref, o_ref, acc_s):
    e = pl.program_id(0)

    @pl.when(e == 0)
    def _():
        acc_s[...] = jnp.zeros_like(acc_s)

    lane = lax.broadcasted_iota(jnp.int32, gates_ref.shape, 1)
    gate = jnp.sum(jnp.where(lane == e, gates_ref[...], 0.0), axis=1, keepdims=True)
    acc_s[...] += gate * _swiglu(x_ref[...], wg_ref, wu_ref, wd_ref)

    @pl.when(e == pl.num_programs(0) - 1)
    def _():
        o_ref[...] = _rms(hp_ref[...] + acc_s[...], gf_ref[...])


def _moe_dense(xn, hp, gates, wg, wu, wd, gf):
    m, d = hp.shape
    ne, _, f = wg.shape
    full = lambda a: pl.BlockSpec(a.shape, lambda e: (0,) * a.ndim)
    return pl.pallas_call(
        _moe_dense_kernel,
        grid=(ne,),
        in_specs=[full(xn), full(hp), full(gates),
                  pl.BlockSpec((None, d, f), lambda e: (e, 0, 0)),
                  pl.BlockSpec((None, d, f), lambda e: (e, 0, 0)),
                  pl.BlockSpec((None, f, d), lambda e: (e, 0, 0)),
                  full(gf)],
        out_specs=full(hp),
        out_shape=jax.ShapeDtypeStruct((m, d), F32),
        scratch_shapes=[pltpu.VMEM((m, d), F32)],
        compiler_params=_cparams(1),
        name="moe_dense",
    )(xn, hp, gates, wg, wu, wd, gf)


def _combine_kernel(hp_ref, y1_ref, y2_ref, gf_ref, o_ref):
    o_ref[...] = _rms(hp_ref[...] + (y1_ref[...].astype(F32) + y2_ref[...].astype(F32)), gf_ref[...])


def _combine(hp, y1, y2, gf, *, tm):
    m, d = hp.shape
    row = pl.BlockSpec((tm, d), lambda i: (i, 0))
    return pl.pallas_call(
        _combine_kernel,
        grid=(m // tm,),
        in_specs=[row, row, row, _const_spec(gf.shape)],
        out_specs=row,
        out_shape=jax.ShapeDtypeStruct((m, d), F32),
        compiler_params=_cparams(1),
        name="combine",
    )(hp, y1, y2, gf)


def _rope_table(pos):
    half = MLA_ROPE // 2
    inv_freq = ROPE_THETA ** (-jnp.arange(half, dtype=F32) / half)
    ang = pos.astype(F32)[:, None] * inv_freq[None, :]
    c, s = jnp.cos(ang), jnp.sin(ang)
    return jnp.concatenate([c, c, -s, s], axis=1)


def _swap_halves(w):
    half = MLA_ROPE // 2
    return jnp.concatenate([w[..., half:], w[..., :half]], axis=-1)


def _values_t(v):
    n = v.shape[0]
    vt = v.reshape(n, DA_HEADS, DA_V).transpose(1, 2, 0)
    return jnp.concatenate([vt, jnp.ones((DA_HEADS, BF16_ROWS, n), v.dtype)], axis=1).reshape(DA_HEADS * VT_W, n)


def _sort_by_expert(eid, w, tm):
    t = eid.shape[1]
    flat_e = eid.reshape(-1)
    onehot = (flat_e[:, None] == jnp.arange(N_EXPERTS, dtype=jnp.int32)[None, :]).astype(jnp.int32)
    rank = jnp.sum((jnp.cumsum(onehot, axis=0) - onehot) * onehot, axis=1)
    counts = jnp.sum(onehot, axis=0)
    tiles_per = (counts + tm - 1) // tm
    tiles_end = jnp.cumsum(tiles_per)
    row_start = (tiles_end - tiles_per) * tm
    pos = row_start[flat_e] + rank
    n_tiles = (2 * t) // tm + N_EXPERTS
    slot_a = jnp.full((n_tiles * tm,), -1, jnp.int32).at[pos].set(jnp.arange(2 * t, dtype=jnp.int32),
                                                                   unique_indices=True, mode="promise_in_bounds")
    used = slot_a >= 0
    safe_a = jnp.maximum(slot_a, 0)
    sorted_tok = jnp.where(used, safe_a % t, 0)
    sorted_w = jnp.where(used, w.reshape(-1).at[safe_a].get(mode="promise_in_bounds"), 0.0)
    tile_ids = jnp.arange(n_tiles, dtype=jnp.int32)
    tile_expert = jnp.minimum(jnp.sum((tiles_end[None, :] <= tile_ids[:, None]).astype(jnp.int32), axis=1),
                              N_EXPERTS - 1)
    n_used = tiles_end[-1:].astype(jnp.int32)
    return pos.reshape(2, t), sorted_tok, sorted_w, tile_expert, n_used


def kernel(x_prompt, x_sample, cache_diff_k, cache_diff_v, cache_mla_ckv, cache_mla_kr, meta_tokens, norm1_g, w_in, diff_lam_q1, diff_lam_k1, diff_lam_q2, diff_lam_k2, diff_subln_g, mla_q_norm_g, mla_w_uq, mla_kv_norm_g, mla_w_uk, mla_w_uv, w_o, norm2_g, router_group_w, router_group_b, router_expert_w, router_expert_b, expert_w_gate, expert_w_up, expert_w_down, final_norm_g):
    depth = norm1_g.shape[0]
    assert depth == 1, "single-layer step only"
    assert MLA_HEADS == DA_HEADS and MLA_V == DA_V
    lam_init = 0.8 - 0.6 * math.exp(-0.3 * 0)
    b, s, d = x_prompt.shape
    bs, ss, _ = x_sample.shape
    past = cache_mla_kr.shape[2]
    lc = N_META + past
    c_qk = DA_HEADS * 2 * DA_D
    c_v = DA_HEADS * DA_V
    c_ql = mla_q_norm_g.shape[1]
    c_kvl = mla_kv_norm_g.shape[1]
    o5 = 2 * c_qk + c_v + c_ql + c_kvl

    win = w_in[0]
    win_ext = jnp.concatenate([win, _swap_halves(win[:, o5:])], axis=1).astype(BF16)
    wuq = mla_w_uq[0].reshape(c_ql, MLA_HEADS, MLA_NOPE + MLA_ROPE)
    wuq_n = wuq[:, :, :MLA_NOPE].reshape(c_ql, MLA_HEADS * MLA_NOPE)
    wuq_r = jnp.concatenate([wuq[:, :, MLA_NOPE:], _swap_halves(wuq[:, :, MLA_NOPE:])], axis=2)
    wuq_ext = jnp.concatenate([wuq_n, wuq_r.reshape(c_ql, MLA_HEADS * LANES)], axis=1).astype(BF16)
    wuk = mla_w_uk[0].astype(BF16)
    wuv = mla_w_uv[0].astype(BF16)
    wukv = jnp.concatenate([wuk, wuv], axis=1)
    wo = w_o[0].astype(BF16)
    wr = jnp.zeros((ROUTER_ROWS, d), F32).at[0:N_GROUPS].set(router_group_w[0].T)
    br = jnp.zeros((ROUTER_ROWS, 1), F32).at[0:N_GROUPS, 0].set(router_group_b[0])
    rew = router_expert_w[0].T.reshape(N_GROUPS, EXPERTS_PER_GROUP, d)
    reb = router_expert_b[0].reshape(N_GROUPS, EXPERTS_PER_GROUP)
    for grp in range(N_GROUPS):
        wr = wr.at[SUBLANES * (grp + 1):SUBLANES * (grp + 1) + EXPERTS_PER_GROUP].set(rew[grp])
        br = br.at[SUBLANES * (grp + 1):SUBLANES * (grp + 1) + EXPERTS_PER_GROUP, 0].set(reb[grp])
    wr = wr.astype(BF16)
    wg = _cast_bf16(expert_w_gate[0], rows=min(CAST_ROWS, expert_w_gate.shape[2]))
    wu = _cast_bf16(expert_w_up[0], rows=min(CAST_ROWS, expert_w_up.shape[2]))
    wd = _cast_bf16(expert_w_down[0], rows=min(CAST_ROWS, expert_w_down.shape[2]))
    gf = final_norm_g[None, :]
    lamv = jnp.stack([diff_lam_q1[0], diff_lam_k1[0], diff_lam_q2[0], diff_lam_k2[0]])
    subg = diff_subln_g

    dims = (c_qk, c_v, c_ql, c_kvl)
    inproj = functools.partial(_inproj, g1=norm1_g, win=win_ext, qg=mla_q_norm_g, wuq=wuq_ext,
                               kvg=mla_kv_norm_g, wukv=wukv, dims=dims)

    (_, mdk32, mdk, mdv32, mdv, _, mckv, _, mkm, mvm) = inproj(
        meta_tokens, jnp.zeros((N_META, LANES), F32), tm=N_META, tab_blocks=1)

    ts = bs * ss
    s_pos = past + jnp.arange(ss, dtype=jnp.int32)
    (sqd, s_dk, skd, s_dv, svd, sqm, sckv, skr, _, _) = inproj(
        x_sample.reshape(ts, d), _rope_table(s_pos), tm=ss, tab_blocks=1)
    q3 = lambda a: a.reshape(bs, ss, a.shape[-1])
    kpos_c = np.arange(lc) - N_META
    dist_c = np.where(kpos_c[None, :] >= 0, np.abs(past + np.arange(ss)[:, None] - kpos_c[None, :]), 0)
    dist_n = np.abs(np.arange(ss)[:, None] - np.arange(ss)[None, :])
    tk_s = lc // 2 if (lc // 2) % SUBLANES == 0 and lc % 2 == 0 else lc
    dist_c = np.tile(dist_c, (2, 1)).astype(np.float32).reshape(2 * ss, lc // tk_s, tk_s)
    dist_c = jnp.asarray(np.moveaxis(dist_c, 1, 0))
    dist_n = jnp.asarray(np.tile(dist_n, (2, 1)).astype(np.float32))
    sod = _sample_diff_attn(q3(sqd), cache_diff_k[0], cache_diff_v[0], q3(skd), q3(svd), dist_c, dist_n, lamv, subg,
                            tk=tk_s, lam_init=lam_init)
    krc = jnp.concatenate([jnp.zeros((bs, N_META, MLA_ROPE), F32), cache_mla_kr[0]], axis=1)
    som = _sample_mla_attn(q3(sqm), cache_mla_ckv[0], krc, q3(sckv), q3(skr), wuk, wuv, tk=tk_s)
    hs, xn2s, rts = _merge(sod.reshape(ts, -1), som.reshape(ts, -1), x_sample.reshape(ts, d), wo, norm2_g, wr, br,
                           tm=ts)
    eids = rts[2:4].astype(jnp.int32)
    gates = (jnp.where(eids[0][:, None] == jnp.arange(LANES)[None, :], rts[0][:, None], 0.0)
             + jnp.where(eids[1][:, None] == jnp.arange(LANES)[None, :], rts[1][:, None], 0.0))
    y_sample = _moe_dense(xn2s, hs, gates, wg, wu, wd, gf).reshape(bs, ss, d)

    tm_p = min(INPROJ_ROWS, s)
    tab_p = _rope_table(jnp.arange(s, dtype=jnp.int32))
    (pqdt, p_dk, pkd, p_dv, pvdt, pqmt, p_ckv, pkr, pkm, pvmt) = inproj(
        x_prompt.reshape(b * s, d), tab_p, tm=tm_p, tab_blocks=s // tm_p, batch=b, meta=(mdk32, mdv32, mckv))
    tq = min(ATTN_TILE, s)
    r3 = lambda a: a.reshape(b, s, a.shape[-1])
    od, om = _prompt_attn(pqdt, r3(pkd), pvdt, mdk, _values_t(mdv), lamv, subg.T,
                          pqmt, r3(pkm), pvmt, mkm, _values_t(mvm), tq=tq, lam_init=lam_init)
    t = b * s
    tm_t = min(TOKEN_ROWS, t)
    hp, xn2, rt = _merge(od.reshape(t, -1), om.reshape(t, -1), x_prompt.reshape(t, d), wo, norm2_g, wr, br, tm=tm_t)
    pos, sorted_tok, sorted_w, tile_expert, n_used = _sort_by_expert(rt[2:4].astype(jnp.int32), rt[0:2], tm_t)
    rows = lambda a, idx: a.at[idx].get(mode="promise_in_bounds")
    n_tiles = sorted_tok.shape[0] // tm_t
    cut = (n_tiles // 2) * tm_t
    sw = sorted_w[:, None]
    ys = _moe_sorted(tile_expert, n_used, rows(xn2, sorted_tok[:cut]), sw, wg, wu, wd, tm=tm_t, x_first=0)
    if cut < n_tiles * tm_t:
        ys = _moe_sorted(tile_expert, n_used, rows(xn2, sorted_tok[cut:]), sw, wg, wu, wd, tm=tm_t,
                         x_first=cut // tm_t, y_prev=ys)
    y_prompt = _combine(hp, rows(ys, pos[0]), rows(ys, pos[1]), gf, tm=tm_t).reshape(b, s, d)

    return (y_prompt, y_sample,
            p_dk[None], p_dv[None], p_ckv[None], pkr.reshape(1, b, s, MLA_ROPE),
            s_dk.reshape(1, bs, ss, DA_HEADS, 2 * DA_D), s_dv.reshape(1, bs, ss, DA_HEADS, DA_V),
            sckv.reshape(1, bs, ss, c_kvl), skr.reshape(1, bs, ss, MLA_ROPE))
```
